```python
import math
import jax, jax.numpy as jnp
from jax import lax
import numpy as np

D_MODEL = 1024
BATCH = 2
SEQ = 8192
DEPTH = 1

EPS = 1e-6
Q_BLOCK = 128
MLA_HEADS = 8
MLA_NOPE_DIM = 64
MLA_ROPE_DIM = 32
MLA_V_DIM = 64
Q_LORA_RANK = 256
KV_LORA_RANK = 128
ROPE_THETA = 10000.0
MLA_QK_DIM = MLA_NOPE_DIM + MLA_ROPE_DIM
SB_HEADS = 8
SB_HEAD_DIM = 64
MLA_WIDTH = MLA_HEADS * MLA_V_DIM
SB_WIDTH = SB_HEADS * SB_HEAD_DIM
MIX_WIDTH = MLA_WIDTH + SB_WIDTH
IN_SPLITS = (Q_LORA_RANK, KV_LORA_RANK, MLA_ROPE_DIM, SB_WIDTH, SB_WIDTH, SB_WIDTH)
IN_PROJ_WIDTH = sum(IN_SPLITS)
IN_SPLIT_POINTS = tuple(int(v) for v in np.cumsum(IN_SPLITS)[:-1])
D_FF = ((8 * D_MODEL + 3 * 256 - 1) // (3 * 256)) * 256

kernel_name = "hymba_mla_stickbreaking_swiglu"


def rmsnorm(x, g):
    xf = x.astype(jnp.float32)
    y = xf * lax.rsqrt(jnp.mean(xf * xf, axis=-1, keepdims=True) + EPS)
    return (y * g.astype(jnp.float32)).astype(x.dtype)


def rope_tables(positions, dim):
    inv_freq = ROPE_THETA ** (-jnp.arange(0, dim, 2, dtype=jnp.float32) / dim)
    ang = positions.astype(jnp.float32)[:, :, None] * inv_freq[None, None, :]
    return jnp.cos(ang)[:, None], jnp.sin(ang)[:, None]


def apply_rope(x, cos, sin):
    xf = x.astype(jnp.float32)
    x1, x2 = jnp.split(xf, 2, axis=-1)
    out = jnp.concatenate([x1 * cos - x2 * sin, x2 * cos + x1 * sin], axis=-1)
    return out.astype(x.dtype)


def to_query_blocks(q):
    b, h, s, d = q.shape
    return q.reshape(b, h, s // Q_BLOCK, Q_BLOCK, d).transpose(2, 0, 1, 3, 4)


def from_query_blocks(o):
    nb, b, h, qb, d = o.shape
    return o.transpose(1, 2, 0, 3, 4).reshape(b, h, nb * qb, d)


def mla_causal_attention(q, k, v):
    seq = q.shape[2]
    scale = 1.0 / math.sqrt(q.shape[-1])
    kf = k.astype(jnp.float32)
    vf = v.astype(jnp.float32)
    k_pos = jnp.arange(seq)
    nb = seq // Q_BLOCK

    def block(args):
        i, qb = args
        s = jnp.einsum("bhqd,bhkd->bhqk", qb.astype(jnp.float32), kf) * scale
        q_pos = i * Q_BLOCK + jnp.arange(Q_BLOCK)
        causal = k_pos[None, :] <= q_pos[:, None]
        p = jax.nn.softmax(jnp.where(causal, s, -jnp.inf), axis=-1)
        return jnp.einsum("bhqk,bhkd->bhqd", p, vf)

    o = lax.map(block, (jnp.arange(nb), to_query_blocks(q)))
    return from_query_blocks(o).astype(q.dtype)


def stick_breaking_attention(q, k, v):
    seq = q.shape[2]
    scale = 1.0 / math.sqrt(q.shape[-1])
    kf = k.astype(jnp.float32)
    vf = v.astype(jnp.float32)
    k_pos = jnp.arange(seq)
    nb = seq // Q_BLOCK

    def block(args):
        i, qb = args
        z = jnp.einsum("bhqd,bhkd->bhqk", qb.astype(jnp.float32), kf) * scale
        q_pos = i * Q_BLOCK + jnp.arange(Q_BLOCK)
        strict = k_pos[None, :] < q_pos[:, None]
        log_beta = jax.nn.log_sigmoid(z)
        log_one_minus = jnp.where(strict, jax.nn.log_sigmoid(-z), 0.0)
        suffix = lax.cumsum(log_one_minus, axis=3, reverse=True) - log_one_minus
        a = jnp.where(strict, jnp.exp(log_beta + suffix), 0.0)
        return jnp.einsum("bhqk,bhkd->bhqd", a, vf)

    o = lax.map(block, (jnp.arange(nb), to_query_blocks(q)))
    return from_query_blocks(o).astype(q.dtype)


def split_heads(t, n_heads):
    b, s, _ = t.shape
    return t.reshape(b, s, n_heads, -1).transpose(0, 2, 1, 3)


def merge_heads(t):
    b, h, s, d = t.shape
    return t.transpose(0, 2, 1, 3).reshape(b, s, h * d)


def setup_inputs(seed: int = 0) -> dict:
    key = jax.random.key(seed)
    ks = jax.random.split(key, 20)
    f32 = jnp.float32

    def w(k, shape, fan_in):
        return jax.random.normal(k, shape, f32) * (fan_in ** -0.5)

    def gain(k, shape):
        return 1.0 + 0.02 * jax.random.normal(k, shape, f32)

    x = jax.random.normal(ks[0], (BATCH, SEQ, D_MODEL), f32)
    positions = jnp.broadcast_to(jnp.arange(SEQ, dtype=jnp.int32), (BATCH, SEQ))
    return {
        "x": x,
        "positions": positions,
        "norm_mix": gain(ks[1], (DEPTH, D_MODEL)),
        "w_in": w(ks[2], (DEPTH, D_MODEL, IN_PROJ_WIDTH), D_MODEL),
        "q_latent_norm": gain(ks[3], (DEPTH, Q_LORA_RANK)),
        "w_uq": w(ks[4], (DEPTH, Q_LORA_RANK, MLA_HEADS * MLA_QK_DIM), Q_LORA_RANK),
        "kv_latent_norm": gain(ks[5], (DEPTH, KV_LORA_RANK)),
        "w_ukv": w(ks[6], (DEPTH, KV_LORA_RANK, MLA_HEADS * (MLA_NOPE_DIM + MLA_V_DIM)), KV_LORA_RANK),
        "out_norm_mla": gain(ks[7], (DEPTH, MLA_WIDTH)),
        "out_norm_sb": gain(ks[8], (DEPTH, SB_WIDTH)),
        "w_o": w(ks[9], (DEPTH, MIX_WIDTH, D_MODEL), MIX_WIDTH),
        "norm_ffn": gain(ks[10], (DEPTH, D_MODEL)),
        "w_gate": w(ks[11], (DEPTH, D_MODEL, D_FF), D_MODEL),
        "w_up": w(ks[12], (DEPTH, D_MODEL, D_FF), D_MODEL),
        "w_down": w(ks[13], (DEPTH, D_FF, D_MODEL), D_FF),
        "norm_final": gain(ks[14], (D_MODEL,)),
    }


def reference(x, positions, norm_mix, w_in, q_latent_norm, w_uq, kv_latent_norm,
              w_ukv, out_norm_mla, out_norm_sb, w_o, norm_ffn, w_gate, w_up,
              w_down, norm_final):
    b, s, _ = x.shape
    cos, sin = rope_tables(positions, MLA_ROPE_DIM)
    h = x
    for l in range(DEPTH):
        u = rmsnorm(h, norm_mix[l])
        proj = jnp.einsum("bsd,de->bse", u, w_in[l])
        c_q, c_kv, k_r, q_sb, k_sb, v_sb = jnp.split(proj, IN_SPLIT_POINTS, axis=-1)

        q = split_heads(jnp.einsum("bsr,re->bse", rmsnorm(c_q, q_latent_norm[l]), w_uq[l]), MLA_HEADS)
        q_nope, q_rope = q[..., :MLA_NOPE_DIM], q[..., MLA_NOPE_DIM:]
        q_rope = apply_rope(q_rope, cos, sin)
        kv = split_heads(jnp.einsum("bsr,re->bse", rmsnorm(c_kv, kv_latent_norm[l]), w_ukv[l]), MLA_HEADS)
        k_nope, v_mla = kv[..., :MLA_NOPE_DIM], kv[..., MLA_NOPE_DIM:]
        k_rope = apply_rope(k_r[:, None], cos, sin)
        q_mla = jnp.concatenate([q_nope, q_rope], axis=-1)
        k_mla = jnp.concatenate([k_nope, jnp.broadcast_to(k_rope, (b, MLA_HEADS, s, MLA_ROPE_DIM))], axis=-1)
        o_mla = merge_heads(mla_causal_attention(q_mla, k_mla, v_mla))

        o_sb = merge_heads(stick_breaking_attention(
            split_heads(q_sb, SB_HEADS), split_heads(k_sb, SB_HEADS), split_heads(v_sb, SB_HEADS)))

        merged = jnp.concatenate([rmsnorm(o_mla, out_norm_mla[l]), rmsnorm(o_sb, out_norm_sb[l])], axis=-1)
        h = h + jnp.einsum("bse,ed->bsd", merged, w_o[l])

        f = rmsnorm(h, norm_ffn[l])
        gate = jnp.einsum("bsd,df->bsf", f, w_gate[l])
        up = jnp.einsum("bsd,df->bsf", f, w_up[l])
        h = h + jnp.einsum("bsf,fd->bsd", jax.nn.silu(gate) * up, w_down[l])
    return rmsnorm(h, norm_final)
```

```python
import functools
import math

import jax
import jax.numpy as jnp
from jax import lax
from jax.experimental import pallas as pl
from jax.experimental.pallas import tpu as pltpu

EPS = 1e-6
ROPE_THETA = 10000.0

MLA_HEADS = 8
MLA_NOPE_DIM = 64
MLA_ROPE_DIM = 32
MLA_V_DIM = 64
MLA_QK_DIM = MLA_NOPE_DIM + MLA_ROPE_DIM
MLA_PAD_DIM = 128
Q_LORA_RANK = 256
KV_LORA_RANK = 128
SB_HEADS = 8
SB_HEAD_DIM = 64
MLA_WIDTH = MLA_HEADS * MLA_V_DIM
SB_WIDTH = SB_HEADS * SB_HEAD_DIM

SEQ_BLOCK = 256
FF_CHUNK = 256
VMEM_LIMIT_BYTES = 56 * 1024 * 1024

F32 = jnp.float32
BF16 = jnp.bfloat16


def _rms_scale(v, axis):
    return lax.rsqrt(jnp.mean(v * v, axis=axis, keepdims=True) + EPS)


def _dot(a, b):
    return jnp.dot(a, b, preferred_element_type=F32)


def _dot_nt(a, b):
    return lax.dot_general(a, b, (((1,), (1,)), ((), ())), preferred_element_type=F32)


def _dot_tn(a, b):
    return lax.dot_general(a, b, (((0,), (0,)), ((), ())), preferred_element_type=F32)


def _proj_kernel(x_ref, pos_ref, freq_ref, g_mix_ref, g_q_ref, g_kv_ref,
                 w_lat_ref, w_sb_ref, w_uq_ref, w_uq_rot_ref, w_uk_ref, w_uv_ref,
                 q_mla_ref, k_mla_ref, vt_mla_ref, q_sb_ref, k_sb_ref, vt_sb_ref):
    x = x_ref[0]
    u = (x * _rms_scale(x, -1) * g_mix_ref[...]).astype(BF16)

    ang = pos_ref[0].astype(F32) * freq_ref[...]
    cos = jnp.cos(ang)
    sin = jnp.sin(ang)

    lat = _dot(u, w_lat_ref[...])
    c_q = lat[:, :Q_LORA_RANK]
    c_kv = lat[:, Q_LORA_RANK:Q_LORA_RANK + KV_LORA_RANK]
    k_r = lat[:, Q_LORA_RANK + KV_LORA_RANK:Q_LORA_RANK + KV_LORA_RANK + MLA_PAD_DIM]
    k_r_rot = lat[:, Q_LORA_RANK + KV_LORA_RANK + MLA_PAD_DIM:]
    k_rope = k_r * cos + k_r_rot * sin

    ql = (c_q * _rms_scale(c_q, -1) * g_q_ref[...]).astype(BF16)
    kvl = (c_kv * _rms_scale(c_kv, -1) * g_kv_ref[...]).astype(BF16)

    q_lin = _dot(ql, w_uq_ref[...])
    q_rot = _dot(ql, w_uq_rot_ref[...])
    k_nope = _dot(kvl, w_uk_ref[...])
    v_mla = _dot(kvl, w_uv_ref[...])
    q_scale = 1.0 / math.sqrt(MLA_QK_DIM)
    for h in range(MLA_HEADS):
        sl = slice(h * MLA_PAD_DIM, (h + 1) * MLA_PAD_DIM)
        q_h = (q_lin[:, sl] * cos + q_rot[:, sl] * sin) * q_scale
        q_mla_ref[0, h] = q_h.astype(BF16)
        k_mla_ref[0, h] = (k_nope[:, sl] + k_rope).astype(BF16)
    v_mla_t = v_mla.T
    for h in range(MLA_HEADS):
        vt_mla_ref[0, h, 0] = v_mla_t[h * MLA_V_DIM:(h + 1) * MLA_V_DIM, :].astype(BF16)

    sb = _dot(u, w_sb_ref[...])
    sb_scale = 1.0 / math.sqrt(SB_HEAD_DIM)
    q_sb = sb[:, :SB_WIDTH] * sb_scale
    k_sb = sb[:, SB_WIDTH:2 * SB_WIDTH]
    v_sb_t = sb[:, 2 * SB_WIDTH:].T
    for h in range(SB_HEADS):
        sl = slice(h * SB_HEAD_DIM, (h + 1) * SB_HEAD_DIM)
        q_sb_ref[0, h] = q_sb[:, sl].astype(BF16)
        k_sb_ref[0, h] = k_sb[:, sl].astype(BF16)
        vt_sb_ref[0, h, 0] = v_sb_t[sl, :].astype(BF16)


def _mla_kernel(q_ref, k_ref, vt_ref, o_ref, *, n_blocks):
    blk = SEQ_BLOCK
    key_idx = lax.broadcasted_iota(jnp.int32, (blk, blk), 0)
    qry_idx = lax.broadcasted_iota(jnp.int32, (blk, blk), 1)
    causal = key_idx <= qry_idx

    def q_body(qi, carry):
        q = q_ref[0, 0, pl.ds(pl.multiple_of(qi * blk, blk), blk), :]

        def scores(j):
            k = k_ref[0, 0, pl.ds(pl.multiple_of(j * blk, blk), blk), :]
            return _dot_nt(k, q)

        s = jnp.where(causal, scores(qi), -jnp.inf)
        m = jnp.max(s, axis=0, keepdims=True)
        p = jnp.exp(s - m)
        l = jnp.sum(p, axis=0, keepdims=True)
        acc = _dot(vt_ref[0, 0, qi], p.astype(BF16))

        def k_body(j, mla):
            m, l, acc = mla
            s = scores(j)
            m_new = jnp.maximum(m, jnp.max(s, axis=0, keepdims=True))
            alpha = jnp.exp(m - m_new)
            p = jnp.exp(s - m_new)
            l = alpha * l + jnp.sum(p, axis=0, keepdims=True)
            acc = alpha * acc + _dot(vt_ref[0, 0, j], p.astype(BF16))
            return m_new, l, acc

        m, l, acc = lax.fori_loop(0, qi, k_body, (m, l, acc))
        o_ref[0, 0, qi] = acc / l
        return carry

    lax.fori_loop(0, n_blocks, q_body, 0)


def _softplus(z):
    return jnp.maximum(z, 0.0) + jnp.log(1.0 + jnp.exp(-jnp.abs(z)))


def _sb_kernel(q_ref, k_ref, vt_ref, tri_ref, o_ref, *, n_blocks):
    blk = SEQ_BLOCK
    key_idx = lax.broadcasted_iota(jnp.int32, (blk, blk), 0)
    qry_idx = lax.broadcasted_iota(jnp.int32, (blk, blk), 1)
    strict = key_idx < qry_idx

    def q_body(qi, carry):
        q = q_ref[0, 0, pl.ds(pl.multiple_of(qi * blk, blk), blk), :]

        def logits(j):
            k = k_ref[0, 0, pl.ds(pl.multiple_of(j * blk, blk), blk), :]
            return _dot_nt(k, q)

        z = logits(qi)
        sp = jnp.where(strict, _softplus(z), 0.0)
        suf = _dot(tri_ref[...], sp.astype(BF16))
        a = jnp.where(strict, jnp.exp(z + suf), 0.0)
        acc = _dot(vt_ref[0, 0, qi], a.astype(BF16))
        c = suf[0:1, :]

        def k_body(jj, state):
            c, acc = state
            j = qi - 1 - jj
            z = logits(j)
            suf = _dot(tri_ref[...], _softplus(z).astype(BF16))
            a = jnp.exp(z + suf + c)
            acc = acc + _dot(vt_ref[0, 0, j], a.astype(BF16))
            return c + suf[0:1, :], acc

        c, acc = lax.fori_loop(0, qi, k_body, (c, acc))
        o_ref[0, 0, qi] = acc
        return carry

    lax.fori_loop(0, n_blocks, q_body, 0)


def _out_kernel(x_ref, o_mla_ref, o_sb_ref, g_mla_ref, g_sb_ref, w_o_mla_ref, w_o_sb_ref,
                g_ffn_ref, w_gate_ref, w_up_ref, w_down_ref, g_final_ref, out_ref, *, d_ff):
    blk = SEQ_BLOCK

    def group(o_ref, g_ref, w_ref):
        o = o_ref[0, :, 0].reshape(-1, blk)
        y = (o * _rms_scale(o, 0) * g_ref[...]).astype(BF16)
        return _dot_tn(y, w_ref[...])

    h = x_ref[0] + group(o_mla_ref, g_mla_ref, w_o_mla_ref) + group(o_sb_ref, g_sb_ref, w_o_sb_ref)
    f = (h * _rms_scale(h, -1) * g_ffn_ref[...]).astype(BF16)
    ffn = jnp.zeros_like(h)
    for c in range(0, d_ff, FF_CHUNK):
        gate = _dot(f, w_gate_ref[:, c:c + FF_CHUNK])
        up = _dot(f, w_up_ref[:, c:c + FF_CHUNK])
        act = (gate * jax.nn.sigmoid(gate) * up).astype(BF16)
        ffn = ffn + _dot(act, w_down_ref[c:c + FF_CHUNK, :])
    h = h + ffn
    out_ref[0] = h * _rms_scale(h, -1) * g_final_ref[...]


def _rotate_half_cols(w):
    half = w.shape[-1] // 2
    return jnp.concatenate([-w[..., half:], w[..., :half]], axis=-1)


def _head_slots(nope, rope):
    ref = nope if nope is not None else rope
    r, h = ref.shape[0], ref.shape[1]
    nope = jnp.zeros((r, h, MLA_NOPE_DIM), ref.dtype) if nope is None else nope
    rope = jnp.zeros((r, h, MLA_ROPE_DIM), ref.dtype) if rope is None else rope
    pad = jnp.zeros((r, h, MLA_PAD_DIM - MLA_QK_DIM), ref.dtype)
    return jnp.concatenate([nope, rope, pad], axis=-1).reshape(r, h * MLA_PAD_DIM)


def _const_spec(shape):
    return pl.BlockSpec(shape, lambda *_: (0,) * len(shape))


def kernel(x, positions, norm_mix, w_in, q_latent_norm, w_uq, kv_latent_norm, w_ukv,
           out_norm_mla, out_norm_sb, w_o, norm_ffn, w_gate, w_up, w_down, norm_final):
    b, s, d = x.shape
    depth = w_in.shape[0]
    d_ff = w_gate.shape[-1]
    blk = SEQ_BLOCK
    nb = s // blk
    assert s % blk == 0 and d_ff % FF_CHUNK == 0

    inv_freq = ROPE_THETA ** (-jnp.arange(0, MLA_ROPE_DIM, 2, dtype=F32) / MLA_ROPE_DIM)
    zeros = lambda n: jnp.zeros((n,), F32)
    freq = jnp.concatenate([zeros(MLA_NOPE_DIM), inv_freq, inv_freq,
                            zeros(MLA_PAD_DIM - MLA_QK_DIM)])[None, :]
    tri = jnp.where(jnp.arange(blk)[None, :] >= jnp.arange(blk)[:, None], -1.0, 0.0).astype(BF16)
    pos = positions.reshape(b, s, 1)

    params = pltpu.CompilerParams(
        dimension_semantics=("arbitrary", "arbitrary"), vmem_limit_bytes=VMEM_LIMIT_BYTES)

    h = x
    for l in range(depth):
        o0 = Q_LORA_RANK
        o1 = o0 + KV_LORA_RANK
        o2 = o1 + MLA_ROPE_DIM
        w_cq, w_ckv, w_kr, w_sb = w_in[l][:, :o0], w_in[l][:, o0:o1], w_in[l][:, o1:o2], w_in[l][:, o2:]
        kr_slot = lambda w: _head_slots(None, w[:, None, :])
        w_lat = jnp.concatenate([w_cq, w_ckv, kr_slot(w_kr), kr_slot(_rotate_half_cols(w_kr))],
                                axis=1).astype(BF16)
        uq = w_uq[l].reshape(Q_LORA_RANK, MLA_HEADS, MLA_QK_DIM)
        uq_nope, uq_rope = uq[..., :MLA_NOPE_DIM], uq[..., MLA_NOPE_DIM:]
        w_uq_lin = _head_slots(uq_nope, uq_rope).astype(BF16)
        w_uq_rot = _head_slots(None, _rotate_half_cols(uq_rope)).astype(BF16)
        ukv = w_ukv[l].reshape(KV_LORA_RANK, MLA_HEADS, MLA_NOPE_DIM + MLA_V_DIM)
        w_uk = _head_slots(ukv[..., :MLA_NOPE_DIM], None).astype(BF16)
        w_uv = ukv[..., MLA_NOPE_DIM:].reshape(KV_LORA_RANK, MLA_WIDTH).astype(BF16)

        head_major = lambda width: jax.ShapeDtypeStruct((b, MLA_HEADS, s, width), BF16)
        head_spec = lambda width: pl.BlockSpec((1, MLA_HEADS, blk, width), lambda bi, ti: (bi, 0, ti, 0))
        vt_shape = jax.ShapeDtypeStruct((b, MLA_HEADS, nb, MLA_V_DIM, blk), BF16)
        vt_spec = pl.BlockSpec((1, MLA_HEADS, 1, MLA_V_DIM, blk), lambda bi, ti: (bi, 0, ti, 0, 0))
        q_mla, k_mla, vt_mla, q_sb, k_sb, vt_sb = pl.pallas_call(
            _proj_kernel,
            grid=(b, nb),
            in_specs=[
                pl.BlockSpec((1, blk, d), lambda bi, ti: (bi, ti, 0)),
                pl.BlockSpec((1, blk, 1), lambda bi, ti: (bi, ti, 0)),
                _const_spec((1, MLA_PAD_DIM)),
                _const_spec((1, d)),
                _const_spec((1, Q_LORA_RANK)),
                _const_spec((1, KV_LORA_RANK)),
                _const_spec(w_lat.shape),
                _const_spec((d, 3 * SB_WIDTH)),
                _const_spec(w_uq_lin.shape),
                _const_spec(w_uq_rot.shape),
                _const_spec(w_uk.shape),
                _const_spec(w_uv.shape),
            ],
            out_specs=[head_spec(MLA_PAD_DIM), head_spec(MLA_PAD_DIM), vt_spec,
                       head_spec(SB_HEAD_DIM), head_spec(SB_HEAD_DIM), vt_spec],
            out_shape=[head_major(MLA_PAD_DIM), head_major(MLA_PAD_DIM), vt_shape,
                       head_major(SB_HEAD_DIM), head_major(SB_HEAD_DIM), vt_shape],
            compiler_params=params,
            name="proj",
        )(h, pos, freq, norm_mix[l][None, :], q_latent_norm[l][None, :], kv_latent_norm[l][None, :],
          w_lat, w_sb.astype(BF16), w_uq_lin, w_uq_rot, w_uk, w_uv)

        def attn_specs(width):
            seq_spec = pl.BlockSpec((1, 1, s, width), lambda bi, hi: (bi, hi, 0, 0))
            blocked = pl.BlockSpec((1, 1, nb, MLA_V_DIM, blk), lambda bi, hi: (bi, hi, 0, 0, 0))
            return seq_spec, blocked

        o_shape = jax.ShapeDtypeStruct((b, MLA_HEADS, nb, MLA_V_DIM, blk), F32)
        seq_spec, blocked = attn_specs(MLA_PAD_DIM)
        o_mla = pl.pallas_call(
            functools.partial(_mla_kernel, n_blocks=nb),
            grid=(b, MLA_HEADS),
            in_specs=[seq_spec, seq_spec, blocked],
            out_specs=blocked,
            out_shape=o_shape,
            compiler_params=params,
            name="mla_attn",
        )(q_mla, k_mla, vt_mla)
        seq_spec, blocked = attn_specs(SB_HEAD_DIM)
        o_sb = pl.pallas_call(
            functools.partial(_sb_kernel, n_blocks=nb),
            grid=(b, SB_HEADS),
            in_specs=[seq_spec, seq_spec, blocked, _const_spec((blk, blk))],
            out_specs=blocked,
            out_shape=o_shape,
            compiler_params=params,
            name="sb_attn",
        )(q_sb, k_sb, vt_sb, tri)

        assert depth == 1
        o_spec = pl.BlockSpec((1, MLA_HEADS, 1, MLA_V_DIM, blk), lambda bi, ti: (bi, 0, ti, 0, 0))
        tok_spec = pl.BlockSpec((1, blk, d), lambda bi, ti: (bi, ti, 0))
        resident = lambda shape: pl.BlockSpec(shape, lambda *_: (0,) * len(shape),
                                              pipeline_mode=pl.Buffered(1))
        h = pl.pallas_call(
            functools.partial(_out_kernel, d_ff=d_ff),
            grid=(b, nb),
            in_specs=[
                tok_spec, o_spec, o_spec,
                _const_spec((MLA_WIDTH, 1)), _const_spec((SB_WIDTH, 1)),
                resident((MLA_WIDTH, d)), resident((SB_WIDTH, d)),
                _const_spec((1, d)),
                resident((d, d_ff)), resident((d, d_ff)), resident((d_ff, d)),
                _const_spec((1, d)),
            ],
            out_specs=tok_spec,
            out_shape=jax.ShapeDtypeStruct((b, s, d), F32),
            compiler_params=params,
            name="out_ffn",
        )(h, o_mla, o_sb, out_norm_mla[l][:, None], out_norm_sb[l][:, None],
          w_o[l][:MLA_WIDTH].astype(BF16), w_o[l][MLA_WIDTH:].astype(BF16),
          norm_ffn[l][None, :], w_gate[l].astype(BF16), w_up[l].astype(BF16),
          w_down[l].astype(BF16), norm_final[None, :])
    return h
```

```python
import functools
import math

import jax
import jax.numpy as jnp
from jax import lax
from jax.experimental import pallas as pl
from jax.experimental.pallas import tpu as pltpu

EPS = 1e-6
ROPE_THETA = 10000.0

MLA_HEADS = 8
MLA_NOPE_DIM = 64
MLA_ROPE_DIM = 32
MLA_V_DIM = 64
MLA_QK_DIM = MLA_NOPE_DIM + MLA_ROPE_DIM
MLA_PAD_DIM = 128
Q_LORA_RANK = 256
KV_LORA_RANK = 128
SB_HEADS = 8
SB_HEAD_DIM = 64
MLA_WIDTH = MLA_HEADS * MLA_V_DIM
SB_WIDTH = SB_HEADS * SB_HEAD_DIM

SEQ_BLOCK = 256
N_CHAINS = 4
FF_CHUNK = 256
VMEM_LIMIT_BYTES = 56 * 1024 * 1024

F32 = jnp.float32
BF16 = jnp.bfloat16


def _rms_scale(v, axis):
    return lax.rsqrt(jnp.mean(v * v, axis=axis, keepdims=True) + EPS)


def _dot(a, b):
    return jnp.dot(a, b, preferred_element_type=F32)


def _dot_nt(a, b):
    return lax.dot_general(a, b, (((1,), (1,)), ((), ())), preferred_element_type=F32)


def _dot_tn(a, b):
    return lax.dot_general(a, b, (((0,), (0,)), ((), ())), preferred_element_type=F32)


def _proj_kernel(x_ref, pos_ref, freq_ref, g_mix_ref, g_q_ref, g_kv_ref,
                 w_lat_ref, w_sb_ref, w_uq_ref, w_uq_rot_ref, w_uk_ref, w_uv_ref,
                 q_mla_ref, k_mla_ref, vt_mla_ref, q_sb_ref, k_sb_ref, vt_sb_ref):
    x = x_ref[0]
    u = (x * _rms_scale(x, -1) * g_mix_ref[...]).astype(BF16)

    ang = pos_ref[0].astype(F32) * freq_ref[...]
    cos = jnp.cos(ang)
    sin = jnp.sin(ang)

    lat = _dot(u, w_lat_ref[...])
    c_q = lat[:, :Q_LORA_RANK]
    c_kv = lat[:, Q_LORA_RANK:Q_LORA_RANK + KV_LORA_RANK]
    k_r = lat[:, Q_LORA_RANK + KV_LORA_RANK:Q_LORA_RANK + KV_LORA_RANK + MLA_PAD_DIM]
    k_r_rot = lat[:, Q_LORA_RANK + KV_LORA_RANK + MLA_PAD_DIM:]
    k_rope = k_r * cos + k_r_rot * sin

    ql = (c_q * _rms_scale(c_q, -1) * g_q_ref[...]).astype(BF16)
    kvl = (c_kv * _rms_scale(c_kv, -1) * g_kv_ref[...]).astype(BF16)

    q_lin = _dot(ql, w_uq_ref[...])
    q_rot = _dot(ql, w_uq_rot_ref[...])
    k_nope = _dot(kvl, w_uk_ref[...])
    v_mla = _dot(kvl, w_uv_ref[...])
    q_scale = 1.0 / math.sqrt(MLA_QK_DIM)
    for h in range(MLA_HEADS):
        sl = slice(h * MLA_PAD_DIM, (h + 1) * MLA_PAD_DIM)
        q_h = (q_lin[:, sl] * cos + q_rot[:, sl] * sin) * q_scale
        q_mla_ref[0, h] = q_h.astype(BF16)
        k_mla_ref[0, h] = (k_nope[:, sl] + k_rope).astype(BF16)
    v_mla_t = v_mla.T
    for h in range(MLA_HEADS):
        vt_mla_ref[0, h, 0] = v_mla_t[h * MLA_V_DIM:(h + 1) * MLA_V_DIM, :].astype(BF16)

    sb = _dot(u, w_sb_ref[...])
    sb_scale = 1.0 / math.sqrt(SB_HEAD_DIM)
    q_sb = sb[:, :SB_WIDTH] * sb_scale
    k_sb = sb[:, SB_WIDTH:2 * SB_WIDTH]
    v_sb_t = sb[:, 2 * SB_WIDTH:].T
    for h in range(SB_HEADS):
        sl = slice(h * SB_HEAD_DIM, (h + 1) * SB_HEAD_DIM)
        q_sb_ref[0, h] = q_sb[:, sl].astype(BF16)
        k_sb_ref[0, h] = k_sb[:, sl].astype(BF16)
        vt_sb_ref[0, h, 0] = v_sb_t[sl, :].astype(BF16)


def _attention_driver(q_ref, k_ref, vt_ref, o_ref, z_scr, consume_fn, finish_fn, diag_mask, n_super):
    blk, n = SEQ_BLOCK, N_CHAINS

    def rows(ref, j):
        return ref[0, 0, pl.ds(pl.multiple_of(j * blk, blk), blk), :]

    def super_body(sb, carry):
        base = sb * n

        def scores(j, chains):
            k = rows(k_ref, j)
            return [_dot_nt(k, rows(q_ref, base + a)) for a in chains]

        state = [None] * n
        zs = scores(base + n - 1, [n - 1])
        for kb in reversed(range(n)):
            chains = list(range(kb, n))
            if kb > 0:
                zs_next = scores(base + kb - 1, list(range(kb - 1, n)))
            else:
                for a, z in enumerate(scores(jnp.maximum(base - 1, 0), range(n))):
                    z_scr[0, a] = z
            new = consume_fn(zs, vt_ref[0, 0, base + kb], [state[a] for a in chains],
                             [diag_mask if a == kb else None for a in chains])
            for a, st in zip(chains, new):
                state[a] = st
            if kb > 0:
                zs = zs_next

        def k_body(it, state):
            j = base - 1 - 2 * it
            for slot, j_cur, j_next in ((0, j, j - 1), (1, j - 1, jnp.maximum(j - 2, 0))):
                for a, z in enumerate(scores(j_next, range(n))):
                    z_scr[1 - slot, a] = z
                state = consume_fn([z_scr[slot, a] for a in range(n)], vt_ref[0, 0, j_cur],
                                   list(state), [None] * n)
            return tuple(state)

        state = lax.fori_loop(0, sb * (n // 2), k_body, tuple(state))
        for a in range(n):
            o_ref[0, 0, base + a] = finish_fn(state[a])
        return carry

    lax.fori_loop(0, n_super, super_body, 0)


def _block_iotas():
    key_idx = lax.broadcasted_iota(jnp.int32, (SEQ_BLOCK, SEQ_BLOCK), 0)
    qry_idx = lax.broadcasted_iota(jnp.int32, (SEQ_BLOCK, SEQ_BLOCK), 1)
    return key_idx, qry_idx


def _mla_kernel(q_ref, k_ref, vt_ref, o_ref, z_scr, *, n_super):
    key_idx, qry_idx = _block_iotas()

    def consume(zs, vt, states, masks):
        partial = []
        for s, state, mask in zip(zs, states, masks):
            if mask is not None:
                s = jnp.where(mask, s, -jnp.inf)
            s_max = jnp.max(s, axis=0, keepdims=True)
            if state is None:
                m_new, alpha = s_max, None
            else:
                m_new = jnp.maximum(state[0], s_max)
                alpha = jnp.exp(state[0] - m_new)
            p = jnp.exp(s - m_new)
            partial.append((m_new, alpha, jnp.sum(p, axis=0, keepdims=True), p.astype(BF16)))
        out = []
        for (m_new, alpha, p_sum, p), state in zip(partial, states):
            pv = _dot(vt, p)
            if state is None:
                out.append((m_new, p_sum, pv))
            else:
                out.append((m_new, alpha * state[1] + p_sum, alpha * state[2] + pv))
        return out

    def finish(state):
        _, l, acc = state
        return acc / l

    _attention_driver(q_ref, k_ref, vt_ref, o_ref, z_scr, consume, finish, key_idx <= qry_idx, n_super)


def _softplus(z):
    return jnp.maximum(z, 0.0) + jnp.log(1.0 + jnp.exp(-jnp.abs(z)))


def _sb_kernel(q_ref, k_ref, vt_ref, tri_ref, o_ref, z_scr, *, n_super):
    key_idx, qry_idx = _block_iotas()

    def consume(zs, vt, states, masks):
        sps = []
        for z, mask in zip(zs, masks):
            sp = _softplus(z)
            if mask is not None:
                sp = jnp.where(mask, sp, 0.0)
            sps.append(sp.astype(BF16))
        sufs = [_dot(tri_ref[...], sp) for sp in sps]
        weights = []
        for z, suf, state, mask in zip(zs, sufs, states, masks):
            e = z + suf
            if state is not None:
                e = e + state[0]
            a = jnp.exp(e)
            if mask is not None:
                a = jnp.where(mask, a, 0.0)
            weights.append(a.astype(BF16))
        out = []
        for a, suf, state in zip(weights, sufs, states):
            pv = _dot(vt, a)
            if state is None:
                out.append((suf[0:1, :], pv))
            else:
                out.append((state[0] + suf[0:1, :], state[1] + pv))
        return out

    _attention_driver(q_ref, k_ref, vt_ref, o_ref, z_scr, consume, lambda st: st[1],
                      key_idx < qry_idx, n_super)


def _out_kernel(x_ref, o_mla_ref, o_sb_ref, g_mla_ref, g_sb_ref, w_o_mla_ref, w_o_sb_ref,
                g_ffn_ref, w_gate_ref, w_up_ref, w_down_ref, g_final_ref, out_ref, *, d_ff):
    blk = SEQ_BLOCK

    def group(o_ref, g_ref, w_ref):
        o = o_ref[0, :, 0].reshape(-1, blk)
        y = (o * _rms_scale(o, 0) * g_ref[...]).astype(BF16)
        return _dot_tn(y, w_ref[...])

    h = x_ref[0] + group(o_mla_ref, g_mla_ref, w_o_mla_ref) + group(o_sb_ref, g_sb_ref, w_o_sb_ref)
    f = (h * _rms_scale(h, -1) * g_ffn_ref[...]).astype(BF16)
    ffn = jnp.zeros_like(h)
    for c in range(0, d_ff, FF_CHUNK):
        gate = _dot(f, w_gate_ref[:, c:c + FF_CHUNK])
        up = _dot(f, w_up_ref[:, c:c + FF_CHUNK])
        act = (gate * jax.nn.sigmoid(gate) * up).astype(BF16)
        ffn = ffn + _dot(act, w_down_ref[c:c + FF_CHUNK, :])
    h = h + ffn
    out_ref[0] = h * _rms_scale(h, -1) * g_final_ref[...]


def _rotate_half_cols(w):
    half = w.shape[-1] // 2
    return jnp.concatenate([-w[..., half:], w[..., :half]], axis=-1)


def _head_slots(nope, rope):
    ref = nope if nope is not None else rope
    r, h = ref.shape[0], ref.shape[1]
    nope = jnp.zeros((r, h, MLA_NOPE_DIM), ref.dtype) if nope is None else nope
    rope = jnp.zeros((r, h, MLA_ROPE_DIM), ref.dtype) if rope is None else rope
    pad = jnp.zeros((r, h, MLA_PAD_DIM - MLA_QK_DIM), ref.dtype)
    return jnp.concatenate([nope, rope, pad], axis=-1).reshape(r, h * MLA_PAD_DIM)


def _const_spec(shape):
    return pl.BlockSpec(shape, lambda *_: (0,) * len(shape))


def kernel(x, positions, norm_mix, w_in, q_latent_norm, w_uq, kv_latent_norm, w_ukv,
           out_norm_mla, out_norm_sb, w_o, norm_ffn, w_gate, w_up, w_down, norm_final):
    b, s, d = x.shape
    depth = w_in.shape[0]
    d_ff = w_gate.shape[-1]
    blk = SEQ_BLOCK
    nb = s // blk
    assert s % (blk * N_CHAINS) == 0 and N_CHAINS % 2 == 0 and d_ff % FF_CHUNK == 0

    inv_freq = ROPE_THETA ** (-jnp.arange(0, MLA_ROPE_DIM, 2, dtype=F32) / MLA_ROPE_DIM)
    zeros = lambda n: jnp.zeros((n,), F32)
    freq = jnp.concatenate([zeros(MLA_NOPE_DIM), inv_freq, inv_freq,
                            zeros(MLA_PAD_DIM - MLA_QK_DIM)])[None, :]
    tri = jnp.where(jnp.arange(blk)[None, :] >= jnp.arange(blk)[:, None], -1.0, 0.0).astype(BF16)
    pos = positions.reshape(b, s, 1)

    params = pltpu.CompilerParams(
        dimension_semantics=("arbitrary", "arbitrary"), vmem_limit_bytes=VMEM_LIMIT_BYTES)

    h = x
    for l in range(depth):
        o0 = Q_LORA_RANK
        o1 = o0 + KV_LORA_RANK
        o2 = o1 + MLA_ROPE_DIM
        w_cq, w_ckv, w_kr, w_sb = w_in[l][:, :o0], w_in[l][:, o0:o1], w_in[l][:, o1:o2], w_in[l][:, o2:]
        kr_slot = lambda w: _head_slots(None, w[:, None, :])
        w_lat = jnp.concatenate([w_cq, w_ckv, kr_slot(w_kr), kr_slot(_rotate_half_cols(w_kr))],
                                axis=1).astype(BF16)
        uq = w_uq[l].reshape(Q_LORA_RANK, MLA_HEADS, MLA_QK_DIM)
        uq_nope, uq_rope = uq[..., :MLA_NOPE_DIM], uq[..., MLA_NOPE_DIM:]
        w_uq_lin = _head_slots(uq_nope, uq_rope).astype(BF16)
        w_uq_rot = _head_slots(None, _rotate_half_cols(uq_rope)).astype(BF16)
        ukv = w_ukv[l].reshape(KV_LORA_RANK, MLA_HEADS, MLA_NOPE_DIM + MLA_V_DIM)
        w_uk = _head_slots(ukv[..., :MLA_NOPE_DIM], None).astype(BF16)
        w_uv = ukv[..., MLA_NOPE_DIM:].reshape(KV_LORA_RANK, MLA_WIDTH).astype(BF16)

        head_major = lambda width: jax.ShapeDtypeStruct((b, MLA_HEADS, s, width), BF16)
        head_spec = lambda width: pl.BlockSpec((1, MLA_HEADS, blk, width), lambda bi, ti: (bi, 0, ti, 0))
        vt_shape = jax.ShapeDtypeStruct((b, MLA_HEADS, nb, MLA_V_DIM, blk), BF16)
        vt_spec = pl.BlockSpec((1, MLA_HEADS, 1, MLA_V_DIM, blk), lambda bi, ti: (bi, 0, ti, 0, 0))
        q_mla, k_mla, vt_mla, q_sb, k_sb, vt_sb = pl.pallas_call(
            _proj_kernel,
            grid=(b, nb),
            in_specs=[
                pl.BlockSpec((1, blk, d), lambda bi, ti: (bi, ti, 0)),
                pl.BlockSpec((1, blk, 1), lambda bi, ti: (bi, ti, 0)),
                _const_spec((1, MLA_PAD_DIM)),
                _const_spec((1, d)),
                _const_spec((1, Q_LORA_RANK)),
                _const_spec((1, KV_LORA_RANK)),
                _const_spec(w_lat.shape),
                _const_spec((d, 3 * SB_WIDTH)),
                _const_spec(w_uq_lin.shape),
                _const_spec(w_uq_rot.shape),
                _const_spec(w_uk.shape),
                _const_spec(w_uv.shape),
            ],
            out_specs=[head_spec(MLA_PAD_DIM), head_spec(MLA_PAD_DIM), vt_spec,
                       head_spec(SB_HEAD_DIM), head_spec(SB_HEAD_DIM), vt_spec],
            out_shape=[head_major(MLA_PAD_DIM), head_major(MLA_PAD_DIM), vt_shape,
                       head_major(SB_HEAD_DIM), head_major(SB_HEAD_DIM), vt_shape],
            compiler_params=params,
            name="proj",
        )(h, pos, freq, norm_mix[l][None, :], q_latent_norm[l][None, :], kv_latent_norm[l][None, :],
          w_lat, w_sb.astype(BF16), w_uq_lin, w_uq_rot, w_uk, w_uv)

        def attn_specs(width):
            seq_spec = pl.BlockSpec((1, 1, s, width), lambda bi, hi: (bi, hi, 0, 0))
            blocked = pl.BlockSpec((1, 1, nb, MLA_V_DIM, blk), lambda bi, hi: (bi, hi, 0, 0, 0))
            return seq_spec, blocked

        o_shape = jax.ShapeDtypeStruct((b, MLA_HEADS, nb, MLA_V_DIM, blk), F32)
        score_scratch = [pltpu.VMEM((2, N_CHAINS, blk, blk), F32)]
        seq_spec, blocked = attn_specs(MLA_PAD_DIM)
        o_mla = pl.pallas_call(
            functools.partial(_mla_kernel, n_super=nb // N_CHAINS),
            grid=(b, MLA_HEADS),
            in_specs=[seq_spec, seq_spec, blocked],
            out_specs=blocked,
            out_shape=o_shape,
            scratch_shapes=score_scratch,
            compiler_params=params,
            name="mla_attn",
        )(q_mla, k_mla, vt_mla)
        seq_spec, blocked = attn_specs(SB_HEAD_DIM)
        o_sb = pl.pallas_call(
            functools.partial(_sb_kernel, n_super=nb // N_CHAINS),
            grid=(b, SB_HEADS),
            in_specs=[seq_spec, seq_spec, blocked, _const_spec((blk, blk))],
            out_specs=blocked,
            out_shape=o_shape,
            scratch_shapes=score_scratch,
            compiler_params=params,
            name="sb_attn",
        )(q_sb, k_sb, vt_sb, tri)

        assert depth == 1
        o_spec = pl.BlockSpec((1, MLA_HEADS, 1, MLA_V_DIM, blk), lambda bi, ti: (bi, 0, ti, 0, 0))
        tok_spec = pl.BlockSpec((1, blk, d), lambda bi, ti: (bi, ti, 0))
        resident = lambda shape: pl.BlockSpec(shape, lambda *_: (0,) * len(shape),
                                              pipeline_mode=pl.Buffered(1))
        h = pl.pallas_call(
            functools.partial(_out_kernel, d_ff=d_ff),
            grid=(b, nb),
            in_specs=[
                tok_spec, o_spec, o_spec,
                _const_spec((MLA_WIDTH, 1)), _const_spec((SB_WIDTH, 1)),
                resident((MLA_WIDTH, d)), resident((SB_WIDTH, d)),
                _const_spec((1, d)),
                resident((d, d_ff)), resident((d, d_ff)), resident((d_ff, d)),
                _const_spec((1, d)),
            ],
            out_specs=tok_spec,
            out_shape=jax.ShapeDtypeStruct((b, s, d), F32),
            compiler_params=params,
            name="out_ffn",
        )(h, o_mla, o_sb, out_norm_mla[l][:, None], out_norm_sb[l][:, None],
          w_o[l][:MLA_WIDTH].astype(BF16), w_o[l][MLA_WIDTH:].astype(BF16),
          norm_ffn[l][None, :], w_gate[l].astype(BF16), w_up[l].astype(BF16),
          w_down[l].astype(BF16), norm_final[None, :])
    return h
```

```python
import functools
import math

import jax
import jax.numpy as jnp
from jax import lax
from jax.experimental import pallas as pl
from jax.experimental.pallas import tpu as pltpu

EPS = 1e-6
ROPE_THETA = 10000.0
LOG2_E = 1.4426950408889634

MLA_HEADS = 8
MLA_NOPE_DIM = 64
MLA_ROPE_DIM = 32
MLA_V_DIM = 64
MLA_QK_DIM = MLA_NOPE_DIM + MLA_ROPE_DIM
MLA_PAD_DIM = 128
Q_LORA_RANK = 256
KV_LORA_RANK = 128
SB_HEADS = 8
SB_HEAD_DIM = 64
MLA_WIDTH = MLA_HEADS * MLA_V_DIM
SB_WIDTH = SB_HEADS * SB_HEAD_DIM

SEQ_BLOCK = 256
N_CHAINS = 4
K_PER_STEP = 4
FF_CHUNK = 256
VMEM_LIMIT_BYTES = 56 * 1024 * 1024

F32 = jnp.float32
BF16 = jnp.bfloat16


def _rms_scale(v, axis):
    return lax.rsqrt(jnp.mean(v * v, axis=axis, keepdims=True) + EPS)


def _dot(a, b):
    return jnp.dot(a, b, preferred_element_type=F32)


def _dot_nt(a, b):
    return lax.dot_general(a, b, (((1,), (1,)), ((), ())), preferred_element_type=F32)


def _dot_tn(a, b):
    return lax.dot_general(a, b, (((0,), (0,)), ((), ())), preferred_element_type=F32)


def _proj_kernel(x_ref, pos_ref, freq_ref, g_mix_ref, g_q_ref, g_kv_ref,
                 w_lat_ref, w_sb_ref, w_uq_ref, w_uq_rot_ref, w_uk_ref, w_uv_ref,
                 q_mla_ref, k_mla_ref, vt_mla_ref, q_sb_ref, k_sb_ref, vt_sb_ref):
    x = x_ref[0]
    u = (x * _rms_scale(x, -1) * g_mix_ref[...]).astype(BF16)

    ang = pos_ref[0].astype(F32) * freq_ref[...]
    cos = jnp.cos(ang)
    sin = jnp.sin(ang)

    lat = _dot(u, w_lat_ref[...])
    c_q = lat[:, :Q_LORA_RANK]
    c_kv = lat[:, Q_LORA_RANK:Q_LORA_RANK + KV_LORA_RANK]
    k_r = lat[:, Q_LORA_RANK + KV_LORA_RANK:Q_LORA_RANK + KV_LORA_RANK + MLA_PAD_DIM]
    k_r_rot = lat[:, Q_LORA_RANK + KV_LORA_RANK + MLA_PAD_DIM:]
    k_rope = k_r * cos + k_r_rot * sin

    ql = (c_q * _rms_scale(c_q, -1) * g_q_ref[...]).astype(BF16)
    kvl = (c_kv * _rms_scale(c_kv, -1) * g_kv_ref[...]).astype(BF16)

    q_lin = _dot(ql, w_uq_ref[...])
    q_rot = _dot(ql, w_uq_rot_ref[...])
    k_nope = _dot(kvl, w_uk_ref[...])
    v_mla = _dot(kvl, w_uv_ref[...])
    q_scale = LOG2_E / math.sqrt(MLA_QK_DIM)
    for h in range(MLA_HEADS):
        sl = slice(h * MLA_PAD_DIM, (h + 1) * MLA_PAD_DIM)
        q_h = (q_lin[:, sl] * cos + q_rot[:, sl] * sin) * q_scale
        q_mla_ref[0, h] = q_h.astype(BF16)
        k_mla_ref[0, h] = (k_nope[:, sl] + k_rope).astype(BF16)
    v_mla_t = v_mla.T
    for h in range(MLA_HEADS):
        vt_mla_ref[0, h, 0] = v_mla_t[h * MLA_V_DIM:(h + 1) * MLA_V_DIM, :].astype(BF16)

    sb = _dot(u, w_sb_ref[...])
    sb_scale = LOG2_E / math.sqrt(SB_HEAD_DIM)
    q_sb = sb[:, :SB_WIDTH] * sb_scale
    k_sb = sb[:, SB_WIDTH:2 * SB_WIDTH]
    v_sb_t = sb[:, 2 * SB_WIDTH:].T
    for h in range(SB_HEADS):
        sl = slice(h * SB_HEAD_DIM, (h + 1) * SB_HEAD_DIM)
        q_sb_ref[0, h] = q_sb[:, sl].astype(BF16)
        k_sb_ref[0, h] = k_sb[:, sl].astype(BF16)
        vt_sb_ref[0, h, 0] = v_sb_t[sl, :].astype(BF16)


def _attention_driver(q_ref, k_ref, vt_ref, o_ref, z_scr, w_scr, front_fn, apply_fn, finish_fn,
                      diag_mask, n_super):
    blk, n = SEQ_BLOCK, N_CHAINS
    unit = jnp.ones((1, blk), F32)

    def rows(ref, j):
        return ref[0, 0, pl.ds(pl.multiple_of(j * blk, blk), blk), :]

    def super_body(sb, carry):
        base = sb * n

        def scores(j, chains):
            k = rows(k_ref, j)
            return [_dot_nt(k, rows(q_ref, base + a)) for a in chains]

        def park_scores(slot, j):
            for a, z in enumerate(scores(j, range(n))):
                z_scr[slot, a] = z

        def flush(j, accs, factors):
            vt = vt_ref[0, 0, j]
            return [apply_fn(acc, _dot(vt, w_scr[a]), f) for a, (acc, f) in enumerate(zip(accs, factors))]

        smalls, accs = [None] * n, [None] * n
        zs = scores(base + n - 1, [n - 1])
        for kb in reversed(range(n)):
            chains = list(range(kb, n))
            zs_next = []
            if kb > 0:
                prefetch = lambda kb=kb: zs_next.extend(scores(base + kb - 1, range(kb - 1, n)))
            else:
                prefetch = lambda: park_scores(0, jnp.maximum(base - 1, 0))
            new_smalls, ws, factors = front_fn(zs, [smalls[a] for a in chains],
                                               [diag_mask if a == kb else None for a in chains], prefetch)
            for a, sm in zip(chains, new_smalls):
                smalls[a] = sm
            if kb > 0:
                vt = vt_ref[0, 0, base + kb]
                for a, w, f in zip(chains, ws, factors):
                    accs[a] = apply_fn(accs[a], _dot(vt, w), f)
                zs = zs_next
            else:
                for a in range(n):
                    w_scr[a] = ws[a]
                accs[0] = jnp.zeros((vt_ref.shape[3], blk), F32)
                factors[0] = unit

        def k_body(it, state):
            smalls, accs, factors = (list(t) for t in state)
            j = base - 1 - K_PER_STEP * it
            for u in range(K_PER_STEP):
                slot, j_cur, j_next = u % 2, j - u, jnp.maximum(j - u - 1, 0)
                accs = flush(j_cur + 1, accs, factors)
                smalls, ws, factors = front_fn([z_scr[slot, a] for a in range(n)], smalls, [None] * n,
                                               lambda: park_scores(1 - slot, j_next))
                for a in range(n):
                    w_scr[a] = ws[a]
            return tuple(smalls), tuple(accs), tuple(factors)

        smalls, accs, factors = lax.fori_loop(
            0, sb * (n // K_PER_STEP), k_body, (tuple(smalls), tuple(accs), tuple(factors)))
        accs = flush(0, accs, factors)
        for a in range(n):
            o_ref[0, 0, base + a] = finish_fn(smalls[a], accs[a])
        return carry

    lax.fori_loop(0, n_super, super_body, 0)


def _block_iotas():
    key_idx = lax.broadcasted_iota(jnp.int32, (SEQ_BLOCK, SEQ_BLOCK), 0)
    qry_idx = lax.broadcasted_iota(jnp.int32, (SEQ_BLOCK, SEQ_BLOCK), 1)
    return key_idx, qry_idx


def _mla_kernel(q_ref, k_ref, vt_ref, o_ref, z_scr, w_scr, *, n_super):
    key_idx, qry_idx = _block_iotas()

    def front(zs, smalls, masks, prefetch):
        prefetch()
        new_smalls, ws, alphas = [], [], []
        for s, small, mask in zip(zs, smalls, masks):
            if mask is not None:
                s = jnp.where(mask, s, -jnp.inf)
            s_max = jnp.max(s, axis=0, keepdims=True)
            if small is None:
                m_new, alpha = s_max, None
            else:
                m_new = jnp.maximum(small[0], s_max)
                alpha = jnp.exp2(small[0] - m_new)
            p = jnp.exp2(s - m_new)
            p_sum = jnp.sum(p, axis=0, keepdims=True)
            new_smalls.append((m_new, p_sum if small is None else alpha * small[1] + p_sum))
            ws.append(p.astype(BF16))
            alphas.append(alpha)
        return new_smalls, ws, alphas

    def apply(acc, pv, alpha):
        return pv if acc is None else alpha * acc + pv

    _attention_driver(q_ref, k_ref, vt_ref, o_ref, z_scr, w_scr, front, apply,
                      lambda small, acc: acc / small[1], key_idx <= qry_idx, n_super)


def _softplus2(z):
    neg_abs = pltpu.bitcast(pltpu.bitcast(z, jnp.uint32) | jnp.uint32(0x80000000), F32)
    return jnp.maximum(z, 0.0) + jnp.log2(1.0 + jnp.exp2(neg_abs))


def _sb_kernel(q_ref, k_ref, vt_ref, tri_ref, o_ref, z_scr, w_scr, *, n_super):
    key_idx, qry_idx = _block_iotas()

    def front(zs, smalls, masks, prefetch):
        log_betas, sps = [], []
        for z, mask in zip(zs, masks):
            sp = _softplus2(z)
            log_betas.append(z - sp)
            if mask is not None:
                sp = jnp.where(mask, sp, 0.0)
            sps.append(sp.astype(BF16))
        sufs = [_dot(tri_ref[...], sp) for sp in sps]
        prefetch()
        new_smalls, ws, factors = [], [], []
        for log_beta, sp, suf, small, mask in zip(log_betas, sps, sufs, smalls, masks):
            a = jnp.exp2(log_beta + suf)
            if mask is not None:
                a = jnp.where(mask, a, 0.0)
            ws.append(a.astype(BF16))
            block_sum = suf[0:1, :] - sp[0:1, :].astype(F32)
            new_smalls.append((block_sum,) if small is None else (small[0] + block_sum,))
            factors.append(None if small is None else jnp.exp2(small[0]))
        return new_smalls, ws, factors

    def apply(acc, pv, factor):
        return pv if acc is None else acc + pv * factor

    _attention_driver(q_ref, k_ref, vt_ref, o_ref, z_scr, w_scr, front, apply,
                      lambda small, acc: acc, key_idx < qry_idx, n_super)


def _out_kernel(x_ref, o_mla_ref, o_sb_ref, g_mla_ref, g_sb_ref, w_o_mla_ref, w_o_sb_ref,
                g_ffn_ref, w_gate_ref, w_up_ref, w_down_ref, g_final_ref, out_ref, *, d_ff):
    blk = SEQ_BLOCK

    def group(o_ref, g_ref, w_ref):
        o = o_ref[0, :, 0].reshape(-1, blk)
        y = (o * _rms_scale(o, 0) * g_ref[...]).astype(BF16)
        return _dot_tn(y, w_ref[...])

    h = x_ref[0] + group(o_mla_ref, g_mla_ref, w_o_mla_ref) + group(o_sb_ref, g_sb_ref, w_o_sb_ref)
    f = (h * _rms_scale(h, -1) * g_ffn_ref[...]).astype(BF16)
    ffn = jnp.zeros_like(h)
    for c in range(0, d_ff, FF_CHUNK):
        gate = _dot(f, w_gate_ref[:, c:c + FF_CHUNK])
        up = _dot(f, w_up_ref[:, c:c + FF_CHUNK])
        act = (gate * jax.nn.sigmoid(gate) * up).astype(BF16)
        ffn = ffn + _dot(act, w_down_ref[c:c + FF_CHUNK, :])
    h = h + ffn
    out_ref[0] = h * _rms_scale(h, -1) * g_final_ref[...]


def _rotate_half_cols(w):
    half = w.shape[-1] // 2
    return jnp.concatenate([-w[..., half:], w[..., :half]], axis=-1)


def _head_slots(nope, rope):
    ref = nope if nope is not None else rope
    r, h = ref.shape[0], ref.shape[1]
    nope = jnp.zeros((r, h, MLA_NOPE_DIM), ref.dtype) if nope is None else nope
    rope = jnp.zeros((r, h, MLA_ROPE_DIM), ref.dtype) if rope is None else rope
    pad = jnp.zeros((r, h, MLA_PAD_DIM - MLA_QK_DIM), ref.dtype)
    return jnp.concatenate([nope, rope, pad], axis=-1).reshape(r, h * MLA_PAD_DIM)


def _const_spec(shape):
    return pl.BlockSpec(shape, lambda *_: (0,) * len(shape))


def kernel(x, positions, norm_mix, w_in, q_latent_norm, w_uq, kv_latent_norm, w_ukv,
           out_norm_mla, out_norm_sb, w_o, norm_ffn, w_gate, w_up, w_down, norm_final):
    b, s, d = x.shape
    depth = w_in.shape[0]
    d_ff = w_gate.shape[-1]
    blk = SEQ_BLOCK
    nb = s // blk
    assert s % (blk * N_CHAINS) == 0 and N_CHAINS % 2 == 0 and d_ff % FF_CHUNK == 0

    inv_freq = ROPE_THETA ** (-jnp.arange(0, MLA_ROPE_DIM, 2, dtype=F32) / MLA_ROPE_DIM)
    zeros = lambda n: jnp.zeros((n,), F32)
    freq = jnp.concatenate([zeros(MLA_NOPE_DIM), inv_freq, inv_freq,
                            zeros(MLA_PAD_DIM - MLA_QK_DIM)])[None, :]
    tri = jnp.where(jnp.arange(blk)[None, :] > jnp.arange(blk)[:, None], -1.0, 0.0).astype(BF16)
    pos = positions.reshape(b, s, 1)

    params = pltpu.CompilerParams(
        dimension_semantics=("arbitrary", "arbitrary"), vmem_limit_bytes=VMEM_LIMIT_BYTES)

    h = x
    for l in range(depth):
        o0 = Q_LORA_RANK
        o1 = o0 + KV_LORA_RANK
        o2 = o1 + MLA_ROPE_DIM
        w_cq, w_ckv, w_kr, w_sb = w_in[l][:, :o0], w_in[l][:, o0:o1], w_in[l][:, o1:o2], w_in[l][:, o2:]
        kr_slot = lambda w: _head_slots(None, w[:, None, :])
        w_lat = jnp.concatenate([w_cq, w_ckv, kr_slot(w_kr), kr_slot(_rotate_half_cols(w_kr))],
                                axis=1).astype(BF16)
        uq = w_uq[l].reshape(Q_LORA_RANK, MLA_HEADS, MLA_QK_DIM)
        uq_nope, uq_rope = uq[..., :MLA_NOPE_DIM], uq[..., MLA_NOPE_DIM:]
        w_uq_lin = _head_slots(uq_nope, uq_rope).astype(BF16)
        w_uq_rot = _head_slots(None, _rotate_half_cols(uq_rope)).astype(BF16)
        ukv = w_ukv[l].reshape(KV_LORA_RANK, MLA_HEADS, MLA_NOPE_DIM + MLA_V_DIM)
        w_uk = _head_slots(ukv[..., :MLA_NOPE_DIM], None).astype(BF16)
        w_uv = ukv[..., MLA_NOPE_DIM:].reshape(KV_LORA_RANK, MLA_WIDTH).astype(BF16)

        head_major = lambda width: jax.ShapeDtypeStruct((b, MLA_HEADS, s, width), BF16)
        head_spec = lambda width: pl.BlockSpec((1, MLA_HEADS, blk, width), lambda bi, ti: (bi, 0, ti, 0))
        vt_shape = jax.ShapeDtypeStruct((b, MLA_HEADS, nb, MLA_V_DIM, blk), BF16)
        vt_spec = pl.BlockSpec((1, MLA_HEADS, 1, MLA_V_DIM, blk), lambda bi, ti: (bi, 0, ti, 0, 0))
        q_mla, k_mla, vt_mla, q_sb, k_sb, vt_sb = pl.pallas_call(
            _proj_kernel,
            grid=(b, nb),
            in_specs=[
                pl.BlockSpec((1, blk, d), lambda bi, ti: (bi, ti, 0)),
                pl.BlockSpec((1, blk, 1), lambda bi, ti: (bi, ti, 0)),
                _const_spec((1, MLA_PAD_DIM)),
                _const_spec((1, d)),
                _const_spec((1, Q_LORA_RANK)),
                _const_spec((1, KV_LORA_RANK)),
                _const_spec(w_lat.shape),
                _const_spec((d, 3 * SB_WIDTH)),
                _const_spec(w_uq_lin.shape),
                _const_spec(w_uq_rot.shape),
                _const_spec(w_uk.shape),
                _const_spec(w_uv.shape),
            ],
            out_specs=[head_spec(MLA_PAD_DIM), head_spec(MLA_PAD_DIM), vt_spec,
                       head_spec(SB_HEAD_DIM), head_spec(SB_HEAD_DIM), vt_spec],
            out_shape=[head_major(MLA_PAD_DIM), head_major(MLA_PAD_DIM), vt_shape,
                       head_major(SB_HEAD_DIM), head_major(SB_HEAD_DIM), vt_shape],
            compiler_params=params,
            name="proj",
        )(h, pos, freq, norm_mix[l][None, :], q_latent_norm[l][None, :], kv_latent_norm[l][None, :],
          w_lat, w_sb.astype(BF16), w_uq_lin, w_uq_rot, w_uk, w_uv)

        def attn_specs(width):
            seq_spec = pl.BlockSpec((1, 1, s, width), lambda bi, hi: (bi, hi, 0, 0))
            blocked = pl.BlockSpec((1, 1, nb, MLA_V_DIM, blk), lambda bi, hi: (bi, hi, 0, 0, 0))
            return seq_spec, blocked

        o_shape = jax.ShapeDtypeStruct((b, MLA_HEADS, nb, MLA_V_DIM, blk), F32)
        score_scratch = [pltpu.VMEM((2, N_CHAINS, blk, blk), F32),
                         pltpu.VMEM((N_CHAINS, blk, blk), BF16)]
        seq_spec, blocked = attn_specs(MLA_PAD_DIM)
        o_mla = pl.pallas_call(
            functools.partial(_mla_kernel, n_super=nb // N_CHAINS),
            grid=(b, MLA_HEADS),
            in_specs=[seq_spec, seq_spec, blocked],
            out_specs=blocked,
            out_shape=o_shape,
            scratch_shapes=score_scratch,
            compiler_params=params,
            name="mla_attn",
        )(q_mla, k_mla, vt_mla)
        seq_spec, blocked = attn_specs(SB_HEAD_DIM)
        o_sb = pl.pallas_call(
            functools.partial(_sb_kernel, n_super=nb // N_CHAINS),
            grid=(b, SB_HEADS),
            in_specs=[seq_spec, seq_spec, blocked, _const_spec((blk, blk))],
            out_specs=blocked,
            out_shape=o_shape,
            scratch_shapes=score_scratch,
            compiler_params=params,
            name="sb_attn",
        )(q_sb, k_sb, vt_sb, tri)

        assert depth == 1
        o_spec = pl.BlockSpec((1, MLA_HEADS, 1, MLA_V_DIM, blk), lambda bi, ti: (bi, 0, ti, 0, 0))
        tok_spec = pl.BlockSpec((1, blk, d), lambda bi, ti: (bi, ti, 0))
        resident = lambda shape: pl.BlockSpec(shape, lambda *_: (0,) * len(shape),
                                              pipeline_mode=pl.Buffered(1))
        h = pl.pallas_call(
            functools.partial(_out_kernel, d_ff=d_ff),
            grid=(b, nb),
            in_specs=[
                tok_spec, o_spec, o_spec,
                _const_spec((MLA_WIDTH, 1)), _const_spec((SB_WIDTH, 1)),
                resident((MLA_WIDTH, d)), resident((SB_WIDTH, d)),
                _const_spec((1, d)),
                resident((d, d_ff)), resident((d, d_ff)), resident((d_ff, d)),
                _const_spec((1, d)),
            ],
            out_specs=tok_spec,
            out_shape=jax.ShapeDtypeStruct((b, s, d), F32),
            compiler_params=params,
            name="out_ffn",
        )(h, o_mla, o_sb, out_norm_mla[l][:, None], out_norm_sb[l][:, None],
          w_o[l][:MLA_WIDTH].astype(BF16), w_o[l][MLA_WIDTH:].astype(BF16),
          norm_ffn[l][None, :], w_gate[l].astype(BF16), w_up[l].astype(BF16),
          w_down[l].astype(BF16), norm_final[None, :])
    return h
```

```python
import functools
import math

import jax
import jax.numpy as jnp
from jax import lax
from jax.experimental import pallas as pl
from jax.experimental.pallas import tpu as pltpu

EPS = 1e-6
ROPE_THETA = 10000.0
LOG2_E = 1.4426950408889634

MLA_HEADS = 8
MLA_NOPE_DIM = 64
MLA_ROPE_DIM = 32
MLA_V_DIM = 64
MLA_QK_DIM = MLA_NOPE_DIM + MLA_ROPE_DIM
MLA_PAD_DIM = 128
Q_LORA_RANK = 256
KV_LORA_RANK = 128
SB_HEADS = 8
SB_HEAD_DIM = 64
MLA_WIDTH = MLA_HEADS * MLA_V_DIM
SB_WIDTH = SB_HEADS * SB_HEAD_DIM

SEQ_BLOCK = 256
N_CHAINS = 4
MLA_K_PER_STEP = 4
SB_K_PER_STEP = 2
SB_DEAD_BITS = 160.0
FF_CHUNK = 256
VMEM_LIMIT_BYTES = 56 * 1024 * 1024

F32 = jnp.float32
BF16 = jnp.bfloat16


def _rms_scale(v, axis):
    return lax.rsqrt(jnp.mean(v * v, axis=axis, keepdims=True) + EPS)


def _dot(a, b):
    return jnp.dot(a, b, preferred_element_type=F32)


def _dot_nt(a, b):
    return lax.dot_general(a, b, (((1,), (1,)), ((), ())), preferred_element_type=F32)


def _dot_tn(a, b):
    return lax.dot_general(a, b, (((0,), (0,)), ((), ())), preferred_element_type=F32)


def _proj_kernel(x_ref, pos_ref, freq_ref, g_mix_ref, g_q_ref, g_kv_ref,
                 w_lat_ref, w_sb_ref, w_uq_ref, w_uq_rot_ref, w_uk_ref, w_uv_ref,
                 q_mla_ref, k_mla_ref, vt_mla_ref, q_sb_ref, k_sb_ref, vt_sb_ref):
    x = x_ref[0]
    u = (x * _rms_scale(x, -1) * g_mix_ref[...]).astype(BF16)

    ang = pos_ref[0].astype(F32) * freq_ref[...]
    cos = jnp.cos(ang)
    sin = jnp.sin(ang)

    lat = _dot(u, w_lat_ref[...])
    c_q = lat[:, :Q_LORA_RANK]
    c_kv = lat[:, Q_LORA_RANK:Q_LORA_RANK + KV_LORA_RANK]
    k_r = lat[:, Q_LORA_RANK + KV_LORA_RANK:Q_LORA_RANK + KV_LORA_RANK + MLA_PAD_DIM]
    k_r_rot = lat[:, Q_LORA_RANK + KV_LORA_RANK + MLA_PAD_DIM:]
    k_rope = k_r * cos + k_r_rot * sin

    ql = (c_q * _rms_scale(c_q, -1) * g_q_ref[...]).astype(BF16)
    kvl = (c_kv * _rms_scale(c_kv, -1) * g_kv_ref[...]).astype(BF16)

    q_lin = _dot(ql, w_uq_ref[...])
    q_rot = _dot(ql, w_uq_rot_ref[...])
    k_nope = _dot(kvl, w_uk_ref[...])
    v_mla = _dot(kvl, w_uv_ref[...])
    q_scale = LOG2_E / math.sqrt(MLA_QK_DIM)
    for h in range(MLA_HEADS):
        sl = slice(h * MLA_PAD_DIM, (h + 1) * MLA_PAD_DIM)
        q_h = (q_lin[:, sl] * cos + q_rot[:, sl] * sin) * q_scale
        q_mla_ref[0, h] = q_h.astype(BF16)
        k_mla_ref[0, h] = (k_nope[:, sl] + k_rope).astype(BF16)
    v_mla_t = v_mla.T
    for h in range(MLA_HEADS):
        vt_mla_ref[0, h, 0] = v_mla_t[h * MLA_V_DIM:(h + 1) * MLA_V_DIM, :].astype(BF16)

    sb = _dot(u, w_sb_ref[...])
    sb_scale = LOG2_E / math.sqrt(SB_HEAD_DIM)
    q_sb = sb[:, :SB_WIDTH] * sb_scale
    k_sb = sb[:, SB_WIDTH:2 * SB_WIDTH]
    v_sb_t = sb[:, 2 * SB_WIDTH:].T
    for h in range(SB_HEADS):
        sl = slice(h * SB_HEAD_DIM, (h + 1) * SB_HEAD_DIM)
        q_sb_ref[0, h] = q_sb[:, sl].astype(BF16)
        k_sb_ref[0, h] = k_sb[:, sl].astype(BF16)
        vt_sb_ref[0, h, 0] = v_sb_t[sl, :].astype(BF16)


def _attention_driver(q_ref, k_ref, vt_ref, o_ref, z_scr, w_scr, front_fn, apply_fn, finish_fn,
                      diag_mask, n_super, k_per_step, live_fn=None):
    blk, n = SEQ_BLOCK, N_CHAINS
    unit = jnp.ones((1, blk), F32)

    def rows(ref, j):
        return ref[0, 0, pl.ds(pl.multiple_of(j * blk, blk), blk), :]

    def super_body(sb, carry):
        base = sb * n

        def scores(j, chains):
            k = rows(k_ref, j)
            return [_dot_nt(k, rows(q_ref, base + a)) for a in chains]

        def park_scores(slot, j):
            for a, z in enumerate(scores(j, range(n))):
                z_scr[slot, a] = z

        def flush(j, accs, factors):
            vt = vt_ref[0, 0, j]
            return [apply_fn(acc, _dot(vt, w_scr[a]), f) for a, (acc, f) in enumerate(zip(accs, factors))]

        smalls, accs = [None] * n, [None] * n
        zs = scores(base + n - 1, [n - 1])
        for kb in reversed(range(n)):
            chains = list(range(kb, n))
            zs_next = []
            if kb > 0:
                prefetch = lambda kb=kb: zs_next.extend(scores(base + kb - 1, range(kb - 1, n)))
            else:
                prefetch = lambda: park_scores(0, jnp.maximum(base - 1, 0))
            new_smalls, ws, factors = front_fn(zs, [smalls[a] for a in chains],
                                               [diag_mask if a == kb else None for a in chains], prefetch)
            for a, sm in zip(chains, new_smalls):
                smalls[a] = sm
            if kb > 0:
                vt = vt_ref[0, 0, base + kb]
                for a, w, f in zip(chains, ws, factors):
                    accs[a] = apply_fn(accs[a], _dot(vt, w), f)
                zs = zs_next
            else:
                for a in range(n):
                    w_scr[a] = ws[a]
                accs[0] = jnp.zeros((vt_ref.shape[3], blk), F32)
                factors[0] = unit

        def k_body(it, state):
            smalls, accs, factors = (list(t) for t in state)
            j = base - 1 - k_per_step * it
            live = None
            for u in range(k_per_step):
                slot, j_cur, j_next = u % 2, j - u, jnp.maximum(j - u - 1, 0)
                accs = flush(j_cur + 1, accs, factors)
                smalls, ws, factors = front_fn([z_scr[slot, a] for a in range(n)], smalls, [None] * n,
                                               lambda: park_scores(1 - slot, j_next))
                for a in range(n):
                    w_scr[a] = ws[a]
                if u == 0 and live_fn is not None:
                    live = live_fn(smalls)
            return live, (tuple(smalls), tuple(accs), tuple(factors))

        steps = sb * (n // k_per_step)
        state = (tuple(smalls), tuple(accs), tuple(factors))
        if live_fn is None:
            state = lax.fori_loop(0, steps, lambda it, st: k_body(it, st)[1], state)
            it = steps
        else:
            def w_body(carry):
                live, st = k_body(carry[0], carry[2])
                return carry[0] + 1, live, st

            it, _, state = lax.while_loop(
                lambda carry: jnp.logical_and(carry[0] < steps, carry[1] > 0), w_body,
                (jnp.int32(0), live_fn(smalls), state))
        smalls, accs, factors = state
        accs = flush(base - k_per_step * it, accs, factors)
        for a in range(n):
            o_ref[0, 0, base + a] = finish_fn(smalls[a], accs[a])
        return carry

    lax.fori_loop(0, n_super, super_body, 0)


def _block_iotas():
    key_idx = lax.broadcasted_iota(jnp.int32, (SEQ_BLOCK, SEQ_BLOCK), 0)
    qry_idx = lax.broadcasted_iota(jnp.int32, (SEQ_BLOCK, SEQ_BLOCK), 1)
    return key_idx, qry_idx


def _mla_kernel(q_ref, k_ref, vt_ref, o_ref, z_scr, w_scr, *, n_super):
    key_idx, qry_idx = _block_iotas()

    def front(zs, smalls, masks, prefetch):
        prefetch()
        new_smalls, ws, alphas = [], [], []
        for s, small, mask in zip(zs, smalls, masks):
            if mask is not None:
                s = jnp.where(mask, s, -jnp.inf)
            s_max = jnp.max(s, axis=0, keepdims=True)
            if small is None:
                m_new, alpha = s_max, None
            else:
                m_new = jnp.maximum(small[0], s_max)
                alpha = jnp.exp2(small[0] - m_new)
            p = jnp.exp2(s - m_new)
            p_sum = jnp.sum(p, axis=0, keepdims=True)
            new_smalls.append((m_new, p_sum if small is None else alpha * small[1] + p_sum))
            ws.append(p.astype(BF16))
            alphas.append(alpha)
        return new_smalls, ws, alphas

    def apply(acc, pv, alpha):
        return pv if acc is None else alpha * acc + pv

    _attention_driver(q_ref, k_ref, vt_ref, o_ref, z_scr, w_scr, front, apply,
                      lambda small, acc: acc / small[1], key_idx <= qry_idx, n_super, MLA_K_PER_STEP)


def _softplus2(z):
    return jnp.maximum(z, 0.0) + jnp.log2(1.0 + jnp.exp2(-jnp.abs(z)))


def _sb_kernel(q_ref, k_ref, vt_ref, tri_ref, o_ref, z_scr, w_scr, *, n_super):
    key_idx, qry_idx = _block_iotas()

    def front(zs, smalls, masks, prefetch):
        log_betas, sps = [], []
        for z, mask in zip(zs, masks):
            sp = _softplus2(z)
            log_betas.append(z - sp)
            if mask is not None:
                sp = jnp.where(mask, sp, 0.0)
            sps.append(sp.astype(BF16))
        sufs = [_dot(tri_ref[...], sp) for sp in sps]
        prefetch()
        new_smalls, ws, factors = [], [], []
        for log_beta, sp, suf, small, mask in zip(log_betas, sps, sufs, smalls, masks):
            a = jnp.exp2(log_beta + suf)
            if mask is not None:
                a = jnp.where(mask, a, 0.0)
            ws.append(a.astype(BF16))
            block_sum = suf[0:1, :] - sp[0:1, :].astype(F32)
            new_smalls.append((block_sum,) if small is None else (small[0] + block_sum,))
            factors.append(None if small is None else jnp.exp2(small[0]))
        return new_smalls, ws, factors

    def apply(acc, pv, factor):
        return pv if acc is None else acc + pv * factor

    def live(smalls):
        c_max = functools.reduce(jnp.maximum, [small[0] for small in smalls])
        return (jnp.max(c_max) > -SB_DEAD_BITS).astype(jnp.int32)

    _attention_driver(q_ref, k_ref, vt_ref, o_ref, z_scr, w_scr, front, apply,
                      lambda small, acc: acc, key_idx < qry_idx, n_super, SB_K_PER_STEP, live)


def _out_kernel(x_ref, o_mla_ref, o_sb_ref, g_mla_ref, g_sb_ref, w_o_mla_ref, w_o_sb_ref,
                g_ffn_ref, w_gate_ref, w_up_ref, w_down_ref, g_final_ref, out_ref, *, d_ff):
    blk = SEQ_BLOCK

    def group(o_ref, g_ref, w_ref):
        o = o_ref[0, :, 0].reshape(-1, blk)
        y = (o * _rms_scale(o, 0) * g_ref[...]).astype(BF16)
        return _dot_tn(y, w_ref[...])

    h = x_ref[0] + group(o_mla_ref, g_mla_ref, w_o_mla_ref) + group(o_sb_ref, g_sb_ref, w_o_sb_ref)
    f = (h * _rms_scale(h, -1) * g_ffn_ref[...]).astype(BF16)
    ffn = jnp.zeros_like(h)
    for c in range(0, d_ff, FF_CHUNK):
        gate = _dot(f, w_gate_ref[:, c:c + FF_CHUNK])
        up = _dot(f, w_up_ref[:, c:c + FF_CHUNK])
        act = (gate * jax.nn.sigmoid(gate) * up).astype(BF16)
        ffn = ffn + _dot(act, w_down_ref[c:c + FF_CHUNK, :])
    h = h + ffn
    out_ref[0] = h * _rms_scale(h, -1) * g_final_ref[...]


def _rotate_half_cols(w):
    half = w.shape[-1] // 2
    return jnp.concatenate([-w[..., half:], w[..., :half]], axis=-1)


def _head_slots(nope, rope):
    ref = nope if nope is not None else rope
    r, h = ref.shape[0], ref.shape[1]
    nope = jnp.zeros((r, h, MLA_NOPE_DIM), ref.dtype) if nope is None else nope
    rope = jnp.zeros((r, h, MLA_ROPE_DIM), ref.dtype) if rope is None else rope
    pad = jnp.zeros((r, h, MLA_PAD_DIM - MLA_QK_DIM), ref.dtype)
    return jnp.concatenate([nope, rope, pad], axis=-1).reshape(r, h * MLA_PAD_DIM)


def _const_spec(shape):
    return pl.BlockSpec(shape, lambda *_: (0,) * len(shape))


def kernel(x, positions, norm_mix, w_in, q_latent_norm, w_uq, kv_latent_norm, w_ukv,
           out_norm_mla, out_norm_sb, w_o, norm_ffn, w_gate, w_up, w_down, norm_final):
    b, s, d = x.shape
    depth = w_in.shape[0]
    d_ff = w_gate.shape[-1]
    blk = SEQ_BLOCK
    nb = s // blk
    assert s % (blk * N_CHAINS) == 0 and N_CHAINS % 2 == 0 and d_ff % FF_CHUNK == 0

    inv_freq = ROPE_THETA ** (-jnp.arange(0, MLA_ROPE_DIM, 2, dtype=F32) / MLA_ROPE_DIM)
    zeros = lambda n: jnp.zeros((n,), F32)
    freq = jnp.concatenate([zeros(MLA_NOPE_DIM), inv_freq, inv_freq,
                            zeros(MLA_PAD_DIM - MLA_QK_DIM)])[None, :]
    tri = jnp.where(jnp.arange(blk)[None, :] > jnp.arange(blk)[:, None], -1.0, 0.0).astype(BF16)
    pos = positions.reshape(b, s, 1)

    params = pltpu.CompilerParams(
        dimension_semantics=("arbitrary", "arbitrary"), vmem_limit_bytes=VMEM_LIMIT_BYTES)

    h = x
    for l in range(depth):
        o0 = Q_LORA_RANK
        o1 = o0 + KV_LORA_RANK
        o2 = o1 + MLA_ROPE_DIM
        w_cq, w_ckv, w_kr, w_sb = w_in[l][:, :o0], w_in[l][:, o0:o1], w_in[l][:, o1:o2], w_in[l][:, o2:]
        kr_slot = lambda w: _head_slots(None, w[:, None, :])
        w_lat = jnp.concatenate([w_cq, w_ckv, kr_slot(w_kr), kr_slot(_rotate_half_cols(w_kr))],
                                axis=1).astype(BF16)
        uq = w_uq[l].reshape(Q_LORA_RANK, MLA_HEADS, MLA_QK_DIM)
        uq_nope, uq_rope = uq[..., :MLA_NOPE_DIM], uq[..., MLA_NOPE_DIM:]
        w_uq_lin = _head_slots(uq_nope, uq_rope).astype(BF16)
        w_uq_rot = _head_slots(None, _rotate_half_cols(uq_rope)).astype(BF16)
        ukv = w_ukv[l].reshape(KV_LORA_RANK, MLA_HEADS, MLA_NOPE_DIM + MLA_V_DIM)
        w_uk = _head_slots(ukv[..., :MLA_NOPE_DIM], None).astype(BF16)
        w_uv = ukv[..., MLA_NOPE_DIM:].reshape(KV_LORA_RANK, MLA_WIDTH).astype(BF16)

        head_major = lambda width: jax.ShapeDtypeStruct((b, MLA_HEADS, s, width), BF16)
        head_spec = lambda width: pl.BlockSpec((1, MLA_HEADS, blk, width), lambda bi, ti: (bi, 0, ti, 0))
        vt_shape = jax.ShapeDtypeStruct((b, MLA_HEADS, nb, MLA_V_DIM, blk), BF16)
        vt_spec = pl.BlockSpec((1, MLA_HEADS, 1, MLA_V_DIM, blk), lambda bi, ti: (bi, 0, ti, 0, 0))
        q_mla, k_mla, vt_mla, q_sb, k_sb, vt_sb = pl.pallas_call(
            _proj_kernel,
            grid=(b, nb),
            in_specs=[
                pl.BlockSpec((1, blk, d), lambda bi, ti: (bi, ti, 0)),
                pl.BlockSpec((1, blk, 1), lambda bi, ti: (bi, ti, 0)),
                _const_spec((1, MLA_PAD_DIM)),
                _const_spec((1, d)),
                _const_spec((1, Q_LORA_RANK)),
                _const_spec((1, KV_LORA_RANK)),
                _const_spec(w_lat.shape),
                _const_spec((d, 3 * SB_WIDTH)),
                _const_spec(w_uq_lin.shape),
                _const_spec(w_uq_rot.shape),
                _const_spec(w_uk.shape),
                _const_spec(w_uv.shape),
            ],
            out_specs=[head_spec(MLA_PAD_DIM), head_spec(MLA_PAD_DIM), vt_spec,
                       head_spec(SB_HEAD_DIM), head_spec(SB_HEAD_DIM), vt_spec],
            out_shape=[head_major(MLA_PAD_DIM), head_major(MLA_PAD_DIM), vt_shape,
                       head_major(SB_HEAD_DIM), head_major(SB_HEAD_DIM), vt_shape],
            compiler_params=params,
            name="proj",
        )(h, pos, freq, norm_mix[l][None, :], q_latent_norm[l][None, :], kv_latent_norm[l][None, :],
          w_lat, w_sb.astype(BF16), w_uq_lin, w_uq_rot, w_uk, w_uv)

        def attn_specs(width):
            seq_spec = pl.BlockSpec((1, 1, s, width), lambda bi, hi: (bi, hi, 0, 0))
            blocked = pl.BlockSpec((1, 1, nb, MLA_V_DIM, blk), lambda bi, hi: (bi, hi, 0, 0, 0))
            return seq_spec, blocked

        o_shape = jax.ShapeDtypeStruct((b, MLA_HEADS, nb, MLA_V_DIM, blk), F32)
        score_scratch = [pltpu.VMEM((2, N_CHAINS, blk, blk), F32),
                         pltpu.VMEM((N_CHAINS, blk, blk), BF16)]
        seq_spec, blocked = attn_specs(MLA_PAD_DIM)
        o_mla = pl.pallas_call(
            functools.partial(_mla_kernel, n_super=nb // N_CHAINS),
            grid=(b, MLA_HEADS),
            in_specs=[seq_spec, seq_spec, blocked],
            out_specs=blocked,
            out_shape=o_shape,
            scratch_shapes=score_scratch,
            compiler_params=params,
            name="mla_attn",
        )(q_mla, k_mla, vt_mla)
        seq_spec, blocked = attn_specs(SB_HEAD_DIM)
        o_sb = pl.pallas_call(
            functools.partial(_sb_kernel, n_super=nb // N_CHAINS),
            grid=(b, SB_HEADS),
            in_specs=[seq_spec, seq_spec, blocked, _const_spec((blk, blk))],
            out_specs=blocked,
            out_shape=o_shape,
            scratch_shapes=score_scratch,
            compiler_params=params,
            name="sb_attn",
        )(q_sb, k_sb, vt_sb, tri)

        assert depth == 1
        o_spec = pl.BlockSpec((1, MLA_HEADS, 1, MLA_V_DIM, blk), lambda bi, ti: (bi, 0, ti, 0, 0))
        tok_spec = pl.BlockSpec((1, blk, d), lambda bi, ti: (bi, ti, 0))
        resident = lambda shape: pl.BlockSpec(shape, lambda *_: (0,) * len(shape),
                                              pipeline_mode=pl.Buffered(1))
        h = pl.pallas_call(
            functools.partial(_out_kernel, d_ff=d_ff),
            grid=(b, nb),
            in_specs=[
                tok_spec, o_spec, o_spec,
                _const_spec((MLA_WIDTH, 1)), _const_spec((SB_WIDTH, 1)),
                resident((MLA_WIDTH, d)), resident((SB_WIDTH, d)),
                _const_spec((1, d)),
                resident((d, d_ff)), resident((d, d_ff)), resident((d_ff, d)),
                _const_spec((1, d)),
            ],
            out_specs=tok_spec,
            out_shape=jax.ShapeDtypeStruct((b, s, d), F32),
            compiler_params=params,
            name="out_ffn",
        )(h, o_mla, o_sb, out_norm_mla[l][:, None], out_norm_sb[l][:, None],
          w_o[l][:MLA_WIDTH].astype(BF16), w_o[l][MLA_WIDTH:].astype(BF16),
          norm_ffn[l][None, :], w_gate[l].astype(BF16), w_up[l].astype(BF16),
          w_down[l].astype(BF16), norm_final[None, :])
    return h
```

```python
import functools
import math

import jax
import jax.numpy as jnp
from jax import lax
from jax.experimental import pallas as pl
from jax.experimental.pallas import tpu as pltpu

EPS = 1e-6
ROPE_THETA = 10000.0
LOG2_E = 1.4426950408889634

MLA_HEADS = 8
MLA_NOPE_DIM = 64
MLA_ROPE_DIM = 32
MLA_V_DIM = 64
MLA_QK_DIM = MLA_NOPE_DIM + MLA_ROPE_DIM
MLA_PAD_DIM = 128
Q_LORA_RANK = 256
KV_LORA_RANK = 128
SB_HEADS = 8
SB_HEAD_DIM = 64
MLA_WIDTH = MLA_HEADS * MLA_V_DIM
SB_WIDTH = SB_HEADS * SB_HEAD_DIM

SEQ_BLOCK = 256
N_CHAINS = 4
MLA_K_PER_STEP = 4
SB_DEAD_BITS = 160.0
FF_CHUNK = 256
OUT_BLOCKS = 2
PROJ_BLOCKS = 2
VMEM_LIMIT_BYTES = 56 * 1024 * 1024

F32 = jnp.float32
BF16 = jnp.bfloat16


def _rms_scale(v, axis):
    return lax.rsqrt(jnp.mean(v * v, axis=axis, keepdims=True) + EPS)


def _dot(a, b):
    return jnp.dot(a, b, preferred_element_type=F32)


def _dot_nt(a, b):
    return lax.dot_general(a, b, (((1,), (1,)), ((), ())), preferred_element_type=F32)


def _dot_tn(a, b):
    return lax.dot_general(a, b, (((0,), (0,)), ((), ())), preferred_element_type=F32)


def _proj_kernel(x_ref, pos_ref, freq_ref, g_mix_ref, g_q_ref, g_kv_ref,
                 w_lat_ref, w_sb_ref, w_uq_ref, w_uq_rot_ref, w_uk_ref, w_uv_ref,
                 q_mla_ref, k_mla_ref, vt_mla_ref, q_sb_ref, k_sb_ref, vt_sb_ref):
    x = x_ref[0]
    u = (x * _rms_scale(x, -1) * g_mix_ref[...]).astype(BF16)

    ang = pos_ref[0].astype(F32) * freq_ref[...]
    cos = jnp.cos(ang)
    sin = jnp.sin(ang)

    lat = _dot(u, w_lat_ref[...])
    c_q = lat[:, :Q_LORA_RANK]
    c_kv = lat[:, Q_LORA_RANK:Q_LORA_RANK + KV_LORA_RANK]
    k_r = lat[:, Q_LORA_RANK + KV_LORA_RANK:Q_LORA_RANK + KV_LORA_RANK + MLA_PAD_DIM]
    k_r_rot = lat[:, Q_LORA_RANK + KV_LORA_RANK + MLA_PAD_DIM:]
    k_rope = k_r * cos + k_r_rot * sin

    ql = (c_q * _rms_scale(c_q, -1) * g_q_ref[...]).astype(BF16)
    kvl = (c_kv * _rms_scale(c_kv, -1) * g_kv_ref[...]).astype(BF16)

    q_lin = _dot(ql, w_uq_ref[...])
    q_rot = _dot(ql, w_uq_rot_ref[...])
    k_nope = _dot(kvl, w_uk_ref[...])
    v_mla = _dot(kvl, w_uv_ref[...])
    q_scale = LOG2_E / math.sqrt(MLA_QK_DIM)
    for h in range(MLA_HEADS):
        sl = slice(h * MLA_PAD_DIM, (h + 1) * MLA_PAD_DIM)
        q_h = (q_lin[:, sl] * cos + q_rot[:, sl] * sin) * q_scale
        q_mla_ref[0, h] = q_h.astype(BF16)
        k_mla_ref[0, h] = (k_nope[:, sl] + k_rope).astype(BF16)
    v_mla_t = v_mla.T
    for h in range(MLA_HEADS):
        for t in range(PROJ_BLOCKS):
            vt_mla_ref[0, h, t] = v_mla_t[h * MLA_V_DIM:(h + 1) * MLA_V_DIM,
                                          t * SEQ_BLOCK:(t + 1) * SEQ_BLOCK].astype(BF16)

    sb = _dot(u, w_sb_ref[...])
    sb_scale = LOG2_E / math.sqrt(SB_HEAD_DIM)
    q_sb = sb[:, :SB_WIDTH] * sb_scale
    k_sb = sb[:, SB_WIDTH:2 * SB_WIDTH]
    v_sb_t = sb[:, 2 * SB_WIDTH:].T
    for h in range(SB_HEADS):
        sl = slice(h * SB_HEAD_DIM, (h + 1) * SB_HEAD_DIM)
        q_sb_ref[0, h] = q_sb[:, sl].astype(BF16)
        k_sb_ref[0, h] = k_sb[:, sl].astype(BF16)
        for t in range(PROJ_BLOCKS):
            vt_sb_ref[0, h, t] = v_sb_t[sl, t * SEQ_BLOCK:(t + 1) * SEQ_BLOCK].astype(BF16)


def _attention_driver(q_ref, k_ref, vt_ref, o_ref, z_scr, w_scr, front_fn, apply_fn, finish_fn,
                      diag_mask, n_super, k_per_step):
    blk, n = SEQ_BLOCK, N_CHAINS
    unit = jnp.ones((1, blk), F32)

    def rows(ref, j):
        return ref[0, 0, pl.ds(pl.multiple_of(j * blk, blk), blk), :]

    def super_body(sb, carry):
        base = sb * n

        def scores(j, chains):
            k = rows(k_ref, j)
            return [_dot_nt(k, rows(q_ref, base + a)) for a in chains]

        def park_scores(slot, j):
            for a, z in enumerate(scores(j, range(n))):
                z_scr[slot, a] = z

        def flush(j, accs, factors):
            vt = vt_ref[0, 0, j]
            return [apply_fn(acc, _dot(vt, w_scr[a]), f) for a, (acc, f) in enumerate(zip(accs, factors))]

        smalls, accs = [None] * n, [None] * n
        zs = scores(base + n - 1, [n - 1])
        for kb in reversed(range(n)):
            chains = list(range(kb, n))
            zs_next = []
            if kb > 0:
                prefetch = lambda kb=kb: zs_next.extend(scores(base + kb - 1, range(kb - 1, n)))
            else:
                prefetch = lambda: park_scores(0, jnp.maximum(base - 1, 0))
            new_smalls, ws, factors = front_fn(zs, [smalls[a] for a in chains],
                                               [diag_mask if a == kb else None for a in chains], prefetch)
            for a, sm in zip(chains, new_smalls):
                smalls[a] = sm
            if kb > 0:
                vt = vt_ref[0, 0, base + kb]
                for a, w, f in zip(chains, ws, factors):
                    accs[a] = apply_fn(accs[a], _dot(vt, w), f)
                zs = zs_next
            else:
                for a in range(n):
                    w_scr[a] = ws[a]
                accs[0] = jnp.zeros((vt_ref.shape[3], blk), F32)
                factors[0] = unit

        def k_body(it, state):
            smalls, accs, factors = (list(t) for t in state)
            j = base - 1 - k_per_step * it
            for u in range(k_per_step):
                slot, j_cur, j_next = u % 2, j - u, jnp.maximum(j - u - 1, 0)
                accs = flush(j_cur + 1, accs, factors)
                smalls, ws, factors = front_fn([z_scr[slot, a] for a in range(n)], smalls, [None] * n,
                                               lambda: park_scores(1 - slot, j_next))
                for a in range(n):
                    w_scr[a] = ws[a]
            return tuple(smalls), tuple(accs), tuple(factors)

        smalls, accs, factors = lax.fori_loop(
            0, sb * (n // k_per_step), k_body, (tuple(smalls), tuple(accs), tuple(factors)))
        accs = flush(0, accs, factors)
        for a in range(n):
            o_ref[0, 0, base + a] = finish_fn(smalls[a], accs[a])
        return carry

    lax.fori_loop(0, n_super, super_body, 0)


def _block_iotas():
    key_idx = lax.broadcasted_iota(jnp.int32, (SEQ_BLOCK, SEQ_BLOCK), 0)
    qry_idx = lax.broadcasted_iota(jnp.int32, (SEQ_BLOCK, SEQ_BLOCK), 1)
    return key_idx, qry_idx


def _mla_kernel(q_ref, k_ref, vt_ref, o_ref, z_scr, w_scr, *, n_super):
    key_idx, qry_idx = _block_iotas()

    def front(zs, smalls, masks, prefetch):
        prefetch()
        new_smalls, ws, alphas = [], [], []
        for s, small, mask in zip(zs, smalls, masks):
            if mask is not None:
                s = jnp.where(mask, s, -jnp.inf)
            s_max = jnp.max(s, axis=0, keepdims=True)
            if small is None:
                m_new, alpha = s_max, None
            else:
                m_new = jnp.maximum(small[0], s_max)
                alpha = jnp.exp2(small[0] - m_new)
            p = jnp.exp2(s - m_new)
            p_sum = jnp.sum(p, axis=0, keepdims=True)
            new_smalls.append((m_new, p_sum if small is None else alpha * small[1] + p_sum))
            ws.append(p.astype(BF16))
            alphas.append(alpha)
        return new_smalls, ws, alphas

    def apply(acc, pv, alpha):
        return pv if acc is None else alpha * acc + pv

    _attention_driver(q_ref, k_ref, vt_ref, o_ref, z_scr, w_scr, front, apply,
                      lambda small, acc: acc / small[1], key_idx <= qry_idx, n_super, MLA_K_PER_STEP)


def _softplus2(z):
    return jnp.maximum(z, 0.0) + jnp.log2(1.0 + jnp.exp2(-jnp.abs(z)))


def _sb_kernel(q_ref, k_ref, vt_ref, tri_ref, o_ref, *, n_super):
    blk, n = SEQ_BLOCK, N_CHAINS
    key_idx, qry_idx = _block_iotas()
    strict = key_idx < qry_idx

    def rows(ref, j):
        return ref[0, 0, pl.ds(pl.multiple_of(j * blk, blk), blk), :]

    def tiles(base, specs):
        zs = [_dot_nt(rows(k_ref, j), rows(q_ref, base + a)) for a, j, _ in specs]
        log_betas, sps = [], []
        for z, (_, _, diag) in zip(zs, specs):
            sp = _softplus2(z)
            log_betas.append(z - sp)
            if diag:
                sp = jnp.where(strict, sp, 0.0)
            sps.append(sp.astype(BF16))
        sufs = [_dot(tri_ref[...], sp) for sp in sps]
        ws = []
        for log_beta, suf, (_, _, diag) in zip(log_betas, sufs, specs):
            w = jnp.exp2(log_beta + suf)
            if diag:
                w = jnp.where(strict, w, 0.0)
            ws.append(w.astype(BF16))
        return [(suf[0:1, :] - sp[0:1, :].astype(F32), _dot(vt_ref[0, 0, j], w))
                for w, sp, suf, (_, j, _) in zip(ws, sps, sufs, specs)]

    def band(base, d):
        return [(a, jnp.maximum(base + a - d, 0), False) for a in range(n)]

    def factor(base, a, d, c):
        return jnp.where(base + a - d >= 0, jnp.exp2(c), 0.0)

    def live(cs):
        return (jnp.max(functools.reduce(jnp.maximum, cs)) > -SB_DEAD_BITS).astype(jnp.int32)

    def super_body(sb, carry):
        base = sb * n
        res = tiles(base, [(a, base + a, True) for a in range(n)] + band(base, 1))
        cs, accs = [], []
        for a in range(n):
            (sum0, pv0), (sum1, pv1) = res[a], res[n + a]
            accs.append(pv0 + pv1 * factor(base, a, 1, sum0))
            cs.append(sum0 + sum1)

        def w_body(state):
            d, _, cs, accs = state
            res = tiles(base, band(base, d))
            accs = tuple(acc + pv * factor(base, a, d, c)
                         for a, (acc, c, (_, pv)) in enumerate(zip(accs, cs, res)))
            cs = tuple(c + block_sum for c, (block_sum, _) in zip(cs, res))
            return d + 1, live(cs), cs, accs

        _, _, _, accs = lax.while_loop(
            lambda state: jnp.logical_and(state[0] < base + n, state[1] > 0), w_body,
            (jnp.int32(2), live(cs), tuple(cs), tuple(accs)))
        for a in range(n):
            o_ref[0, 0, base + a] = accs[a]
        return carry

    lax.fori_loop(0, n_super, super_body, 0)


def _out_kernel(x_ref, o_mla_ref, o_sb_ref, g_mla_ref, g_sb_ref, w_o_mla_ref, w_o_sb_ref,
                g_ffn_ref, w_gate_ref, w_up_ref, w_down_ref, g_final_ref, out_ref, *, d_ff):
    blk = SEQ_BLOCK

    def group(o_ref, g_ref, w_ref):
        parts = []
        for t in range(OUT_BLOCKS):
            o = o_ref[0, :, t].reshape(-1, blk)
            y = (o * _rms_scale(o, 0) * g_ref[...]).astype(BF16)
            parts.append(_dot_tn(y, w_ref[...]))
        return jnp.concatenate(parts, axis=0)

    h = x_ref[0] + group(o_mla_ref, g_mla_ref, w_o_mla_ref) + group(o_sb_ref, g_sb_ref, w_o_sb_ref)
    f = (h * _rms_scale(h, -1) * g_ffn_ref[...]).astype(BF16)
    ffn = jnp.zeros_like(h)
    for c in range(0, d_ff, FF_CHUNK):
        gate = _dot(f, w_gate_ref[:, c:c + FF_CHUNK])
        up = _dot(f, w_up_ref[:, c:c + FF_CHUNK])
        act = (gate * jax.nn.sigmoid(gate) * up).astype(BF16)
        ffn = ffn + _dot(act, w_down_ref[c:c + FF_CHUNK, :])
    h = h + ffn
    out_ref[0] = h * _rms_scale(h, -1) * g_final_ref[...]


def _rotate_half_cols(w):
    half = w.shape[-1] // 2
    return jnp.concatenate([-w[..., half:], w[..., :half]], axis=-1)


def _head_slots(nope, rope):
    ref = nope if nope is not None else rope
    r, h = ref.shape[0], ref.shape[1]
    nope = jnp.zeros((r, h, MLA_NOPE_DIM), ref.dtype) if nope is None else nope
    rope = jnp.zeros((r, h, MLA_ROPE_DIM), ref.dtype) if rope is None else rope
    pad = jnp.zeros((r, h, MLA_PAD_DIM - MLA_QK_DIM), ref.dtype)
    return jnp.concatenate([nope, rope, pad], axis=-1).reshape(r, h * MLA_PAD_DIM)


def _const_spec(shape):
    return pl.BlockSpec(shape, lambda *_: (0,) * len(shape))


def kernel(x, positions, norm_mix, w_in, q_latent_norm, w_uq, kv_latent_norm, w_ukv,
           out_norm_mla, out_norm_sb, w_o, norm_ffn, w_gate, w_up, w_down, norm_final):
    b, s, d = x.shape
    depth = w_in.shape[0]
    d_ff = w_gate.shape[-1]
    blk = SEQ_BLOCK
    nb = s // blk
    assert s % (blk * N_CHAINS) == 0 and N_CHAINS % 2 == 0 and d_ff % FF_CHUNK == 0

    inv_freq = ROPE_THETA ** (-jnp.arange(0, MLA_ROPE_DIM, 2, dtype=F32) / MLA_ROPE_DIM)
    zeros = lambda n: jnp.zeros((n,), F32)
    freq = jnp.concatenate([zeros(MLA_NOPE_DIM), inv_freq, inv_freq,
                            zeros(MLA_PAD_DIM - MLA_QK_DIM)])[None, :]
    tri = jnp.where(jnp.arange(blk)[None, :] > jnp.arange(blk)[:, None], -1.0, 0.0).astype(BF16)
    pos = positions.reshape(b, s, 1)

    params = pltpu.CompilerParams(
        dimension_semantics=("arbitrary", "arbitrary"), vmem_limit_bytes=VMEM_LIMIT_BYTES)

    h = x
    for l in range(depth):
        o0 = Q_LORA_RANK
        o1 = o0 + KV_LORA_RANK
        o2 = o1 + MLA_ROPE_DIM
        w_cq, w_ckv, w_kr, w_sb = w_in[l][:, :o0], w_in[l][:, o0:o1], w_in[l][:, o1:o2], w_in[l][:, o2:]
        kr_slot = lambda w: _head_slots(None, w[:, None, :])
        w_lat = jnp.concatenate([w_cq, w_ckv, kr_slot(w_kr), kr_slot(_rotate_half_cols(w_kr))],
                                axis=1).astype(BF16)
        uq = w_uq[l].reshape(Q_LORA_RANK, MLA_HEADS, MLA_QK_DIM)
        uq_nope, uq_rope = uq[..., :MLA_NOPE_DIM], uq[..., MLA_NOPE_DIM:]
        w_uq_lin = _head_slots(uq_nope, uq_rope).astype(BF16)
        w_uq_rot = _head_slots(None, _rotate_half_cols(uq_rope)).astype(BF16)
        ukv = w_ukv[l].reshape(KV_LORA_RANK, MLA_HEADS, MLA_NOPE_DIM + MLA_V_DIM)
        w_uk = _head_slots(ukv[..., :MLA_NOPE_DIM], None).astype(BF16)
        w_uv = ukv[..., MLA_NOPE_DIM:].reshape(KV_LORA_RANK, MLA_WIDTH).astype(BF16)

        head_major = lambda width: jax.ShapeDtypeStruct((b, MLA_HEADS, s, width), BF16)
        tile = PROJ_BLOCKS * blk
        head_spec = lambda width: pl.BlockSpec((1, MLA_HEADS, tile, width), lambda bi, ti: (bi, 0, ti, 0))
        vt_shape = jax.ShapeDtypeStruct((b, MLA_HEADS, nb, MLA_V_DIM, blk), BF16)
        vt_spec = pl.BlockSpec((1, MLA_HEADS, PROJ_BLOCKS, MLA_V_DIM, blk), lambda bi, ti: (bi, 0, ti, 0, 0))
        q_mla, k_mla, vt_mla, q_sb, k_sb, vt_sb = pl.pallas_call(
            _proj_kernel,
            grid=(b, nb // PROJ_BLOCKS),
            in_specs=[
                pl.BlockSpec((1, tile, d), lambda bi, ti: (bi, ti, 0)),
                pl.BlockSpec((1, tile, 1), lambda bi, ti: (bi, ti, 0)),
                _const_spec((1, MLA_PAD_DIM)),
                _const_spec((1, d)),
                _const_spec((1, Q_LORA_RANK)),
                _const_spec((1, KV_LORA_RANK)),
                _const_spec(w_lat.shape),
                _const_spec((d, 3 * SB_WIDTH)),
                _const_spec(w_uq_lin.shape),
                _const_spec(w_uq_rot.shape),
                _const_spec(w_uk.shape),
                _const_spec(w_uv.shape),
            ],
            out_specs=[head_spec(MLA_PAD_DIM), head_spec(MLA_PAD_DIM), vt_spec,
                       head_spec(SB_HEAD_DIM), head_spec(SB_HEAD_DIM), vt_spec],
            out_shape=[head_major(MLA_PAD_DIM), head_major(MLA_PAD_DIM), vt_shape,
                       head_major(SB_HEAD_DIM), head_major(SB_HEAD_DIM), vt_shape],
            compiler_params=params,
            name="proj",
        )(h, pos, freq, norm_mix[l][None, :], q_latent_norm[l][None, :], kv_latent_norm[l][None, :],
          w_lat, w_sb.astype(BF16), w_uq_lin, w_uq_rot, w_uk, w_uv)

        def attn_specs(width):
            seq_spec = pl.BlockSpec((1, 1, s, width), lambda bi, hi: (bi, hi, 0, 0))
            blocked = pl.BlockSpec((1, 1, nb, MLA_V_DIM, blk), lambda bi, hi: (bi, hi, 0, 0, 0))
            return seq_spec, blocked

        o_shape = jax.ShapeDtypeStruct((b, MLA_HEADS, nb, MLA_V_DIM, blk), F32)
        score_scratch = [pltpu.VMEM((2, N_CHAINS, blk, blk), F32),
                         pltpu.VMEM((N_CHAINS, blk, blk), BF16)]
        seq_spec, blocked = attn_specs(MLA_PAD_DIM)
        o_mla = pl.pallas_call(
            functools.partial(_mla_kernel, n_super=nb // N_CHAINS),
            grid=(b, MLA_HEADS),
            in_specs=[seq_spec, seq_spec, blocked],
            out_specs=blocked,
            out_shape=o_shape,
            scratch_shapes=score_scratch,
            compiler_params=params,
            name="mla_attn",
        )(q_mla, k_mla, vt_mla)
        seq_spec, blocked = attn_specs(SB_HEAD_DIM)
        o_sb = pl.pallas_call(
            functools.partial(_sb_kernel, n_super=nb // N_CHAINS),
            grid=(b, SB_HEADS),
            in_specs=[seq_spec, seq_spec, blocked, _const_spec((blk, blk))],
            out_specs=blocked,
            out_shape=o_shape,
            compiler_params=params,
            name="sb_attn",
        )(q_sb, k_sb, vt_sb, tri)

        assert depth == 1
        o_spec = pl.BlockSpec((1, MLA_HEADS, OUT_BLOCKS, MLA_V_DIM, blk), lambda bi, ti: (bi, 0, ti, 0, 0))
        tok_spec = pl.BlockSpec((1, OUT_BLOCKS * blk, d), lambda bi, ti: (bi, ti, 0))
        resident = lambda shape: pl.BlockSpec(shape, lambda *_: (0,) * len(shape),
                                              pipeline_mode=pl.Buffered(1))
        h = pl.pallas_call(
            functools.partial(_out_kernel, d_ff=d_ff),
            grid=(b, nb // OUT_BLOCKS),
            in_specs=[
                tok_spec, o_spec, o_spec,
                _const_spec((MLA_WIDTH, 1)), _const_spec((SB_WIDTH, 1)),
                resident((MLA_WIDTH, d)), resident((SB_WIDTH, d)),
                _const_spec((1, d)),
                resident((d, d_ff)), resident((d, d_ff)), resident((d_ff, d)),
                _const_spec((1, d)),
            ],
            out_specs=tok_spec,
            out_shape=jax.ShapeDtypeStruct((b, s, d), F32),
            compiler_params=params,
            name="out_ffn",
        )(h, o_mla, o_sb, out_norm_mla[l][:, None], out_norm_sb[l][:, None],
          w_o[l][:MLA_WIDTH].astype(BF16), w_o[l][MLA_WIDTH:].astype(BF16),
          norm_ffn[l][None, :], w_gate[l].astype(BF16), w_up[l].astype(BF16),
          w_down[l].astype(BF16), norm_final[None, :])
    return h
```

```python
import functools
import math

import jax
import jax.numpy as jnp
from jax import lax
from jax.experimental import pallas as pl
from jax.experimental.pallas import tpu as pltpu

EPS = 1e-6
ROPE_THETA = 10000.0
LOG2_E = 1.4426950408889634

MLA_HEADS = 8
MLA_NOPE_DIM = 64
MLA_ROPE_DIM = 32
MLA_V_DIM = 64
MLA_QK_DIM = MLA_NOPE_DIM + MLA_ROPE_DIM
MLA_PAD_DIM = 128
Q_LORA_RANK = 256
KV_LORA_RANK = 128
SB_HEADS = 8
SB_HEAD_DIM = 64
MLA_WIDTH = MLA_HEADS * MLA_V_DIM
SB_WIDTH = SB_HEADS * SB_HEAD_DIM

SEQ_BLOCK = 256
N_CHAINS = 4
MLA_K_PER_STEP = 4
SB_DEAD_BITS = 160.0
FF_CHUNK = 256
OUT_BLOCKS = 2
PROJ_BLOCKS = 2
VMEM_LIMIT_BYTES = 56 * 1024 * 1024

F32 = jnp.float32
BF16 = jnp.bfloat16


def _rms_scale(v, axis):
    return lax.rsqrt(jnp.mean(v * v, axis=axis, keepdims=True) + EPS)


def _dot(a, b):
    return jnp.dot(a, b, preferred_element_type=F32)


def _dot_nt(a, b):
    return lax.dot_general(a, b, (((1,), (1,)), ((), ())), preferred_element_type=F32)


def _dot_tn(a, b):
    return lax.dot_general(a, b, (((0,), (0,)), ((), ())), preferred_element_type=F32)


def _proj_kernel(x_ref, pos_ref, freq_ref, g_mix_ref, g_q_ref, g_kv_ref,
                 w_lat_ref, w_sb_ref, w_uq_ref, w_uq_rot_ref, w_uk_ref, w_uv_ref,
                 q_mla_ref, k_mla_ref, vt_mla_ref, q_sb_ref, k_sb_ref, vt_sb_ref):
    x = x_ref[0]
    u = (x * _rms_scale(x, -1) * g_mix_ref[...]).astype(BF16)

    ang = pos_ref[0].astype(F32) * freq_ref[...]
    cos = jnp.cos(ang)
    sin = jnp.sin(ang)

    lat = _dot(u, w_lat_ref[...])
    c_q = lat[:, :Q_LORA_RANK]
    c_kv = lat[:, Q_LORA_RANK:Q_LORA_RANK + KV_LORA_RANK]
    k_r = lat[:, Q_LORA_RANK + KV_LORA_RANK:Q_LORA_RANK + KV_LORA_RANK + MLA_PAD_DIM]
    k_r_rot = lat[:, Q_LORA_RANK + KV_LORA_RANK + MLA_PAD_DIM:]
    k_rope = k_r * cos + k_r_rot * sin

    ql = (c_q * _rms_scale(c_q, -1) * g_q_ref[...]).astype(BF16)
    kvl = (c_kv * _rms_scale(c_kv, -1) * g_kv_ref[...]).astype(BF16)

    q_lin = _dot(ql, w_uq_ref[...])
    q_rot = _dot(ql, w_uq_rot_ref[...])
    k_nope = _dot(kvl, w_uk_ref[...])
    v_mla = _dot(kvl, w_uv_ref[...])
    q_scale = LOG2_E / math.sqrt(MLA_QK_DIM)
    for h in range(MLA_HEADS):
        sl = slice(h * MLA_PAD_DIM, (h + 1) * MLA_PAD_DIM)
        q_h = (q_lin[:, sl] * cos + q_rot[:, sl] * sin) * q_scale
        q_mla_ref[0, h] = q_h.astype(BF16)
        k_mla_ref[0, h] = (k_nope[:, sl] + k_rope).astype(BF16)
    v_mla_t = v_mla.T
    for h in range(MLA_HEADS):
        for t in range(PROJ_BLOCKS):
            vt_mla_ref[0, h, t] = v_mla_t[h * MLA_V_DIM:(h + 1) * MLA_V_DIM,
                                          t * SEQ_BLOCK:(t + 1) * SEQ_BLOCK].astype(BF16)

    sb = _dot(u, w_sb_ref[...])
    sb_scale = LOG2_E / math.sqrt(SB_HEAD_DIM)
    q_sb = sb[:, :SB_WIDTH] * sb_scale
    k_sb = sb[:, SB_WIDTH:2 * SB_WIDTH]
    v_sb_t = sb[:, 2 * SB_WIDTH:].T
    for h in range(SB_HEADS):
        sl = slice(h * SB_HEAD_DIM, (h + 1) * SB_HEAD_DIM)
        q_sb_ref[0, h] = q_sb[:, sl].astype(BF16)
        k_sb_ref[0, h] = k_sb[:, sl].astype(BF16)
        for t in range(PROJ_BLOCKS):
            vt_sb_ref[0, h, t] = v_sb_t[sl, t * SEQ_BLOCK:(t + 1) * SEQ_BLOCK].astype(BF16)


def _attention_driver(q_ref, k_ref, vt_ref, o_ref, z_scr, w_scr, front_fn, apply_fn, finish_fn,
                      diag_mask, n_super, k_per_step):
    blk, n = SEQ_BLOCK, N_CHAINS
    unit = jnp.ones((1, blk), F32)

    def rows(ref, j):
        return ref[0, 0, pl.ds(pl.multiple_of(j * blk, blk), blk), :]

    def super_body(sb, carry):
        base = sb * n

        def scores(j, chains):
            k = rows(k_ref, j)
            return [_dot_nt(k, rows(q_ref, base + a)) for a in chains]

        def park_scores(slot, j):
            for a, z in enumerate(scores(j, range(n))):
                z_scr[slot, a] = z

        def flush(j, accs, factors):
            vt = vt_ref[0, 0, j]
            return [apply_fn(acc, _dot(vt, w_scr[a]), f) for a, (acc, f) in enumerate(zip(accs, factors))]

        smalls, accs = [None] * n, [None] * n
        zs = scores(base + n - 1, [n - 1])
        pending = None
        for kb in reversed(range(n)):
            chains = list(range(kb, n))
            if kb > 0:
                zs_next = scores(base + kb - 1, range(kb - 1, n))
            else:
                park_scores(0, jnp.maximum(base - 1, 0))
            if pending is not None:
                vt = vt_ref[0, 0, base + kb + 1]
                for a, w, f in zip(*pending):
                    accs[a] = apply_fn(accs[a], _dot(vt, w), f)
            new_smalls, ws, factors = front_fn(zs, [smalls[a] for a in chains],
                                               [diag_mask if a == kb else None for a in chains], lambda: None)
            for a, sm in zip(chains, new_smalls):
                smalls[a] = sm
            if kb > 0:
                pending = (chains, ws, factors)
                zs = zs_next
            else:
                for a in range(n):
                    w_scr[a] = ws[a]
                accs[0] = jnp.zeros((vt_ref.shape[3], blk), F32)
                factors[0] = unit

        def k_body(it, state):
            smalls, accs, factors = (list(t) for t in state)
            j = base - 1 - k_per_step * it
            for u in range(k_per_step):
                slot, j_cur, j_next = u % 2, j - u, jnp.maximum(j - u - 1, 0)
                k_next, vt_prev = rows(k_ref, j_next), vt_ref[0, 0, j_cur + 1]
                for a in range(n):
                    z_scr[1 - slot, a] = _dot_nt(k_next, rows(q_ref, base + a))
                    accs[a] = apply_fn(accs[a], _dot(vt_prev, w_scr[a]), factors[a])
                smalls, ws, factors = front_fn([z_scr[slot, a] for a in range(n)], smalls, [None] * n,
                                               lambda: None)
                for a in range(n):
                    w_scr[a] = ws[a]
            return tuple(smalls), tuple(accs), tuple(factors)

        smalls, accs, factors = lax.fori_loop(
            0, sb * (n // k_per_step), k_body, (tuple(smalls), tuple(accs), tuple(factors)))
        accs = flush(0, accs, factors)
        for a in range(n):
            o_ref[0, 0, base + a] = finish_fn(smalls[a], accs[a])
        return carry

    lax.fori_loop(0, n_super, super_body, 0)


def _block_iotas():
    key_idx = lax.broadcasted_iota(jnp.int32, (SEQ_BLOCK, SEQ_BLOCK), 0)
    qry_idx = lax.broadcasted_iota(jnp.int32, (SEQ_BLOCK, SEQ_BLOCK), 1)
    return key_idx, qry_idx


def _mla_kernel(q_ref, k_ref, vt_ref, o_ref, z_scr, w_scr, *, n_super):
    key_idx, qry_idx = _block_iotas()

    def front(zs, smalls, masks, prefetch):
        prefetch()
        new_smalls, ws, alphas = [], [], []
        for s, small, mask in zip(zs, smalls, masks):
            if mask is not None:
                s = jnp.where(mask, s, -jnp.inf)
            s_max = jnp.max(s, axis=0, keepdims=True)
            if small is None:
                m_new, alpha = s_max, None
            else:
                m_new = jnp.maximum(small[0], s_max)
                alpha = jnp.exp2(small[0] - m_new)
            p = jnp.exp2(s - m_new)
            p_sum = jnp.sum(p, axis=0, keepdims=True)
            new_smalls.append((m_new, p_sum if small is None else alpha * small[1] + p_sum))
            ws.append(p.astype(BF16))
            alphas.append(alpha)
        return new_smalls, ws, alphas

    def apply(acc, pv, alpha):
        return pv if acc is None else alpha * acc + pv

    _attention_driver(q_ref, k_ref, vt_ref, o_ref, z_scr, w_scr, front, apply,
                      lambda small, acc: acc / small[1], key_idx <= qry_idx, n_super, MLA_K_PER_STEP)


def _softplus2(z):
    return jnp.maximum(z, 0.0) + jnp.log2(1.0 + jnp.exp2(-jnp.abs(z)))


def _sb_kernel(q_ref, k_ref, vt_ref, tri_ref, o_ref, *, n_super):
    blk, n = SEQ_BLOCK, N_CHAINS
    key_idx, qry_idx = _block_iotas()
    strict = key_idx < qry_idx

    def rows(ref, j):
        return ref[0, 0, pl.ds(pl.multiple_of(j * blk, blk), blk), :]

    def logits(base, specs):
        return [_dot_nt(rows(k_ref, j), rows(q_ref, base + a)) for a, j, _ in specs]

    def suffix_sums(zs, specs):
        log_betas, sps = [], []
        for z, (_, _, diag) in zip(zs, specs):
            sp = _softplus2(z)
            log_betas.append(z - sp)
            if diag:
                sp = jnp.where(strict, sp, 0.0)
            sps.append(sp.astype(BF16))
        return log_betas, sps, [_dot(tri_ref[...], sp) for sp in sps]

    def products(log_betas, sps, sufs, specs):
        ws = []
        for log_beta, suf, (_, _, diag) in zip(log_betas, sufs, specs):
            w = jnp.exp2(log_beta + suf)
            if diag:
                w = jnp.where(strict, w, 0.0)
            ws.append(w.astype(BF16))
        return [(suf[0:1, :] - sp[0:1, :].astype(F32), _dot(vt_ref[0, 0, j], w))
                for w, sp, suf, (_, j, _) in zip(ws, sps, sufs, specs)]

    def tiles(base, groups):
        zs = [logits(base, g) for g in groups]
        out, prev = [], None
        for g, z in zip(groups, zs):
            cur = suffix_sums(z, g) + (g,)
            if prev is not None:
                out += products(*prev)
            prev = cur
        return out + products(*prev)

    def band(base, d):
        return [(a, jnp.maximum(base + a - d, 0), False) for a in range(n)]

    def factor(base, a, d, c):
        return jnp.where(base + a - d >= 0, jnp.exp2(c), 0.0)

    def live(cs):
        return (jnp.max(functools.reduce(jnp.maximum, cs)) > -SB_DEAD_BITS).astype(jnp.int32)

    def super_body(sb, carry):
        base = sb * n
        res = tiles(base, [[(a, base + a, True) for a in range(n)], band(base, 1)])
        cs, accs = [], []
        for a in range(n):
            (sum0, pv0), (sum1, pv1) = res[a], res[n + a]
            accs.append(pv0 + pv1 * factor(base, a, 1, sum0))
            cs.append(sum0 + sum1)

        def w_body(state):
            d, _, cs, accs = state
            res = tiles(base, [band(base, d)])
            accs = tuple(acc + pv * factor(base, a, d, c)
                         for a, (acc, c, (_, pv)) in enumerate(zip(accs, cs, res)))
            cs = tuple(c + block_sum for c, (block_sum, _) in zip(cs, res))
            return d + 1, live(cs), cs, accs

        _, _, _, accs = lax.while_loop(
            lambda state: jnp.logical_and(state[0] < base + n, state[1] > 0), w_body,
            (jnp.int32(2), live(cs), tuple(cs), tuple(accs)))
        for a in range(n):
            o_ref[0, 0, base + a] = accs[a]
        return carry

    lax.fori_loop(0, n_super, super_body, 0)


def _out_kernel(x_ref, o_mla_ref, o_sb_ref, g_mla_ref, g_sb_ref, w_o_mla_ref, w_o_sb_ref,
                g_ffn_ref, w_gate_ref, w_up_ref, w_down_ref, g_final_ref, out_ref, *, d_ff):
    blk = SEQ_BLOCK

    def group(o_ref, g_ref, w_ref):
        parts = []
        for t in range(OUT_BLOCKS):
            o = o_ref[0, :, t].reshape(-1, blk)
            y = (o * _rms_scale(o, 0) * g_ref[...]).astype(BF16)
            parts.append(_dot_tn(y, w_ref[...]))
        return jnp.concatenate(parts, axis=0)

    h = x_ref[0] + group(o_mla_ref, g_mla_ref, w_o_mla_ref) + group(o_sb_ref, g_sb_ref, w_o_sb_ref)
    f = (h * _rms_scale(h, -1) * g_ffn_ref[...]).astype(BF16)
    ffn = jnp.zeros_like(h)
    for c in range(0, d_ff, FF_CHUNK):
        gate = _dot(f, w_gate_ref[:, c:c + FF_CHUNK])
        up = _dot(f, w_up_ref[:, c:c + FF_CHUNK])
        act = (gate * jax.nn.sigmoid(gate) * up).astype(BF16)
        ffn = ffn + _dot(act, w_down_ref[c:c + FF_CHUNK, :])
    h = h + ffn
    out_ref[0] = h * _rms_scale(h, -1) * g_final_ref[...]


def _rotate_half_cols(w):
    half = w.shape[-1] // 2
    return jnp.concatenate([-w[..., half:], w[..., :half]], axis=-1)


def _head_slots(nope, rope):
    ref = nope if nope is not None else rope
    r, h = ref.shape[0], ref.shape[1]
    nope = jnp.zeros((r, h, MLA_NOPE_DIM), ref.dtype) if nope is None else nope
    rope = jnp.zeros((r, h, MLA_ROPE_DIM), ref.dtype) if rope is None else rope
    pad = jnp.zeros((r, h, MLA_PAD_DIM - MLA_QK_DIM), ref.dtype)
    return jnp.concatenate([nope, rope, pad], axis=-1).reshape(r, h * MLA_PAD_DIM)


def _const_spec(shape):
    return pl.BlockSpec(shape, lambda *_: (0,) * len(shape))


def kernel(x, positions, norm_mix, w_in, q_latent_norm, w_uq, kv_latent_norm, w_ukv,
           out_norm_mla, out_norm_sb, w_o, norm_ffn, w_gate, w_up, w_down, norm_final):
    b, s, d = x.shape
    depth = w_in.shape[0]
    d_ff = w_gate.shape[-1]
    blk = SEQ_BLOCK
    nb = s // blk
    assert s % (blk * N_CHAINS) == 0 and N_CHAINS % 2 == 0 and d_ff % FF_CHUNK == 0

    inv_freq = ROPE_THETA ** (-jnp.arange(0, MLA_ROPE_DIM, 2, dtype=F32) / MLA_ROPE_DIM)
    zeros = lambda n: jnp.zeros((n,), F32)
    freq = jnp.concatenate([zeros(MLA_NOPE_DIM), inv_freq, inv_freq,
                            zeros(MLA_PAD_DIM - MLA_QK_DIM)])[None, :]
    tri = jnp.where(jnp.arange(blk)[None, :] > jnp.arange(blk)[:, None], -1.0, 0.0).astype(BF16)
    pos = positions.reshape(b, s, 1)

    params = pltpu.CompilerParams(
        dimension_semantics=("arbitrary", "arbitrary"), vmem_limit_bytes=VMEM_LIMIT_BYTES)

    h = x
    for l in range(depth):
        o0 = Q_LORA_RANK
        o1 = o0 + KV_LORA_RANK
        o2 = o1 + MLA_ROPE_DIM
        w_cq, w_ckv, w_kr, w_sb = w_in[l][:, :o0], w_in[l][:, o0:o1], w_in[l][:, o1:o2], w_in[l][:, o2:]
        kr_slot = lambda w: _head_slots(None, w[:, None, :])
        w_lat = jnp.concatenate([w_cq, w_ckv, kr_slot(w_kr), kr_slot(_rotate_half_cols(w_kr))],
                                axis=1).astype(BF16)
        uq = w_uq[l].reshape(Q_LORA_RANK, MLA_HEADS, MLA_QK_DIM)
        uq_nope, uq_rope = uq[..., :MLA_NOPE_DIM], uq[..., MLA_NOPE_DIM:]
        w_uq_lin = _head_slots(uq_nope, uq_rope).astype(BF16)
        w_uq_rot = _head_slots(None, _rotate_half_cols(uq_rope)).astype(BF16)
        ukv = w_ukv[l].reshape(KV_LORA_RANK, MLA_HEADS, MLA_NOPE_DIM + MLA_V_DIM)
        w_uk = _head_slots(ukv[..., :MLA_NOPE_DIM], None).astype(BF16)
        w_uv = ukv[..., MLA_NOPE_DIM:].reshape(KV_LORA_RANK, MLA_WIDTH).astype(BF16)

        head_major = lambda width: jax.ShapeDtypeStruct((b, MLA_HEADS, s, width), BF16)
        tile = PROJ_BLOCKS * blk
        head_spec = lambda width: pl.BlockSpec((1, MLA_HEADS, tile, width), lambda bi, ti: (bi, 0, ti, 0))
        vt_shape = jax.ShapeDtypeStruct((b, MLA_HEADS, nb, MLA_V_DIM, blk), BF16)
        vt_spec = pl.BlockSpec((1, MLA_HEADS, PROJ_BLOCKS, MLA_V_DIM, blk), lambda bi, ti: (bi, 0, ti, 0, 0))
        q_mla, k_mla, vt_mla, q_sb, k_sb, vt_sb = pl.pallas_call(
            _proj_kernel,
            grid=(b, nb // PROJ_BLOCKS),
            in_specs=[
                pl.BlockSpec((1, tile, d), lambda bi, ti: (bi, ti, 0)),
                pl.BlockSpec((1, tile, 1), lambda bi, ti: (bi, ti, 0)),
                _const_spec((1, MLA_PAD_DIM)),
                _const_spec((1, d)),
                _const_spec((1, Q_LORA_RANK)),
                _const_spec((1, KV_LORA_RANK)),
                _const_spec(w_lat.shape),
                _const_spec((d, 3 * SB_WIDTH)),
                _const_spec(w_uq_lin.shape),
                _const_spec(w_uq_rot.shape),
                _const_spec(w_uk.shape),
                _const_spec(w_uv.shape),
            ],
            out_specs=[head_spec(MLA_PAD_DIM), head_spec(MLA_PAD_DIM), vt_spec,
                       head_spec(SB_HEAD_DIM), head_spec(SB_HEAD_DIM), vt_spec],
            out_shape=[head_major(MLA_PAD_DIM), head_major(MLA_PAD_DIM), vt_shape,
                       head_major(SB_HEAD_DIM), head_major(SB_HEAD_DIM), vt_shape],
            compiler_params=params,
            name="proj",
        )(h, pos, freq, norm_mix[l][None, :], q_latent_norm[l][None, :], kv_latent_norm[l][None, :],
          w_lat, w_sb.astype(BF16), w_uq_lin, w_uq_rot, w_uk, w_uv)

        def attn_specs(width):
            seq_spec = pl.BlockSpec((1, 1, s, width), lambda bi, hi: (bi, hi, 0, 0))
            blocked = pl.BlockSpec((1, 1, nb, MLA_V_DIM, blk), lambda bi, hi: (bi, hi, 0, 0, 0))
            return seq_spec, blocked

        o_shape = jax.ShapeDtypeStruct((b, MLA_HEADS, nb, MLA_V_DIM, blk), F32)
        score_scratch = [pltpu.VMEM((2, N_CHAINS, blk, blk), F32),
                         pltpu.VMEM((N_CHAINS, blk, blk), BF16)]
        seq_spec, blocked = attn_specs(MLA_PAD_DIM)
        o_mla = pl.pallas_call(
            functools.partial(_mla_kernel, n_super=nb // N_CHAINS),
            grid=(b, MLA_HEADS),
            in_specs=[seq_spec, seq_spec, blocked],
            out_specs=blocked,
            out_shape=o_shape,
            scratch_shapes=score_scratch,
            compiler_params=params,
            name="mla_attn",
        )(q_mla, k_mla, vt_mla)
        seq_spec, blocked = attn_specs(SB_HEAD_DIM)
        o_sb = pl.pallas_call(
            functools.partial(_sb_kernel, n_super=nb // N_CHAINS),
            grid=(b, SB_HEADS),
            in_specs=[seq_spec, seq_spec, blocked, _const_spec((blk, blk))],
            out_specs=blocked,
            out_shape=o_shape,
            compiler_params=params,
            name="sb_attn",
        )(q_sb, k_sb, vt_sb, tri)

        assert depth == 1
        o_spec = pl.BlockSpec((1, MLA_HEADS, OUT_BLOCKS, MLA_V_DIM, blk), lambda bi, ti: (bi, 0, ti, 0, 0))
        tok_spec = pl.BlockSpec((1, OUT_BLOCKS * blk, d), lambda bi, ti: (bi, ti, 0))
        resident = lambda shape: pl.BlockSpec(shape, lambda *_: (0,) * len(shape),
                                              pipeline_mode=pl.Buffered(1))
        h = pl.pallas_call(
            functools.partial(_out_kernel, d_ff=d_ff),
            grid=(b, nb // OUT_BLOCKS),
            in_specs=[
                tok_spec, o_spec, o_spec,
                _const_spec((MLA_WIDTH, 1)), _const_spec((SB_WIDTH, 1)),
                resident((MLA_WIDTH, d)), resident((SB_WIDTH, d)),
                _const_spec((1, d)),
                resident((d, d_ff)), resident((d, d_ff)), resident((d_ff, d)),
                _const_spec((1, d)),
            ],
            out_specs=tok_spec,
            out_shape=jax.ShapeDtypeStruct((b, s, d), F32),
            compiler_params=params,
            name="out_ffn",
        )(h, o_mla, o_sb, out_norm_mla[l][:, None], out_norm_sb[l][:, None],
          w_o[l][:MLA_WIDTH].astype(BF16), w_o[l][MLA_WIDTH:].astype(BF16),
          norm_ffn[l][None, :], w_gate[l].astype(BF16), w_up[l].astype(BF16),
          w_down[l].astype(BF16), norm_final[None, :])
    return h
```

```python
import functools
import math

import jax
import jax.numpy as jnp
from jax import lax
from jax.experimental import pallas as pl
from jax.experimental.pallas import tpu as pltpu

EPS = 1e-6
ROPE_THETA = 10000.0
LOG2_E = 1.4426950408889634

MLA_HEADS = 8
MLA_NOPE_DIM = 64
MLA_ROPE_DIM = 32
MLA_V_DIM = 64
MLA_QK_DIM = MLA_NOPE_DIM + MLA_ROPE_DIM
MLA_PAD_DIM = 128
MLA_VT_ROWS = MLA_V_DIM + 16
Q_LORA_RANK = 256
KV_LORA_RANK = 128
SB_HEADS = 8
SB_HEAD_DIM = 64
MLA_WIDTH = MLA_HEADS * MLA_V_DIM
SB_WIDTH = SB_HEADS * SB_HEAD_DIM

SEQ_BLOCK = 256
N_CHAINS = 4
MLA_K_PER_STEP = 4
SB_DEAD_BITS = 160.0
FF_CHUNK = 256
OUT_BLOCKS = 2
PROJ_BLOCKS = 4
VMEM_LIMIT_BYTES = 56 * 1024 * 1024

F32 = jnp.float32
BF16 = jnp.bfloat16


def _rms_scale(v, axis):
    return lax.rsqrt(jnp.mean(v * v, axis=axis, keepdims=True) + EPS)


def _dot(a, b):
    return jnp.dot(a, b, preferred_element_type=F32)


def _dot_nt(a, b):
    return lax.dot_general(a, b, (((1,), (1,)), ((), ())), preferred_element_type=F32)


def _dot_tn(a, b):
    return lax.dot_general(a, b, (((0,), (0,)), ((), ())), preferred_element_type=F32)


def _proj_kernel(x_ref, pos_ref, freq_ref, g_mix_ref, g_q_ref, g_kv_ref,
                 w_lat_ref, w_sb_ref, w_uq_ref, w_uq_rot_ref, w_uk_ref, w_uv_ref,
                 q_mla_ref, k_mla_ref, vt_mla_ref, q_sb_ref, k_sb_ref, vt_sb_ref):
    x = x_ref[0]
    u = (x * _rms_scale(x, -1) * g_mix_ref[...]).astype(BF16)

    ang = pos_ref[0].astype(F32) * freq_ref[...]
    cos = jnp.cos(ang)
    sin = jnp.sin(ang)

    lat = _dot(u, w_lat_ref[...])
    sb = _dot(u, w_sb_ref[...])
    c_q = lat[:, :Q_LORA_RANK]
    c_kv = lat[:, Q_LORA_RANK:Q_LORA_RANK + KV_LORA_RANK]
    k_r = lat[:, Q_LORA_RANK + KV_LORA_RANK:Q_LORA_RANK + KV_LORA_RANK + MLA_PAD_DIM]
    k_r_rot = lat[:, Q_LORA_RANK + KV_LORA_RANK + MLA_PAD_DIM:]
    k_rope = k_r * cos + k_r_rot * sin

    ql = (c_q * _rms_scale(c_q, -1) * g_q_ref[...]).astype(BF16)
    kvl = (c_kv * _rms_scale(c_kv, -1) * g_kv_ref[...]).astype(BF16)

    q_lin = _dot(ql, w_uq_ref[...])
    q_rot = _dot(ql, w_uq_rot_ref[...])
    k_nope = _dot(kvl, w_uk_ref[...])
    v_mla = _dot(kvl, w_uv_ref[...])
    q_scale = LOG2_E / math.sqrt(MLA_QK_DIM)
    for h in range(MLA_HEADS):
        sl = slice(h * MLA_PAD_DIM, (h + 1) * MLA_PAD_DIM)
        q_h = (q_lin[:, sl] * cos + q_rot[:, sl] * sin) * q_scale
        q_mla_ref[0, h] = q_h.astype(BF16)
        k_mla_ref[0, h] = (k_nope[:, sl] + k_rope).astype(BF16)
    v_mla_t = v_mla.T
    pad_row = lax.broadcasted_iota(jnp.int32, (MLA_VT_ROWS - MLA_V_DIM, SEQ_BLOCK), 0)
    ones_then_zeros = jnp.where(pad_row == 0, 1.0, 0.0).astype(BF16)
    for h in range(MLA_HEADS):
        for t in range(PROJ_BLOCKS):
            vt_mla_ref[0, h, t, :MLA_V_DIM, :] = v_mla_t[h * MLA_V_DIM:(h + 1) * MLA_V_DIM,
                                                         t * SEQ_BLOCK:(t + 1) * SEQ_BLOCK].astype(BF16)
            vt_mla_ref[0, h, t, MLA_V_DIM:, :] = ones_then_zeros

    sb_scale = LOG2_E / math.sqrt(SB_HEAD_DIM)
    q_sb = sb[:, :SB_WIDTH] * sb_scale
    k_sb = sb[:, SB_WIDTH:2 * SB_WIDTH]
    v_sb_t = sb[:, 2 * SB_WIDTH:].T
    for h in range(SB_HEADS):
        sl = slice(h * SB_HEAD_DIM, (h + 1) * SB_HEAD_DIM)
        q_sb_ref[0, h] = q_sb[:, sl].astype(BF16)
        k_sb_ref[0, h] = k_sb[:, sl].astype(BF16)
        for t in range(PROJ_BLOCKS):
            vt_sb_ref[0, h, t] = v_sb_t[sl, t * SEQ_BLOCK:(t + 1) * SEQ_BLOCK].astype(BF16)


def _attention_driver(q_ref, k_ref, vt_ref, o_ref, z_scr, w_scr, front_fn, apply_fn, finish_fn,
                      diag_mask, n_super, k_per_step):
    blk, n = SEQ_BLOCK, N_CHAINS
    unit = jnp.ones((1, blk), F32)

    def rows(ref, j):
        return ref[0, 0, pl.ds(pl.multiple_of(j * blk, blk), blk), :]

    def super_body(sb, carry):
        base = sb * n

        def scores(j, chains):
            k = rows(k_ref, j)
            return [_dot_nt(k, rows(q_ref, base + a)) for a in chains]

        def park_scores(slot, j):
            for a, z in enumerate(scores(j, range(n))):
                z_scr[slot, a] = z

        def flush(j, accs, factors):
            vt = vt_ref[0, 0, j]
            return [apply_fn(acc, _dot(vt, w_scr[a]), f) for a, (acc, f) in enumerate(zip(accs, factors))]

        smalls, accs = [None] * n, [None] * n
        zs = scores(base + n - 1, [n - 1])
        pending = None
        for kb in reversed(range(n)):
            chains = list(range(kb, n))
            if kb > 0:
                zs_next = scores(base + kb - 1, range(kb - 1, n))
            else:
                park_scores(0, jnp.maximum(base - 1, 0))
            if pending is not None:
                vt = vt_ref[0, 0, base + kb + 1]
                for a, w, f in zip(*pending):
                    accs[a] = apply_fn(accs[a], _dot(vt, w), f)
            new_smalls, ws, factors = front_fn(zs, [smalls[a] for a in chains],
                                               [diag_mask if a == kb else None for a in chains], lambda: None)
            for a, sm in zip(chains, new_smalls):
                smalls[a] = sm
            if kb > 0:
                pending = (chains, ws, factors)
                zs = zs_next
            else:
                for a in range(n):
                    w_scr[a] = ws[a]
                accs[0] = jnp.zeros((vt_ref.shape[3], blk), F32)
                factors[0] = unit

        def k_body(it, state):
            smalls, accs, factors = (list(t) for t in state)
            j = base - 1 - k_per_step * it
            for u in range(k_per_step):
                slot, j_cur, j_next = u % 2, j - u, jnp.maximum(j - u - 1, 0)
                k_next, vt_prev = rows(k_ref, j_next), vt_ref[0, 0, j_cur + 1]
                for a in range(n):
                    z_scr[1 - slot, a] = _dot_nt(k_next, rows(q_ref, base + a))
                    accs[a] = apply_fn(accs[a], _dot(vt_prev, w_scr[a]), factors[a])
                smalls, ws, factors = front_fn([z_scr[slot, a] for a in range(n)], smalls, [None] * n,
                                               lambda: None)
                for a in range(n):
                    w_scr[a] = ws[a]
            return tuple(smalls), tuple(accs), tuple(factors)

        smalls, accs, factors = lax.fori_loop(
            0, sb * (n // k_per_step), k_body, (tuple(smalls), tuple(accs), tuple(factors)))
        accs = flush(0, accs, factors)
        for a in range(n):
            o_ref[0, 0, base + a] = finish_fn(smalls[a], accs[a])
        return carry

    lax.fori_loop(0, n_super, super_body, 0)


def _block_iotas():
    key_idx = lax.broadcasted_iota(jnp.int32, (SEQ_BLOCK, SEQ_BLOCK), 0)
    qry_idx = lax.broadcasted_iota(jnp.int32, (SEQ_BLOCK, SEQ_BLOCK), 1)
    return key_idx, qry_idx


def _mla_kernel(q_ref, k_ref, vt_ref, o_ref, z_scr, w_scr, *, n_super):
    key_idx, qry_idx = _block_iotas()

    def front(zs, smalls, masks, prefetch):
        prefetch()
        new_smalls, ws, alphas = [], [], []
        for s, small, mask in zip(zs, smalls, masks):
            if mask is not None:
                s = jnp.where(mask, s, -jnp.inf)
            s_max = jnp.max(s, axis=0, keepdims=True)
            if small is None:
                m_new, alpha = s_max, None
            else:
                m_new = jnp.maximum(small[0], s_max)
                alpha = jnp.exp2(small[0] - m_new)
            new_smalls.append((m_new,))
            ws.append(jnp.exp2(s - m_new).astype(BF16))
            alphas.append(alpha)
        return new_smalls, ws, alphas

    def apply(acc, pv, alpha):
        return pv if acc is None else alpha * acc + pv

    def finish(small, acc):
        return acc[:MLA_V_DIM] / acc[MLA_V_DIM:MLA_V_DIM + 1]

    _attention_driver(q_ref, k_ref, vt_ref, o_ref, z_scr, w_scr, front, apply, finish,
                      key_idx <= qry_idx, n_super, MLA_K_PER_STEP)


def _softplus2(z):
    return jnp.maximum(z, 0.0) + jnp.log2(1.0 + jnp.exp2(-jnp.abs(z)))


def _sb_kernel(q_ref, k_ref, vt_ref, tri_ref, o_ref, *, n_super):
    blk, n = SEQ_BLOCK, N_CHAINS
    key_idx, qry_idx = _block_iotas()
    strict = key_idx < qry_idx

    def rows(ref, j):
        return ref[0, 0, pl.ds(pl.multiple_of(j * blk, blk), blk), :]

    def logits(base, specs):
        return [_dot_nt(rows(k_ref, j), rows(q_ref, base + a)) for a, j, _ in specs]

    def suffix_sums(zs, specs):
        log_betas, sps = [], []
        for z, (_, _, diag) in zip(zs, specs):
            sp = _softplus2(z)
            log_betas.append(z - sp)
            if diag:
                sp = jnp.where(strict, sp, 0.0)
            sps.append(sp.astype(BF16))
        return log_betas, sps, [_dot(tri_ref[...], sp) for sp in sps]

    def products(log_betas, sps, sufs, specs):
        ws = []
        for log_beta, suf, (_, _, diag) in zip(log_betas, sufs, specs):
            w = jnp.exp2(log_beta + suf)
            if diag:
                w = jnp.where(strict, w, 0.0)
            ws.append(w.astype(BF16))
        return [(suf[0:1, :] - sp[0:1, :].astype(F32), _dot(vt_ref[0, 0, j], w))
                for w, sp, suf, (_, j, _) in zip(ws, sps, sufs, specs)]

    def tiles(base, groups):
        zs = [logits(base, g) for g in groups]
        out, prev = [], None
        for g, z in zip(groups, zs):
            cur = suffix_sums(z, g) + (g,)
            if prev is not None:
                out += products(*prev)
            prev = cur
        return out + products(*prev)

    def band(base, d):
        return [(a, jnp.maximum(base + a - d, 0), False) for a in range(n)]

    def factor(base, a, d, c):
        return jnp.where(base + a - d >= 0, jnp.exp2(c), 0.0)

    def live(cs):
        return (jnp.max(functools.reduce(jnp.maximum, cs)) > -SB_DEAD_BITS).astype(jnp.int32)

    def super_body(sb, carry):
        base = sb * n
        res = tiles(base, [[(a, base + a, True) for a in range(n)], band(base, 1)])
        cs, accs = [], []
        for a in range(n):
            (sum0, pv0), (sum1, pv1) = res[a], res[n + a]
            accs.append(pv0 + pv1 * factor(base, a, 1, sum0))
            cs.append(sum0 + sum1)

        def w_body(state):
            d, _, cs, accs = state
            res = tiles(base, [band(base, d)])
            accs = tuple(acc + pv * factor(base, a, d, c)
                         for a, (acc, c, (_, pv)) in enumerate(zip(accs, cs, res)))
            cs = tuple(c + block_sum for c, (block_sum, _) in zip(cs, res))
            return d + 1, live(cs), cs, accs

        _, _, _, accs = lax.while_loop(
            lambda state: jnp.logical_and(state[0] < base + n, state[1] > 0), w_body,
            (jnp.int32(2), live(cs), tuple(cs), tuple(accs)))
        for a in range(n):
            o_ref[0, 0, base + a] = accs[a]
        return carry

    lax.fori_loop(0, n_super, super_body, 0)


def _out_kernel(x_ref, o_mla_ref, o_sb_ref, g_mla_ref, g_sb_ref, w_o_mla_ref, w_o_sb_ref,
                g_ffn_ref, w_gate_ref, w_up_ref, w_down_ref, g_final_ref, out_ref, *, d_ff):
    blk = SEQ_BLOCK

    def group(o_ref, g_ref, w_ref):
        parts = []
        for t in range(OUT_BLOCKS):
            o = o_ref[0, :, t].reshape(-1, blk)
            y = (o * _rms_scale(o, 0) * g_ref[...]).astype(BF16)
            parts.append(_dot_tn(y, w_ref[...]))
        return jnp.concatenate(parts, axis=0)

    h = x_ref[0] + group(o_mla_ref, g_mla_ref, w_o_mla_ref) + group(o_sb_ref, g_sb_ref, w_o_sb_ref)
    f = (h * _rms_scale(h, -1) * g_ffn_ref[...]).astype(BF16)
    ffn = jnp.zeros_like(h)
    for c in range(0, d_ff, FF_CHUNK):
        gate = _dot(f, w_gate_ref[:, c:c + FF_CHUNK])
        up = _dot(f, w_up_ref[:, c:c + FF_CHUNK])
        act = (gate * jax.nn.sigmoid(gate) * up).astype(BF16)
        ffn = ffn + _dot(act, w_down_ref[c:c + FF_CHUNK, :])
    h = h + ffn
    out_ref[0] = h * _rms_scale(h, -1) * g_final_ref[...]


def _rotate_half_cols(w):
    half = w.shape[-1] // 2
    return jnp.concatenate([-w[..., half:], w[..., :half]], axis=-1)


def _head_slots(nope, rope):
    ref = nope if nope is not None else rope
    r, h = ref.shape[0], ref.shape[1]
    nope = jnp.zeros((r, h, MLA_NOPE_DIM), ref.dtype) if nope is None else nope
    rope = jnp.zeros((r, h, MLA_ROPE_DIM), ref.dtype) if rope is None else rope
    pad = jnp.zeros((r, h, MLA_PAD_DIM - MLA_QK_DIM), ref.dtype)
    return jnp.concatenate([nope, rope, pad], axis=-1).reshape(r, h * MLA_PAD_DIM)


def _const_spec(shape):
    return pl.BlockSpec(shape, lambda *_: (0,) * len(shape))


def kernel(x, positions, norm_mix, w_in, q_latent_norm, w_uq, kv_latent_norm, w_ukv,
           out_norm_mla, out_norm_sb, w_o, norm_ffn, w_gate, w_up, w_down, norm_final):
    b, s, d = x.shape
    depth = w_in.shape[0]
    d_ff = w_gate.shape[-1]
    blk = SEQ_BLOCK
    nb = s // blk
    assert s % (blk * N_CHAINS) == 0 and N_CHAINS % 2 == 0 and d_ff % FF_CHUNK == 0

    inv_freq = ROPE_THETA ** (-jnp.arange(0, MLA_ROPE_DIM, 2, dtype=F32) / MLA_ROPE_DIM)
    zeros = lambda n: jnp.zeros((n,), F32)
    freq = jnp.concatenate([zeros(MLA_NOPE_DIM), inv_freq, inv_freq,
                            zeros(MLA_PAD_DIM - MLA_QK_DIM)])[None, :]
    tri = jnp.where(jnp.arange(blk)[None, :] > jnp.arange(blk)[:, None], -1.0, 0.0).astype(BF16)
    pos = positions.reshape(b, s, 1)

    params = pltpu.CompilerParams(
        dimension_semantics=("arbitrary", "arbitrary"), vmem_limit_bytes=VMEM_LIMIT_BYTES)

    h = x
    for l in range(depth):
        o0 = Q_LORA_RANK
        o1 = o0 + KV_LORA_RANK
        o2 = o1 + MLA_ROPE_DIM
        w_cq, w_ckv, w_kr, w_sb = w_in[l][:, :o0], w_in[l][:, o0:o1], w_in[l][:, o1:o2], w_in[l][:, o2:]
        kr_slot = lambda w: _head_slots(None, w[:, None, :])
        w_lat = jnp.concatenate([w_cq, w_ckv, kr_slot(w_kr), kr_slot(_rotate_half_cols(w_kr))],
                                axis=1).astype(BF16)
        uq = w_uq[l].reshape(Q_LORA_RANK, MLA_HEADS, MLA_QK_DIM)
        uq_nope, uq_rope = uq[..., :MLA_NOPE_DIM], uq[..., MLA_NOPE_DIM:]
        w_uq_lin = _head_slots(uq_nope, uq_rope).astype(BF16)
        w_uq_rot = _head_slots(None, _rotate_half_cols(uq_rope)).astype(BF16)
        ukv = w_ukv[l].reshape(KV_LORA_RANK, MLA_HEADS, MLA_NOPE_DIM + MLA_V_DIM)
        w_uk = _head_slots(ukv[..., :MLA_NOPE_DIM], None).astype(BF16)
        w_uv = ukv[..., MLA_NOPE_DIM:].reshape(KV_LORA_RANK, MLA_WIDTH).astype(BF16)

        head_major = lambda width: jax.ShapeDtypeStruct((b, MLA_HEADS, s, width), BF16)
        tile = PROJ_BLOCKS * blk
        head_spec = lambda width: pl.BlockSpec((1, MLA_HEADS, tile, width), lambda bi, ti: (bi, 0, ti, 0))
        vt_shape = lambda rows: jax.ShapeDtypeStruct((b, MLA_HEADS, nb, rows, blk), BF16)
        vt_spec = lambda rows: pl.BlockSpec((1, MLA_HEADS, PROJ_BLOCKS, rows, blk),
                                            lambda bi, ti: (bi, 0, ti, 0, 0))
        q_mla, k_mla, vt_mla, q_sb, k_sb, vt_sb = pl.pallas_call(
            _proj_kernel,
            grid=(b, nb // PROJ_BLOCKS),
            in_specs=[
                pl.BlockSpec((1, tile, d), lambda bi, ti: (bi, ti, 0)),
                pl.BlockSpec((1, tile, 1), lambda bi, ti: (bi, ti, 0)),
                _const_spec((1, MLA_PAD_DIM)),
                _const_spec((1, d)),
                _const_spec((1, Q_LORA_RANK)),
                _const_spec((1, KV_LORA_RANK)),
                _const_spec(w_lat.shape),
                _const_spec((d, 3 * SB_WIDTH)),
                _const_spec(w_uq_lin.shape),
                _const_spec(w_uq_rot.shape),
                _const_spec(w_uk.shape),
                _const_spec(w_uv.shape),
            ],
            out_specs=[head_spec(MLA_PAD_DIM), head_spec(MLA_PAD_DIM), vt_spec(MLA_VT_ROWS),
                       head_spec(SB_HEAD_DIM), head_spec(SB_HEAD_DIM), vt_spec(SB_HEAD_DIM)],
            out_shape=[head_major(MLA_PAD_DIM), head_major(MLA_PAD_DIM), vt_shape(MLA_VT_ROWS),
                       head_major(SB_HEAD_DIM), head_major(SB_HEAD_DIM), vt_shape(SB_HEAD_DIM)],
            compiler_params=params,
            name="proj",
        )(h, pos, freq, norm_mix[l][None, :], q_latent_norm[l][None, :], kv_latent_norm[l][None, :],
          w_lat, w_sb.astype(BF16), w_uq_lin, w_uq_rot, w_uk, w_uv)

        seq_spec = lambda width: pl.BlockSpec((1, 1, s, width), lambda bi, hi: (bi, hi, 0, 0))
        blocked = lambda rows: pl.BlockSpec((1, 1, nb, rows, blk), lambda bi, hi: (bi, hi, 0, 0, 0))
        o_shape = jax.ShapeDtypeStruct((b, MLA_HEADS, nb, MLA_V_DIM, blk), F32)
        score_scratch = [pltpu.VMEM((2, N_CHAINS, blk, blk), F32),
                         pltpu.VMEM((N_CHAINS, blk, blk), BF16)]
        o_mla = pl.pallas_call(
            functools.partial(_mla_kernel, n_super=nb // N_CHAINS),
            grid=(b, MLA_HEADS),
            in_specs=[seq_spec(MLA_PAD_DIM), seq_spec(MLA_PAD_DIM), blocked(MLA_VT_ROWS)],
            out_specs=blocked(MLA_V_DIM),
            out_shape=o_shape,
            scratch_shapes=score_scratch,
            compiler_params=params,
            name="mla_attn",
        )(q_mla, k_mla, vt_mla)
        o_sb = pl.pallas_call(
            functools.partial(_sb_kernel, n_super=nb // N_CHAINS),
            grid=(b, SB_HEADS),
            in_specs=[seq_spec(SB_HEAD_DIM), seq_spec(SB_HEAD_DIM), blocked(SB_HEAD_DIM),
                      _const_spec((blk, blk))],
            out_specs=blocked(SB_HEAD_DIM),
            out_shape=o_shape,
            compiler_params=params,
            name="sb_attn",
        )(q_sb, k_sb, vt_sb, tri)

        assert depth == 1
        o_spec = pl.BlockSpec((1, MLA_HEADS, OUT_BLOCKS, MLA_V_DIM, blk), lambda bi, ti: (bi, 0, ti, 0, 0))
        tok_spec = pl.BlockSpec((1, OUT_BLOCKS * blk, d), lambda bi, ti: (bi, ti, 0))
        resident = lambda shape: pl.BlockSpec(shape, lambda *_: (0,) * len(shape),
                                              pipeline_mode=pl.Buffered(1))
        h = pl.pallas_call(
            functools.partial(_out_kernel, d_ff=d_ff),
            grid=(b, nb // OUT_BLOCKS),
            in_specs=[
                tok_spec, o_spec, o_spec,
                _const_spec((MLA_WIDTH, 1)), _const_spec((SB_WIDTH, 1)),
                resident((MLA_WIDTH, d)), resident((SB_WIDTH, d)),
                _const_spec((1, d)),
                resident((d, d_ff)), resident((d, d_ff)), resident((d_ff, d)),
                _const_spec((1, d)),
            ],
            out_specs=tok_spec,
            out_shape=jax.ShapeDtypeStruct((b, s, d), F32),
            compiler_params=params,
            name="out_ffn",
        )(h, o_mla, o_sb, out_norm_mla[l][:, None], out_norm_sb[l][:, None],
          w_o[l][:MLA_WIDTH].astype(BF16), w_o[l][MLA_WIDTH:].astype(BF16),
          norm_ffn[l][None, :], w_gate[l].astype(BF16), w_up[l].astype(BF16),
          w_down[l].astype(BF16), norm_final[None, :])
    return h
```

```python
import functools
import math

import jax
import jax.numpy as jnp
from jax import lax
from jax.experimental import pallas as pl
from jax.experimental.pallas import tpu as pltpu

EPS = 1e-6
ROPE_THETA = 10000.0
LOG2_E = 1.4426950408889634

MLA_HEADS = 8
MLA_NOPE_DIM = 64
MLA_ROPE_DIM = 32
MLA_V_DIM = 64
MLA_QK_DIM = MLA_NOPE_DIM + MLA_ROPE_DIM
MLA_PAD_DIM = 128
MLA_VT_ROWS = MLA_V_DIM + 16
Q_LORA_RANK = 256
KV_LORA_RANK = 128
SB_HEADS = 8
SB_HEAD_DIM = 64
MLA_WIDTH = MLA_HEADS * MLA_V_DIM
SB_WIDTH = SB_HEADS * SB_HEAD_DIM

SEQ_BLOCK = 256
N_CHAINS = 4
MLA_K_PER_STEP = 4
SB_DEAD_BITS = 160.0
FF_CHUNK = 256
OUT_BLOCKS = 2
PROJ_BLOCKS = 4
VMEM_LIMIT_BYTES = 56 * 1024 * 1024

F32 = jnp.float32
BF16 = jnp.bfloat16


def _rms_scale(v, axis):
    return lax.rsqrt(jnp.mean(v * v, axis=axis, keepdims=True) + EPS)


def _dot(a, b):
    return jnp.dot(a, b, preferred_element_type=F32)


def _dot_nt(a, b):
    return lax.dot_general(a, b, (((1,), (1,)), ((), ())), preferred_element_type=F32)


def _dot_tn(a, b):
    return lax.dot_general(a, b, (((0,), (0,)), ((), ())), preferred_element_type=F32)


def _proj_kernel(x_ref, pos_ref, freq_ref, g_mix_ref, g_q_ref, g_kv_ref,
                 w_lat_ref, w_sb_ref, w_uq_ref, w_uq_rot_ref, w_uk_ref, w_uv_ref,
                 q_mla_ref, k_mla_ref, vt_mla_ref, q_sb_ref, k_sb_ref, vt_sb_ref):
    x = x_ref[0]
    u = (x * _rms_scale(x, -1) * g_mix_ref[...]).astype(BF16)

    ang = pos_ref[0].astype(F32) * freq_ref[...]
    cos = jnp.cos(ang)
    sin = jnp.sin(ang)

    lat = _dot(u, w_lat_ref[...])
    sb = _dot(u, w_sb_ref[...])
    c_q = lat[:, :Q_LORA_RANK]
    c_kv = lat[:, Q_LORA_RANK:Q_LORA_RANK + KV_LORA_RANK]
    k_r = lat[:, Q_LORA_RANK + KV_LORA_RANK:Q_LORA_RANK + KV_LORA_RANK + MLA_PAD_DIM]
    k_r_rot = lat[:, Q_LORA_RANK + KV_LORA_RANK + MLA_PAD_DIM:]
    k_rope = k_r * cos + k_r_rot * sin

    ql = (c_q * _rms_scale(c_q, -1) * g_q_ref[...]).astype(BF16)
    kvl = (c_kv * _rms_scale(c_kv, -1) * g_kv_ref[...]).astype(BF16)

    q_lin = _dot(ql, w_uq_ref[...])
    q_rot = _dot(ql, w_uq_rot_ref[...])
    k_nope = _dot(kvl, w_uk_ref[...])
    v_mla = _dot(kvl, w_uv_ref[...])
    q_scale = LOG2_E / math.sqrt(MLA_QK_DIM)
    for h in range(MLA_HEADS):
        sl = slice(h * MLA_PAD_DIM, (h + 1) * MLA_PAD_DIM)
        q_h = (q_lin[:, sl] * cos + q_rot[:, sl] * sin) * q_scale
        q_mla_ref[0, h] = q_h.astype(BF16)
        k_mla_ref[0, h] = (k_nope[:, sl] + k_rope).astype(BF16)
    v_mla_t = v_mla.T
    pad_row = lax.broadcasted_iota(jnp.int32, (MLA_VT_ROWS - MLA_V_DIM, SEQ_BLOCK), 0)
    ones_then_zeros = jnp.where(pad_row == 0, 1.0, 0.0).astype(BF16)
    for h in range(MLA_HEADS):
        for t in range(PROJ_BLOCKS):
            vt_mla_ref[0, h, t, :MLA_V_DIM, :] = v_mla_t[h * MLA_V_DIM:(h + 1) * MLA_V_DIM,
                                                         t * SEQ_BLOCK:(t + 1) * SEQ_BLOCK].astype(BF16)
            vt_mla_ref[0, h, t, MLA_V_DIM:, :] = ones_then_zeros

    sb_scale = LOG2_E / math.sqrt(SB_HEAD_DIM)
    q_sb = sb[:, :SB_WIDTH] * sb_scale
    k_sb = sb[:, SB_WIDTH:2 * SB_WIDTH]
    v_sb_t = sb[:, 2 * SB_WIDTH:].T
    for h in range(SB_HEADS):
        sl = slice(h * SB_HEAD_DIM, (h + 1) * SB_HEAD_DIM)
        q_sb_ref[0, h] = q_sb[:, sl].astype(BF16)
        k_sb_ref[0, h] = k_sb[:, sl].astype(BF16)
        for t in range(PROJ_BLOCKS):
            vt_sb_ref[0, h, t] = v_sb_t[sl, t * SEQ_BLOCK:(t + 1) * SEQ_BLOCK].astype(BF16)


def _attention_driver(q_ref, k_ref, vt_ref, o_ref, z_scr, w_scr, front_fn, apply_fn, finish_fn,
                      diag_mask, n_super, k_per_step, side_stages, side_tail):
    blk, n = SEQ_BLOCK, N_CHAINS
    unit = jnp.ones((1, blk), F32)

    def rows(ref, j):
        return ref[0, 0, pl.ds(pl.multiple_of(j * blk, blk), blk), :]

    def super_body(sb, carry):
        base = sb * n

        def scores(j, chains):
            k = rows(k_ref, j)
            return [_dot_nt(k, rows(q_ref, base + a)) for a in chains]

        def park_scores(slot, j):
            for a, z in enumerate(scores(j, range(n))):
                z_scr[slot, a] = z

        def flush(j, accs, factors):
            vt = vt_ref[0, 0, j]
            return [apply_fn(acc, _dot(vt, w_scr[a]), f) for a, (acc, f) in enumerate(zip(accs, factors))]

        smalls, accs = [None] * n, [None] * n
        side = side_stages(base)
        side.pop(0)()
        zs = scores(base + n - 1, [n - 1])
        pending = None
        for kb in reversed(range(n)):
            chains = list(range(kb, n))
            if kb > 0:
                zs_next = scores(base + kb - 1, range(kb - 1, n))
            else:
                park_scores(0, jnp.maximum(base - 1, 0))
            if pending is not None:
                vt = vt_ref[0, 0, base + kb + 1]
                for a, w, f in zip(*pending):
                    accs[a] = apply_fn(accs[a], _dot(vt, w), f)
            new_smalls, ws, factors = front_fn(zs, [smalls[a] for a in chains],
                                               [diag_mask if a == kb else None for a in chains], lambda: None)
            for a, sm in zip(chains, new_smalls):
                smalls[a] = sm
            side.pop(0)()
            if kb > 0:
                pending = (chains, ws, factors)
                zs = zs_next
            else:
                for a in range(n):
                    w_scr[a] = ws[a]
                accs[0] = jnp.zeros((vt_ref.shape[3], blk), F32)
                factors[0] = unit

        def k_body(it, state):
            smalls, accs, factors = (list(t) for t in state)
            j = base - 1 - k_per_step * it
            for u in range(k_per_step):
                slot, j_cur, j_next = u % 2, j - u, jnp.maximum(j - u - 1, 0)
                k_next, vt_prev = rows(k_ref, j_next), vt_ref[0, 0, j_cur + 1]
                for a in range(n):
                    z_scr[1 - slot, a] = _dot_nt(k_next, rows(q_ref, base + a))
                    accs[a] = apply_fn(accs[a], _dot(vt_prev, w_scr[a]), factors[a])
                smalls, ws, factors = front_fn([z_scr[slot, a] for a in range(n)], smalls, [None] * n,
                                               lambda: None)
                for a in range(n):
                    w_scr[a] = ws[a]
            return tuple(smalls), tuple(accs), tuple(factors)

        smalls, accs, factors = lax.fori_loop(
            0, sb * (n // k_per_step), k_body, (tuple(smalls), tuple(accs), tuple(factors)))
        accs = flush(0, accs, factors)
        for a in range(n):
            o_ref[0, 0, base + a] = finish_fn(smalls[a], accs[a])
        side_tail(base)
        return carry

    lax.fori_loop(0, n_super, super_body, 0)


def _block_iotas():
    key_idx = lax.broadcasted_iota(jnp.int32, (SEQ_BLOCK, SEQ_BLOCK), 0)
    qry_idx = lax.broadcasted_iota(jnp.int32, (SEQ_BLOCK, SEQ_BLOCK), 1)
    return key_idx, qry_idx


def _attn_kernel(q_ref, k_ref, vt_ref, q_sb_ref, k_sb_ref, vt_sb_ref, tri_ref, o_ref, o_sb_ref,
                 z_scr, w_scr, *, n_super):
    key_idx, qry_idx = _block_iotas()

    def front(zs, smalls, masks, prefetch):
        prefetch()
        new_smalls, ws, alphas = [], [], []
        for s, small, mask in zip(zs, smalls, masks):
            if mask is not None:
                s = jnp.where(mask, s, -jnp.inf)
            s_max = jnp.max(s, axis=0, keepdims=True)
            if small is None:
                m_new, alpha = s_max, None
            else:
                m_new = jnp.maximum(small[0], s_max)
                alpha = jnp.exp2(small[0] - m_new)
            new_smalls.append((m_new,))
            ws.append(jnp.exp2(s - m_new).astype(BF16))
            alphas.append(alpha)
        return new_smalls, ws, alphas

    def apply(acc, pv, alpha):
        return pv if acc is None else alpha * acc + pv

    def finish(small, acc):
        return acc[:MLA_V_DIM] / acc[MLA_V_DIM:MLA_V_DIM + 1]

    side_stages, side_tail = _sb_super_block(q_sb_ref, k_sb_ref, vt_sb_ref, tri_ref, o_sb_ref)
    _attention_driver(q_ref, k_ref, vt_ref, o_ref, z_scr, w_scr, front, apply, finish,
                      key_idx <= qry_idx, n_super, MLA_K_PER_STEP, side_stages, side_tail)


def _softplus2(z):
    return jnp.maximum(z, 0.0) + jnp.log2(1.0 + jnp.exp2(-jnp.abs(z)))


def _sb_super_block(q_ref, k_ref, vt_ref, tri_ref, o_ref):
    blk, n = SEQ_BLOCK, N_CHAINS
    key_idx, qry_idx = _block_iotas()
    strict = key_idx < qry_idx

    def rows(ref, j):
        return ref[0, 0, pl.ds(pl.multiple_of(j * blk, blk), blk), :]

    def logits(base, specs):
        return [_dot_nt(rows(k_ref, j), rows(q_ref, base + a)) for a, j, _ in specs]

    def suffix_sums(zs, specs):
        log_betas, sps = [], []
        for z, (_, _, diag) in zip(zs, specs):
            sp = _softplus2(z)
            log_betas.append(z - sp)
            if diag:
                sp = jnp.where(strict, sp, 0.0)
            sps.append(sp.astype(BF16))
        return log_betas, sps, [_dot(tri_ref[...], sp) for sp in sps]

    def products(log_betas, sps, sufs, specs):
        ws = []
        for log_beta, suf, (_, _, diag) in zip(log_betas, sufs, specs):
            w = jnp.exp2(log_beta + suf)
            if diag:
                w = jnp.where(strict, w, 0.0)
            ws.append(w.astype(BF16))
        return [(suf[0:1, :] - sp[0:1, :].astype(F32), _dot(vt_ref[0, 0, j], w))
                for w, sp, suf, (_, j, _) in zip(ws, sps, sufs, specs)]

    def tiles(base, groups):
        zs = [logits(base, g) for g in groups]
        out, prev = [], None
        for g, z in zip(groups, zs):
            cur = suffix_sums(z, g) + (g,)
            if prev is not None:
                out += products(*prev)
            prev = cur
        return out + products(*prev)

    def band(base, d):
        return [(a, jnp.maximum(base + a - d, 0), False) for a in range(n)]

    def factor(base, a, d, c):
        return jnp.where(base + a - d >= 0, jnp.exp2(c), 0.0)

    def live(cs):
        return (jnp.max(functools.reduce(jnp.maximum, cs)) > -SB_DEAD_BITS).astype(jnp.int32)

    carry_sums = {}

    def stages(base):
        groups = [[(a, base + a, True) for a in range(n)], band(base, 1)]
        box = {}

        def s0():
            box["z"] = [logits(base, g) for g in groups]

        def s1():
            box["c0"] = suffix_sums(box["z"][0], groups[0])

        def s2():
            box["c1"] = suffix_sums(box["z"][1], groups[1])
            box["r0"] = products(*box["c0"], groups[0])

        def s3():
            box["r1"] = products(*box["c1"], groups[1])

        def s4():
            cs = []
            for a in range(n):
                (sum0, pv0), (sum1, pv1) = box["r0"][a], box["r1"][a]
                o_ref[0, 0, base + a] = pv0 + pv1 * factor(base, a, 1, sum0)
                cs.append(sum0 + sum1)
            carry_sums[0] = cs

        return [s0, s1, s2, s3, s4]

    def tail(base):
        def w_body(state):
            d, _, cs = state
            res = tiles(base, [band(base, d)])
            for a, (c, (_, pv)) in enumerate(zip(cs, res)):
                o_ref[0, 0, base + a] = o_ref[0, 0, base + a] + pv * factor(base, a, d, c)
            cs = tuple(c + block_sum for c, (block_sum, _) in zip(cs, res))
            return d + 1, live(cs), cs

        cs = carry_sums[0]
        lax.while_loop(lambda state: jnp.logical_and(state[0] < base + n, state[1] > 0), w_body,
                       (jnp.int32(2), live(cs), tuple(cs)))

    return stages, tail


def _out_kernel(x_ref, o_mla_ref, o_sb_ref, g_mla_ref, g_sb_ref, w_o_mla_ref, w_o_sb_ref,
                g_ffn_ref, w_gate_ref, w_up_ref, w_down_ref, g_final_ref, out_ref, *, d_ff):
    blk = SEQ_BLOCK

    def group(o_ref, g_ref, w_ref):
        parts = []
        for t in range(OUT_BLOCKS):
            o = o_ref[0, :, t].reshape(-1, blk)
            y = (o * _rms_scale(o, 0) * g_ref[...]).astype(BF16)
            parts.append(_dot_tn(y, w_ref[...]))
        return jnp.concatenate(parts, axis=0)

    h = x_ref[0] + group(o_mla_ref, g_mla_ref, w_o_mla_ref) + group(o_sb_ref, g_sb_ref, w_o_sb_ref)
    f = (h * _rms_scale(h, -1) * g_ffn_ref[...]).astype(BF16)
    ffn = jnp.zeros_like(h)
    for c in range(0, d_ff, FF_CHUNK):
        gate = _dot(f, w_gate_ref[:, c:c + FF_CHUNK])
        up = _dot(f, w_up_ref[:, c:c + FF_CHUNK])
        act = (gate * jax.nn.sigmoid(gate) * up).astype(BF16)
        ffn = ffn + _dot(act, w_down_ref[c:c + FF_CHUNK, :])
    h = h + ffn
    out_ref[0] = h * _rms_scale(h, -1) * g_final_ref[...]


def _rotate_half_cols(w):
    half = w.shape[-1] // 2
    return jnp.concatenate([-w[..., half:], w[..., :half]], axis=-1)


def _head_slots(nope, rope):
    ref = nope if nope is not None else rope
    r, h = ref.shape[0], ref.shape[1]
    nope = jnp.zeros((r, h, MLA_NOPE_DIM), ref.dtype) if nope is None else nope
    rope = jnp.zeros((r, h, MLA_ROPE_DIM), ref.dtype) if rope is None else rope
    pad = jnp.zeros((r, h, MLA_PAD_DIM - MLA_QK_DIM), ref.dtype)
    return jnp.concatenate([nope, rope, pad], axis=-1).reshape(r, h * MLA_PAD_DIM)


def _const_spec(shape):
    return pl.BlockSpec(shape, lambda *_: (0,) * len(shape))


def kernel(x, positions, norm_mix, w_in, q_latent_norm, w_uq, kv_latent_norm, w_ukv,
           out_norm_mla, out_norm_sb, w_o, norm_ffn, w_gate, w_up, w_down, norm_final):
    b, s, d = x.shape
    depth = w_in.shape[0]
    d_ff = w_gate.shape[-1]
    blk = SEQ_BLOCK
    nb = s // blk
    assert s % (blk * N_CHAINS) == 0 and N_CHAINS % 2 == 0 and d_ff % FF_CHUNK == 0

    inv_freq = ROPE_THETA ** (-jnp.arange(0, MLA_ROPE_DIM, 2, dtype=F32) / MLA_ROPE_DIM)
    zeros = lambda n: jnp.zeros((n,), F32)
    freq = jnp.concatenate([zeros(MLA_NOPE_DIM), inv_freq, inv_freq,
                            zeros(MLA_PAD_DIM - MLA_QK_DIM)])[None, :]
    tri = jnp.where(jnp.arange(blk)[None, :] > jnp.arange(blk)[:, None], -1.0, 0.0).astype(BF16)
    pos = positions.reshape(b, s, 1)

    params = pltpu.CompilerParams(
        dimension_semantics=("arbitrary", "arbitrary"), vmem_limit_bytes=VMEM_LIMIT_BYTES)

    h = x
    for l in range(depth):
        o0 = Q_LORA_RANK
        o1 = o0 + KV_LORA_RANK
        o2 = o1 + MLA_ROPE_DIM
        w_cq, w_ckv, w_kr, w_sb = w_in[l][:, :o0], w_in[l][:, o0:o1], w_in[l][:, o1:o2], w_in[l][:, o2:]
        kr_slot = lambda w: _head_slots(None, w[:, None, :])
        w_lat = jnp.concatenate([w_cq, w_ckv, kr_slot(w_kr), kr_slot(_rotate_half_cols(w_kr))],
                                axis=1).astype(BF16)
        uq = w_uq[l].reshape(Q_LORA_RANK, MLA_HEADS, MLA_QK_DIM)
        uq_nope, uq_rope = uq[..., :MLA_NOPE_DIM], uq[..., MLA_NOPE_DIM:]
        w_uq_lin = _head_slots(uq_nope, uq_rope).astype(BF16)
        w_uq_rot = _head_slots(None, _rotate_half_cols(uq_rope)).astype(BF16)
        ukv = w_ukv[l].reshape(KV_LORA_RANK, MLA_HEADS, MLA_NOPE_DIM + MLA_V_DIM)
        w_uk = _head_slots(ukv[..., :MLA_NOPE_DIM], None).astype(BF16)
        w_uv = ukv[..., MLA_NOPE_DIM:].reshape(KV_LORA_RANK, MLA_WIDTH).astype(BF16)

        head_major = lambda width: jax.ShapeDtypeStruct((b, MLA_HEADS, s, width), BF16)
        tile = PROJ_BLOCKS * blk
        head_spec = lambda width: pl.BlockSpec((1, MLA_HEADS, tile, width), lambda bi, ti: (bi, 0, ti, 0))
        vt_shape = lambda rows: jax.ShapeDtypeStruct((b, MLA_HEADS, nb, rows, blk), BF16)
        vt_spec = lambda rows: pl.BlockSpec((1, MLA_HEADS, PROJ_BLOCKS, rows, blk),
                                            lambda bi, ti: (bi, 0, ti, 0, 0))
        q_mla, k_mla, vt_mla, q_sb, k_sb, vt_sb = pl.pallas_call(
            _proj_kernel,
            grid=(b, nb // PROJ_BLOCKS),
            in_specs=[
                pl.BlockSpec((1, tile, d), lambda bi, ti: (bi, ti, 0)),
                pl.BlockSpec((1, tile, 1), lambda bi, ti: (bi, ti, 0)),
                _const_spec((1, MLA_PAD_DIM)),
                _const_spec((1, d)),
                _const_spec((1, Q_LORA_RANK)),
                _const_spec((1, KV_LORA_RANK)),
                _const_spec(w_lat.shape),
                _const_spec((d, 3 * SB_WIDTH)),
                _const_spec(w_uq_lin.shape),
                _const_spec(w_uq_rot.shape),
                _const_spec(w_uk.shape),
                _const_spec(w_uv.shape),
            ],
            out_specs=[head_spec(MLA_PAD_DIM), head_spec(MLA_PAD_DIM), vt_spec(MLA_VT_ROWS),
                       head_spec(SB_HEAD_DIM), head_spec(SB_HEAD_DIM), vt_spec(SB_HEAD_DIM)],
            out_shape=[head_major(MLA_PAD_DIM), head_major(MLA_PAD_DIM), vt_shape(MLA_VT_ROWS),
                       head_major(SB_HEAD_DIM), head_major(SB_HEAD_DIM), vt_shape(SB_HEAD_DIM)],
            compiler_params=params,
            name="proj",
        )(h, pos, freq, norm_mix[l][None, :], q_latent_norm[l][None, :], kv_latent_norm[l][None, :],
          w_lat, w_sb.astype(BF16), w_uq_lin, w_uq_rot, w_uk, w_uv)

        seq_spec = lambda width: pl.BlockSpec((1, 1, s, width), lambda bi, hi: (bi, hi, 0, 0))
        blocked = lambda rows: pl.BlockSpec((1, 1, nb, rows, blk), lambda bi, hi: (bi, hi, 0, 0, 0))
        o_shape = jax.ShapeDtypeStruct((b, MLA_HEADS, nb, MLA_V_DIM, blk), F32)
        score_scratch = [pltpu.VMEM((2, N_CHAINS, blk, blk), F32),
                         pltpu.VMEM((N_CHAINS, blk, blk), BF16)]
        assert MLA_HEADS == SB_HEADS
        o_mla, o_sb = pl.pallas_call(
            functools.partial(_attn_kernel, n_super=nb // N_CHAINS),
            grid=(b, MLA_HEADS),
            in_specs=[seq_spec(MLA_PAD_DIM), seq_spec(MLA_PAD_DIM), blocked(MLA_VT_ROWS),
                      seq_spec(SB_HEAD_DIM), seq_spec(SB_HEAD_DIM), blocked(SB_HEAD_DIM),
                      _const_spec((blk, blk))],
            out_specs=[blocked(MLA_V_DIM), blocked(SB_HEAD_DIM)],
            out_shape=[o_shape, o_shape],
            scratch_shapes=score_scratch,
            compiler_params=params,
            name="attn",
        )(q_mla, k_mla, vt_mla, q_sb, k_sb, vt_sb, tri)

        assert depth == 1
        o_spec = pl.BlockSpec((1, MLA_HEADS, OUT_BLOCKS, MLA_V_DIM, blk), lambda bi, ti: (bi, 0, ti, 0, 0))
        tok_spec = pl.BlockSpec((1, OUT_BLOCKS * blk, d), lambda bi, ti: (bi, ti, 0))
        resident = lambda shape: pl.BlockSpec(shape, lambda *_: (0,) * len(shape),
                                              pipeline_mode=pl.Buffered(1))
        h = pl.pallas_call(
            functools.partial(_out_kernel, d_ff=d_ff),
            grid=(b, nb // OUT_BLOCKS),
            in_specs=[
                tok_spec, o_spec, o_spec,
                _const_spec((MLA_WIDTH, 1)), _const_spec((SB_WIDTH, 1)),
                resident((MLA_WIDTH, d)), resident((SB_WIDTH, d)),
                _const_spec((1, d)),
                resident((d, d_ff)), resident((d, d_ff)), resident((d_ff, d)),
                _const_spec((1, d)),
            ],
            out_specs=tok_spec,
            out_shape=jax.ShapeDtypeStruct((b, s, d), F32),
            compiler_params=params,
            name="out_ffn",
        )(h, o_mla, o_sb, out_norm_mla[l][:, None], out_norm_sb[l][:, None],
          w_o[l][:MLA_WIDTH].astype(BF16), w_o[l][MLA_WIDTH:].astype(BF16),
          norm_ffn[l][None, :], w_gate[l].astype(BF16), w_up[l].astype(BF16),
          w_down[l].astype(BF16), norm_final[None, :])
    return h
```

```python
import functools
import math

import jax
import jax.numpy as jnp
from jax import lax
from jax.experimental import pallas as pl
from jax.experimental.pallas import tpu as pltpu

EPS = 1e-6
ROPE_THETA = 10000.0
LOG2_E = 1.4426950408889634

MLA_HEADS = 8
MLA_NOPE_DIM = 64
MLA_ROPE_DIM = 32
MLA_V_DIM = 64
MLA_QK_DIM = MLA_NOPE_DIM + MLA_ROPE_DIM
MLA_PAD_DIM = 128
MLA_VT_ROWS = MLA_V_DIM + 16
Q_LORA_RANK = 256
KV_LORA_RANK = 128
SB_HEADS = 8
SB_HEAD_DIM = 64
MLA_WIDTH = MLA_HEADS * MLA_V_DIM
SB_WIDTH = SB_HEADS * SB_HEAD_DIM

SEQ_BLOCK = 256
N_CHAINS = 4
MLA_K_PER_STEP = 4
SB_DEAD_BITS = 160.0
FF_CHUNK = 256
OUT_BLOCKS = 2
PROJ_BLOCKS = 4
VMEM_LIMIT_BYTES = 56 * 1024 * 1024

F32 = jnp.float32
BF16 = jnp.bfloat16


def _rms_scale(v, axis):
    return lax.rsqrt(jnp.mean(v * v, axis=axis, keepdims=True) + EPS)


def _dot(a, b):
    return jnp.dot(a, b, preferred_element_type=F32)


def _dot_nt(a, b):
    return lax.dot_general(a, b, (((1,), (1,)), ((), ())), preferred_element_type=F32)


def _dot_tn(a, b):
    return lax.dot_general(a, b, (((0,), (0,)), ((), ())), preferred_element_type=F32)


def _proj_kernel(x_ref, pos_ref, freq_ref, g_mix_ref, g_q_ref, g_kv_ref,
                 w_lat_ref, w_sb_ref, w_uq_ref, w_uq_rot_ref, w_uk_ref, w_uv_ref,
                 q_mla_ref, k_mla_ref, vt_mla_ref, q_sb_ref, k_sb_ref, vt_sb_ref):
    x = x_ref[0]
    u = (x * _rms_scale(x, -1) * g_mix_ref[...]).astype(BF16)

    ang = pos_ref[0].astype(F32) * freq_ref[...]
    cos = jnp.cos(ang)
    sin = jnp.sin(ang)

    lat = _dot(u, w_lat_ref[...])
    sb = _dot(u, w_sb_ref[...])
    c_q = lat[:, :Q_LORA_RANK]
    c_kv = lat[:, Q_LORA_RANK:Q_LORA_RANK + KV_LORA_RANK]
    k_r = lat[:, Q_LORA_RANK + KV_LORA_RANK:Q_LORA_RANK + KV_LORA_RANK + MLA_PAD_DIM]
    k_r_rot = lat[:, Q_LORA_RANK + KV_LORA_RANK + MLA_PAD_DIM:]
    k_rope = k_r * cos + k_r_rot * sin

    ql = (c_q * _rms_scale(c_q, -1) * g_q_ref[...]).astype(BF16)
    kvl = (c_kv * _rms_scale(c_kv, -1) * g_kv_ref[...]).astype(BF16)

    q_lin = _dot(ql, w_uq_ref[...])
    q_rot = _dot(ql, w_uq_rot_ref[...])
    k_nope = _dot(kvl, w_uk_ref[...])
    v_mla = _dot(kvl, w_uv_ref[...])
    q_scale = LOG2_E / math.sqrt(MLA_QK_DIM)
    for h in range(MLA_HEADS):
        sl = slice(h * MLA_PAD_DIM, (h + 1) * MLA_PAD_DIM)
        q_h = (q_lin[:, sl] * cos + q_rot[:, sl] * sin) * q_scale
        q_mla_ref[0, h] = q_h.astype(BF16)
        k_mla_ref[0, h] = (k_nope[:, sl] + k_rope).astype(BF16)
    v_mla_t = v_mla.T
    pad_row = lax.broadcasted_iota(jnp.int32, (MLA_VT_ROWS - MLA_V_DIM, SEQ_BLOCK), 0)
    ones_then_zeros = jnp.where(pad_row == 0, 1.0, 0.0).astype(BF16)
    for h in range(MLA_HEADS):
        for t in range(PROJ_BLOCKS):
            vt_mla_ref[0, h, t, :MLA_V_DIM, :] = v_mla_t[h * MLA_V_DIM:(h + 1) * MLA_V_DIM,
                                                         t * SEQ_BLOCK:(t + 1) * SEQ_BLOCK].astype(BF16)
            vt_mla_ref[0, h, t, MLA_V_DIM:, :] = ones_then_zeros

    sb_scale = LOG2_E / math.sqrt(SB_HEAD_DIM)
    q_sb = sb[:, :SB_WIDTH] * sb_scale
    k_sb = sb[:, SB_WIDTH:2 * SB_WIDTH]
    v_sb_t = sb[:, 2 * SB_WIDTH:].T
    for h in range(SB_HEADS):
        sl = slice(h * SB_HEAD_DIM, (h + 1) * SB_HEAD_DIM)
        q_sb_ref[0, h] = q_sb[:, sl].astype(BF16)
        k_sb_ref[0, h] = k_sb[:, sl].astype(BF16)
        for t in range(PROJ_BLOCKS):
            vt_sb_ref[0, h, t] = v_sb_t[sl, t * SEQ_BLOCK:(t + 1) * SEQ_BLOCK].astype(BF16)


def _block_iotas():
    key_idx = lax.broadcasted_iota(jnp.int32, (SEQ_BLOCK, SEQ_BLOCK), 0)
    qry_idx = lax.broadcasted_iota(jnp.int32, (SEQ_BLOCK, SEQ_BLOCK), 1)
    return key_idx, qry_idx


def _attn_kernel(q_ref, k_ref, vt_ref, q_sb_ref, k_sb_ref, vt_sb_ref, tri_ref, o_ref, o_sb_ref,
                 z_scr, w_scr, *, n_super):
    blk, n, k_per_step = SEQ_BLOCK, N_CHAINS, MLA_K_PER_STEP
    key_idx, qry_idx = _block_iotas()
    causal = key_idx <= qry_idx
    unit = jnp.ones((1, blk), F32)
    sb_stages, sb_tail = _sb_super_block(q_sb_ref, k_sb_ref, vt_sb_ref, tri_ref, o_sb_ref)

    def rows(ref, j):
        return ref[0, 0, pl.ds(pl.multiple_of(j * blk, blk), blk), :]

    def softmax_blocks(zs, ms, diag_chain):
        new_ms, ws, alphas = [], [], []
        for i, (s, m) in enumerate(zip(zs, ms)):
            if i == diag_chain:
                s = jnp.where(causal, s, -jnp.inf)
            s_max = jnp.max(s, axis=0, keepdims=True)
            if m is None:
                m_new, alpha = s_max, None
            else:
                m_new = jnp.maximum(m, s_max)
                alpha = jnp.exp2(m - m_new)
            new_ms.append(m_new)
            ws.append(jnp.exp2(s - m_new).astype(BF16))
            alphas.append(alpha)
        return new_ms, ws, alphas

    def rescale_add(acc, pv, alpha):
        return pv if acc is None else alpha * acc + pv

    def super_body(sb, carry):
        base = sb * n

        def scores(j, chains):
            k = rows(k_ref, j)
            return [_dot_nt(k, rows(q_ref, base + a)) for a in chains]

        ms, accs = [None] * n, [None] * n
        side = sb_stages(base)
        side.pop(0)()
        zs = scores(base + n - 1, [n - 1])
        pending = None
        for kb in reversed(range(n)):
            chains = list(range(kb, n))
            if kb > 0:
                zs_next = scores(base + kb - 1, range(kb - 1, n))
            else:
                for a, z in enumerate(scores(jnp.maximum(base - 1, 0), range(n))):
                    z_scr[0, a] = z
            if pending is not None:
                vt = vt_ref[0, 0, base + kb + 1]
                for a, w, alpha in zip(*pending):
                    accs[a] = rescale_add(accs[a], _dot(vt, w), alpha)
            new_ms, ws, alphas = softmax_blocks(zs, [ms[a] for a in chains], 0)
            for a, m in zip(chains, new_ms):
                ms[a] = m
            side.pop(0)()
            if kb > 0:
                pending = (chains, ws, alphas)
                zs = zs_next
            else:
                for a in range(n):
                    w_scr[a] = ws[a]
                accs[0] = jnp.zeros((vt_ref.shape[3], blk), F32)
                alphas[0] = unit

        def k_body(it, state):
            ms, accs, alphas = (list(t) for t in state)
            j = base - 1 - k_per_step * it
            for u in range(k_per_step):
                slot, j_cur, j_next = u % 2, j - u, jnp.maximum(j - u - 1, 0)
                k_next, vt_prev = rows(k_ref, j_next), vt_ref[0, 0, j_cur + 1]
                for a in range(n):
                    z_scr[1 - slot, a] = _dot_nt(k_next, rows(q_ref, base + a))
                    accs[a] = rescale_add(accs[a], _dot(vt_prev, w_scr[a]), alphas[a])
                ms, ws, alphas = softmax_blocks([z_scr[slot, a] for a in range(n)], ms, None)
                for a in range(n):
                    w_scr[a] = ws[a]
            return tuple(ms), tuple(accs), tuple(alphas)

        ms, accs, alphas = lax.fori_loop(
            0, sb * (n // k_per_step), k_body, (tuple(ms), tuple(accs), tuple(alphas)))
        vt = vt_ref[0, 0, 0]
        accs = [rescale_add(acc, _dot(vt, w_scr[a]), alpha)
                for a, (acc, alpha) in enumerate(zip(accs, alphas))]
        for a in range(n):
            o_ref[0, 0, base + a] = accs[a][:MLA_V_DIM] / accs[a][MLA_V_DIM:MLA_V_DIM + 1]
        sb_tail(base)
        return carry

    lax.fori_loop(0, n_super, super_body, 0)


def _softplus2(z):
    return jnp.maximum(z, 0.0) + jnp.log2(1.0 + jnp.exp2(-jnp.abs(z)))


def _sb_super_block(q_ref, k_ref, vt_ref, tri_ref, o_ref):
    blk, n = SEQ_BLOCK, N_CHAINS
    key_idx, qry_idx = _block_iotas()
    strict = key_idx < qry_idx

    def rows(ref, j):
        return ref[0, 0, pl.ds(pl.multiple_of(j * blk, blk), blk), :]

    def logits(base, specs):
        return [_dot_nt(rows(k_ref, j), rows(q_ref, base + a)) for a, j, _ in specs]

    def suffix_sums(zs, specs):
        log_betas, sps = [], []
        for z, (_, _, diag) in zip(zs, specs):
            sp = _softplus2(z)
            log_betas.append(z - sp)
            if diag:
                sp = jnp.where(strict, sp, 0.0)
            sps.append(sp.astype(BF16))
        return log_betas, sps, [_dot(tri_ref[...], sp) for sp in sps]

    def products(log_betas, sps, sufs, specs):
        ws = []
        for log_beta, suf, (_, _, diag) in zip(log_betas, sufs, specs):
            w = jnp.exp2(log_beta + suf)
            if diag:
                w = jnp.where(strict, w, 0.0)
            ws.append(w.astype(BF16))
        return [(suf[0:1, :] - sp[0:1, :].astype(F32), _dot(vt_ref[0, 0, j], w))
                for w, sp, suf, (_, j, _) in zip(ws, sps, sufs, specs)]

    def tiles(base, groups):
        zs = [logits(base, g) for g in groups]
        out, prev = [], None
        for g, z in zip(groups, zs):
            cur = suffix_sums(z, g) + (g,)
            if prev is not None:
                out += products(*prev)
            prev = cur
        return out + products(*prev)

    def band(base, d):
        return [(a, jnp.maximum(base + a - d, 0), False) for a in range(n)]

    def factor(base, a, d, c):
        return jnp.where(base + a - d >= 0, jnp.exp2(c), 0.0)

    def live(cs):
        return (jnp.max(functools.reduce(jnp.maximum, cs)) > -SB_DEAD_BITS).astype(jnp.int32)

    carry_sums = {}

    def stages(base):
        groups = [[(a, base + a, True) for a in range(n)], band(base, 1)]
        box = {}

        def s0():
            box["z"] = [logits(base, g) for g in groups]

        def s1():
            box["c0"] = suffix_sums(box["z"][0], groups[0])

        def s2():
            box["c1"] = suffix_sums(box["z"][1], groups[1])
            box["r0"] = products(*box["c0"], groups[0])

        def s3():
            box["r1"] = products(*box["c1"], groups[1])

        def s4():
            cs = []
            for a in range(n):
                (sum0, pv0), (sum1, pv1) = box["r0"][a], box["r1"][a]
                o_ref[0, 0, base + a] = pv0 + pv1 * factor(base, a, 1, sum0)
                cs.append(sum0 + sum1)
            carry_sums[0] = cs

        return [s0, s1, s2, s3, s4]

    def tail(base):
        def w_body(state):
            d, _, cs = state
            res = tiles(base, [band(base, d)])
            for a, (c, (_, pv)) in enumerate(zip(cs, res)):
                o_ref[0, 0, base + a] = o_ref[0, 0, base + a] + pv * factor(base, a, d, c)
            cs = tuple(c + block_sum for c, (block_sum, _) in zip(cs, res))
            return d + 1, live(cs), cs

        cs = carry_sums[0]
        lax.while_loop(lambda state: jnp.logical_and(state[0] < base + n, state[1] > 0), w_body,
                       (jnp.int32(2), live(cs), tuple(cs)))

    return stages, tail


def _out_kernel(x_ref, o_mla_ref, o_sb_ref, g_mla_ref, g_sb_ref, w_o_mla_ref, w_o_sb_ref,
                g_ffn_ref, w_gate_ref, w_up_ref, w_down_ref, g_final_ref, out_ref, *, d_ff):
    blk = SEQ_BLOCK

    def group(o_ref, g_ref, w_ref):
        parts = []
        for t in range(OUT_BLOCKS):
            o = o_ref[0, :, t].reshape(-1, blk)
            y = (o * _rms_scale(o, 0) * g_ref[...]).astype(BF16)
            parts.append(_dot_tn(y, w_ref[...]))
        return jnp.concatenate(parts, axis=0)

    h = x_ref[0] + group(o_mla_ref, g_mla_ref, w_o_mla_ref) + group(o_sb_ref, g_sb_ref, w_o_sb_ref)
    f = (h * _rms_scale(h, -1) * g_ffn_ref[...]).astype(BF16)
    ffn = jnp.zeros_like(h)
    for c in range(0, d_ff, FF_CHUNK):
        gate = _dot(f, w_gate_ref[:, c:c + FF_CHUNK])
        up = _dot(f, w_up_ref[:, c:c + FF_CHUNK])
        act = (gate * jax.nn.sigmoid(gate) * up).astype(BF16)
        ffn = ffn + _dot(act, w_down_ref[c:c + FF_CHUNK, :])
    h = h + ffn
    out_ref[0] = h * _rms_scale(h, -1) * g_final_ref[...]


def _rotate_half_cols(w):
    half = w.shape[-1] // 2
    return jnp.concatenate([-w[..., half:], w[..., :half]], axis=-1)


def _head_slots(nope, rope):
    ref = nope if nope is not None else rope
    r, h = ref.shape[0], ref.shape[1]
    nope = jnp.zeros((r, h, MLA_NOPE_DIM), ref.dtype) if nope is None else nope
    rope = jnp.zeros((r, h, MLA_ROPE_DIM), ref.dtype) if rope is None else rope
    pad = jnp.zeros((r, h, MLA_PAD_DIM - MLA_QK_DIM), ref.dtype)
    return jnp.concatenate([nope, rope, pad], axis=-1).reshape(r, h * MLA_PAD_DIM)


def _const_spec(shape):
    return pl.BlockSpec(shape, lambda *_: (0,) * len(shape))


def kernel(x, positions, norm_mix, w_in, q_latent_norm, w_uq, kv_latent_norm, w_ukv,
           out_norm_mla, out_norm_sb, w_o, norm_ffn, w_gate, w_up, w_down, norm_final):
    b, s, d = x.shape
    depth = w_in.shape[0]
    d_ff = w_gate.shape[-1]
    blk = SEQ_BLOCK
    nb = s // blk
    assert s % (blk * N_CHAINS) == 0 and N_CHAINS % 2 == 0 and d_ff % FF_CHUNK == 0

    inv_freq = ROPE_THETA ** (-jnp.arange(0, MLA_ROPE_DIM, 2, dtype=F32) / MLA_ROPE_DIM)
    zeros = lambda n: jnp.zeros((n,), F32)
    freq = jnp.concatenate([zeros(MLA_NOPE_DIM), inv_freq, inv_freq,
                            zeros(MLA_PAD_DIM - MLA_QK_DIM)])[None, :]
    tri = jnp.where(jnp.arange(blk)[None, :] > jnp.arange(blk)[:, None], -1.0, 0.0).astype(BF16)
    pos = positions.reshape(b, s, 1)

    params = pltpu.CompilerParams(
        dimension_semantics=("arbitrary", "arbitrary"), vmem_limit_bytes=VMEM_LIMIT_BYTES)

    h = x
    for l in range(depth):
        o0 = Q_LORA_RANK
        o1 = o0 + KV_LORA_RANK
        o2 = o1 + MLA_ROPE_DIM
        w_cq, w_ckv, w_kr, w_sb = w_in[l][:, :o0], w_in[l][:, o0:o1], w_in[l][:, o1:o2], w_in[l][:, o2:]
        kr_slot = lambda w: _head_slots(None, w[:, None, :])
        w_lat = jnp.concatenate([w_cq, w_ckv, kr_slot(w_kr), kr_slot(_rotate_half_cols(w_kr))],
                                axis=1).astype(BF16)
        uq = w_uq[l].reshape(Q_LORA_RANK, MLA_HEADS, MLA_QK_DIM)
        uq_nope, uq_rope = uq[..., :MLA_NOPE_DIM], uq[..., MLA_NOPE_DIM:]
        w_uq_lin = _head_slots(uq_nope, uq_rope).astype(BF16)
        w_uq_rot = _head_slots(None, _rotate_half_cols(uq_rope)).astype(BF16)
        ukv = w_ukv[l].reshape(KV_LORA_RANK, MLA_HEADS, MLA_NOPE_DIM + MLA_V_DIM)
        w_uk = _head_slots(ukv[..., :MLA_NOPE_DIM], None).astype(BF16)
        w_uv = ukv[..., MLA_NOPE_DIM:].reshape(KV_LORA_RANK, MLA_WIDTH).astype(BF16)

        head_major = lambda width: jax.ShapeDtypeStruct((b, MLA_HEADS, s, width), BF16)
        tile = PROJ_BLOCKS * blk
        head_spec = lambda width: pl.BlockSpec((1, MLA_HEADS, tile, width), lambda bi, ti: (bi, 0, ti, 0))
        vt_shape = lambda rows: jax.ShapeDtypeStruct((b, MLA_HEADS, nb, rows, blk), BF16)
        vt_spec = lambda rows: pl.BlockSpec((1, MLA_HEADS, PROJ_BLOCKS, rows, blk),
                                            lambda bi, ti: (bi, 0, ti, 0, 0))
        q_mla, k_mla, vt_mla, q_sb, k_sb, vt_sb = pl.pallas_call(
            _proj_kernel,
            grid=(b, nb // PROJ_BLOCKS),
            in_specs=[
                pl.BlockSpec((1, tile, d), lambda bi, ti: (bi, ti, 0)),
                pl.BlockSpec((1, tile, 1), lambda bi, ti: (bi, ti, 0)),
                _const_spec((1, MLA_PAD_DIM)),
                _const_spec((1, d)),
                _const_spec((1, Q_LORA_RANK)),
                _const_spec((1, KV_LORA_RANK)),
                _const_spec(w_lat.shape),
                _const_spec((d, 3 * SB_WIDTH)),
                _const_spec(w_uq_lin.shape),
                _const_spec(w_uq_rot.shape),
                _const_spec(w_uk.shape),
                _const_spec(w_uv.shape),
            ],
            out_specs=[head_spec(MLA_PAD_DIM), head_spec(MLA_PAD_DIM), vt_spec(MLA_VT_ROWS),
                       head_spec(SB_HEAD_DIM), head_spec(SB_HEAD_DIM), vt_spec(SB_HEAD_DIM)],
            out_shape=[head_major(MLA_PAD_DIM), head_major(MLA_PAD_DIM), vt_shape(MLA_VT_ROWS),
                       head_major(SB_HEAD_DIM), head_major(SB_HEAD_DIM), vt_shape(SB_HEAD_DIM)],
            compiler_params=params,
            name="proj",
        )(h, pos, freq, norm_mix[l][None, :], q_latent_norm[l][None, :], kv_latent_norm[l][None, :],
          w_lat, w_sb.astype(BF16), w_uq_lin, w_uq_rot, w_uk, w_uv)

        seq_spec = lambda width: pl.BlockSpec((1, 1, s, width), lambda bi, hi: (bi, hi, 0, 0))
        blocked = lambda rows: pl.BlockSpec((1, 1, nb, rows, blk), lambda bi, hi: (bi, hi, 0, 0, 0))
        o_shape = jax.ShapeDtypeStruct((b, MLA_HEADS, nb, MLA_V_DIM, blk), F32)
        score_scratch = [pltpu.VMEM((2, N_CHAINS, blk, blk), F32),
                         pltpu.VMEM((N_CHAINS, blk, blk), BF16)]
        assert MLA_HEADS == SB_HEADS
        o_mla, o_sb = pl.pallas_call(
            functools.partial(_attn_kernel, n_super=nb // N_CHAINS),
            grid=(b, MLA_HEADS),
            in_specs=[seq_spec(MLA_PAD_DIM), seq_spec(MLA_PAD_DIM), blocked(MLA_VT_ROWS),
                      seq_spec(SB_HEAD_DIM), seq_spec(SB_HEAD_DIM), blocked(SB_HEAD_DIM),
                      _const_spec((blk, blk))],
            out_specs=[blocked(MLA_V_DIM), blocked(SB_HEAD_DIM)],
            out_shape=[o_shape, o_shape],
            scratch_shapes=score_scratch,
            compiler_params=params,
            name="attn",
        )(q_mla, k_mla, vt_mla, q_sb, k_sb, vt_sb, tri)

        assert depth == 1
        o_spec = pl.BlockSpec((1, MLA_HEADS, OUT_BLOCKS, MLA_V_DIM, blk), lambda bi, ti: (bi, 0, ti, 0, 0))
        tok_spec = pl.BlockSpec((1, OUT_BLOCKS * blk, d), lambda bi, ti: (bi, ti, 0))
        resident = lambda shape: pl.BlockSpec(shape, lambda *_: (0,) * len(shape),
                                              pipeline_mode=pl.Buffered(1))
        h = pl.pallas_call(
            functools.partial(_out_kernel, d_ff=d_ff),
            grid=(b, nb // OUT_BLOCKS),
            in_specs=[
                tok_spec, o_spec, o_spec,
                _const_spec((MLA_WIDTH, 1)), _const_spec((SB_WIDTH, 1)),
                resident((MLA_WIDTH, d)), resident((SB_WIDTH, d)),
                _const_spec((1, d)),
                resident((d, d_ff)), resident((d, d_ff)), resident((d_ff, d)),
                _const_spec((1, d)),
            ],
            out_specs=tok_spec,
            out_shape=jax.ShapeDtypeStruct((b, s, d), F32),
            compiler_params=params,
            name="out_ffn",
        )(h, o_mla, o_sb, out_norm_mla[l][:, None], out_norm_sb[l][:, None],
          w_o[l][:MLA_WIDTH].astype(BF16), w_o[l][MLA_WIDTH:].astype(BF16),
          norm_ffn[l][None, :], w_gate[l].astype(BF16), w_up[l].astype(BF16),
          w_down[l].astype(BF16), norm_final[None, :])
    return h
```

```python
import functools
import math

import jax
import jax.numpy as jnp
from jax import lax
from jax.experimental import pallas as pl
from jax.experimental.pallas import tpu as pltpu

EPS = 1e-6
ROPE_THETA = 10000.0
LOG2_E = 1.4426950408889634

MLA_HEADS = 8
MLA_NOPE_DIM = 64
MLA_ROPE_DIM = 32
MLA_V_DIM = 64
MLA_QK_DIM = MLA_NOPE_DIM + MLA_ROPE_DIM
MLA_PAD_DIM = 128
MLA_VT_ROWS = MLA_V_DIM + 16
Q_LORA_RANK = 256
KV_LORA_RANK = 128
SB_HEADS = 8
SB_HEAD_DIM = 64
MLA_WIDTH = MLA_HEADS * MLA_V_DIM
SB_WIDTH = SB_HEADS * SB_HEAD_DIM

SEQ_BLOCK = 256
N_CHAINS = 4
MLA_K_PER_STEP = 4
SB_DEAD_BITS = 160.0
FF_CHUNK = 256
OUT_BLOCKS = 2
PROJ_BLOCKS = 4
VMEM_LIMIT_BYTES = 56 * 1024 * 1024

F32 = jnp.float32
BF16 = jnp.bfloat16


def _rms_scale(v, axis):
    return lax.rsqrt(jnp.mean(v * v, axis=axis, keepdims=True) + EPS)


def _dot(a, b):
    return jnp.dot(a, b, preferred_element_type=F32)


def _dot_nt(a, b):
    return lax.dot_general(a, b, (((1,), (1,)), ((), ())), preferred_element_type=F32)


def _dot_tn(a, b):
    return lax.dot_general(a, b, (((0,), (0,)), ((), ())), preferred_element_type=F32)


def _proj_kernel(x_ref, pos_ref, freq_ref, spread_ref, off_rope_ref, g_mix_ref, g_q_ref, g_kv_ref,
                 w_lat_ref, w_sb_ref, w_uq_ref, w_uq_rot_ref, w_uk_ref, w_uv_ref,
                 q_mla_ref, k_mla_ref, vt_mla_ref, q_sb_ref, k_sb_ref, vt_sb_ref):
    x = x_ref[0]
    u = (x * _rms_scale(x, -1) * g_mix_ref[...]).astype(BF16)

    ang_t = freq_ref[...] * pos_ref[0].astype(F32)
    cos_t = jnp.cos(ang_t)
    sin_t = jnp.sin(ang_t)

    lat = _dot(u, w_lat_ref[...])
    sb = _dot(u, w_sb_ref[...])
    def spread(t):
        t1 = t.astype(BF16)
        r1 = t - t1.astype(F32)
        t2 = r1.astype(BF16)
        t3 = (r1 - t2.astype(F32)).astype(BF16)
        return _dot_tn(jnp.concatenate([t1, t2, t3], axis=0), spread_ref[...])

    cos = spread(cos_t) + off_rope_ref[...]
    sin = spread(sin_t)
    c_q = lat[:, :Q_LORA_RANK]
    c_kv = lat[:, Q_LORA_RANK:Q_LORA_RANK + KV_LORA_RANK]
    k_r = lat[:, Q_LORA_RANK + KV_LORA_RANK:Q_LORA_RANK + KV_LORA_RANK + MLA_PAD_DIM]
    k_r_rot = lat[:, Q_LORA_RANK + KV_LORA_RANK + MLA_PAD_DIM:]
    k_rope = k_r * cos + k_r_rot * sin

    ql = (c_q * _rms_scale(c_q, -1) * g_q_ref[...]).astype(BF16)
    kvl = (c_kv * _rms_scale(c_kv, -1) * g_kv_ref[...]).astype(BF16)

    q_lin = _dot(ql, w_uq_ref[...])
    q_rot = _dot(ql, w_uq_rot_ref[...])
    k_nope = _dot(kvl, w_uk_ref[...])
    v_mla = _dot(kvl, w_uv_ref[...])
    q_scale = LOG2_E / math.sqrt(MLA_QK_DIM)
    for h in range(MLA_HEADS):
        sl = slice(h * MLA_PAD_DIM, (h + 1) * MLA_PAD_DIM)
        q_h = (q_lin[:, sl] * cos + q_rot[:, sl] * sin) * q_scale
        q_mla_ref[0, h] = q_h.astype(BF16)
        k_mla_ref[0, h] = (k_nope[:, sl] + k_rope).astype(BF16)
    v_mla_t = v_mla.T
    pad_row = lax.broadcasted_iota(jnp.int32, (MLA_VT_ROWS - MLA_V_DIM, SEQ_BLOCK), 0)
    ones_then_zeros = jnp.where(pad_row == 0, 1.0, 0.0).astype(BF16)
    for h in range(MLA_HEADS):
        for t in range(PROJ_BLOCKS):
            vt_mla_ref[0, h, t, :MLA_V_DIM, :] = v_mla_t[h * MLA_V_DIM:(h + 1) * MLA_V_DIM,
                                                         t * SEQ_BLOCK:(t + 1) * SEQ_BLOCK].astype(BF16)
            vt_mla_ref[0, h, t, MLA_V_DIM:, :] = ones_then_zeros

    sb_scale = LOG2_E / math.sqrt(SB_HEAD_DIM)
    q_sb = sb[:, :SB_WIDTH] * sb_scale
    k_sb = sb[:, SB_WIDTH:2 * SB_WIDTH]
    v_sb_t = sb[:, 2 * SB_WIDTH:].T
    for h in range(SB_HEADS):
        sl = slice(h * SB_HEAD_DIM, (h + 1) * SB_HEAD_DIM)
        q_sb_ref[0, h] = q_sb[:, sl].astype(BF16)
        k_sb_ref[0, h] = k_sb[:, sl].astype(BF16)
        for t in range(PROJ_BLOCKS):
            vt_sb_ref[0, h, t] = v_sb_t[sl, t * SEQ_BLOCK:(t + 1) * SEQ_BLOCK].astype(BF16)


def _block_iotas():
    key_idx = lax.broadcasted_iota(jnp.int32, (SEQ_BLOCK, SEQ_BLOCK), 0)
    qry_idx = lax.broadcasted_iota(jnp.int32, (SEQ_BLOCK, SEQ_BLOCK), 1)
    return key_idx, qry_idx


def _attn_kernel(q_ref, k_ref, vt_ref, q_sb_ref, k_sb_ref, vt_sb_ref, tri_ref, o_ref, o_sb_ref,
                 z_scr, w_scr, *, n_super):
    blk, n, k_per_step = SEQ_BLOCK, N_CHAINS, MLA_K_PER_STEP
    key_idx, qry_idx = _block_iotas()
    causal = key_idx <= qry_idx
    unit = jnp.ones((1, blk), F32)
    sb_stages, sb_tail = _sb_super_block(q_sb_ref, k_sb_ref, vt_sb_ref, tri_ref, o_sb_ref)

    def rows(ref, j):
        return ref[0, 0, pl.ds(pl.multiple_of(j * blk, blk), blk), :]

    def softmax_blocks(zs, ms, diag_chain):
        new_ms, ws, alphas = [], [], []
        for i, (s, m) in enumerate(zip(zs, ms)):
            if i == diag_chain:
                s = jnp.where(causal, s, -jnp.inf)
            s_max = jnp.max(s, axis=0, keepdims=True)
            if m is None:
                m_new, alpha = s_max, None
            else:
                m_new = jnp.maximum(m, s_max)
                alpha = jnp.exp2(m - m_new)
            new_ms.append(m_new)
            ws.append(jnp.exp2(s - m_new).astype(BF16))
            alphas.append(alpha)
        return new_ms, ws, alphas

    def rescale_add(acc, pv, alpha):
        return pv if acc is None else alpha * acc + pv

    def super_body(sb, carry):
        base = sb * n

        def scores(j, chains):
            k = rows(k_ref, j)
            return [_dot_nt(k, rows(q_ref, base + a)) for a in chains]

        ms, accs = [None] * n, [None] * n
        side = sb_stages(base)
        side.pop(0)()
        zs = scores(base + n - 1, [n - 1])
        pending = None
        for kb in reversed(range(n)):
            chains = list(range(kb, n))
            if kb > 0:
                zs_next = scores(base + kb - 1, range(kb - 1, n))
            else:
                for a, z in enumerate(scores(jnp.maximum(base - 1, 0), range(n))):
                    z_scr[0, a] = z
            if pending is not None:
                vt = vt_ref[0, 0, base + kb + 1]
                for a, w, alpha in zip(*pending):
                    accs[a] = rescale_add(accs[a], _dot(vt, w), alpha)
            new_ms, ws, alphas = softmax_blocks(zs, [ms[a] for a in chains], 0)
            for a, m in zip(chains, new_ms):
                ms[a] = m
            side.pop(0)()
            if kb > 0:
                pending = (chains, ws, alphas)
                zs = zs_next
            else:
                for a in range(n):
                    w_scr[a] = ws[a]
                accs[0] = jnp.zeros((vt_ref.shape[3], blk), F32)
                alphas[0] = unit

        def k_body(it, state):
            ms, accs, alphas = (list(t) for t in state)
            j = base - 1 - k_per_step * it
            for u in range(k_per_step):
                slot, j_cur, j_next = u % 2, j - u, jnp.maximum(j - u - 1, 0)
                k_next, vt_prev = rows(k_ref, j_next), vt_ref[0, 0, j_cur + 1]
                for a in range(n):
                    z_scr[1 - slot, a] = _dot_nt(k_next, rows(q_ref, base + a))
                    accs[a] = rescale_add(accs[a], _dot(vt_prev, w_scr[a]), alphas[a])
                ms, ws, alphas = softmax_blocks([z_scr[slot, a] for a in range(n)], ms, None)
                for a in range(n):
                    w_scr[a] = ws[a]
            return tuple(ms), tuple(accs), tuple(alphas)

        ms, accs, alphas = lax.fori_loop(
            0, sb * (n // k_per_step), k_body, (tuple(ms), tuple(accs), tuple(alphas)))
        vt = vt_ref[0, 0, 0]
        accs = [rescale_add(acc, _dot(vt, w_scr[a]), alpha)
                for a, (acc, alpha) in enumerate(zip(accs, alphas))]
        for a in range(n):
            o_ref[0, 0, base + a] = accs[a][:MLA_V_DIM] / accs[a][MLA_V_DIM:MLA_V_DIM + 1]
        sb_tail(base)
        return carry

    lax.fori_loop(0, n_super, super_body, 0)


def _softplus2(z):
    return jnp.maximum(z, 0.0) + jnp.log2(1.0 + jnp.exp2(-jnp.abs(z)))


def _sb_super_block(q_ref, k_ref, vt_ref, tri_ref, o_ref):
    blk, n = SEQ_BLOCK, N_CHAINS
    key_idx, qry_idx = _block_iotas()
    strict = key_idx < qry_idx

    def rows(ref, j):
        return ref[0, 0, pl.ds(pl.multiple_of(j * blk, blk), blk), :]

    def logits(base, specs):
        return [_dot_nt(rows(k_ref, j), rows(q_ref, base + a)) for a, j, _ in specs]

    def suffix_sums(zs, specs):
        log_betas, sps = [], []
        for z, (_, _, diag) in zip(zs, specs):
            sp = _softplus2(z)
            log_betas.append(z - sp)
            if diag:
                sp = jnp.where(strict, sp, 0.0)
            sps.append(sp.astype(BF16))
        return log_betas, sps, [_dot(tri_ref[...], sp) for sp in sps]

    def products(log_betas, sps, sufs, specs):
        ws = []
        for log_beta, suf, (_, _, diag) in zip(log_betas, sufs, specs):
            w = jnp.exp2(log_beta + suf)
            if diag:
                w = jnp.where(strict, w, 0.0)
            ws.append(w.astype(BF16))
        return [(suf[0:1, :] - sp[0:1, :].astype(F32), _dot(vt_ref[0, 0, j], w))
                for w, sp, suf, (_, j, _) in zip(ws, sps, sufs, specs)]

    def tiles(base, groups):
        zs = [logits(base, g) for g in groups]
        out, prev = [], None
        for g, z in zip(groups, zs):
            cur = suffix_sums(z, g) + (g,)
            if prev is not None:
                out += products(*prev)
            prev = cur
        return out + products(*prev)

    def band(base, d):
        return [(a, jnp.maximum(base + a - d, 0), False) for a in range(n)]

    def factor(base, a, d, c):
        return jnp.where(base + a - d >= 0, jnp.exp2(c), 0.0)

    def live(cs):
        return (jnp.max(functools.reduce(jnp.maximum, cs)) > -SB_DEAD_BITS).astype(jnp.int32)

    carry_sums = {}

    def stages(base):
        groups = [[(a, base + a, True) for a in range(n)], band(base, 1)]
        box = {}

        def s0():
            box["z"] = [logits(base, g) for g in groups]

        def s1():
            box["c0"] = suffix_sums(box["z"][0], groups[0])

        def s2():
            box["c1"] = suffix_sums(box["z"][1], groups[1])
            box["r0"] = products(*box["c0"], groups[0])

        def s3():
            box["r1"] = products(*box["c1"], groups[1])

        def s4():
            cs = []
            for a in range(n):
                (sum0, pv0), (sum1, pv1) = box["r0"][a], box["r1"][a]
                o_ref[0, 0, base + a] = pv0 + pv1 * factor(base, a, 1, sum0)
                cs.append(sum0 + sum1)
            carry_sums[0] = cs

        return [s0, s1, s2, s3, s4]

    def tail(base):
        def w_body(state):
            d, _, cs = state
            res = tiles(base, [band(base, d)])
            for a, (c, (_, pv)) in enumerate(zip(cs, res)):
                o_ref[0, 0, base + a] = o_ref[0, 0, base + a] + pv * factor(base, a, d, c)
            cs = tuple(c + block_sum for c, (block_sum, _) in zip(cs, res))
            return d + 1, live(cs), cs

        cs = carry_sums[0]
        lax.while_loop(lambda state: jnp.logical_and(state[0] < base + n, state[1] > 0), w_body,
                       (jnp.int32(2), live(cs), tuple(cs)))

    return stages, tail


def _out_kernel(x_ref, o_mla_ref, o_sb_ref, g_mla_ref, g_sb_ref, w_o_mla_ref, w_o_sb_ref,
                g_ffn_ref, w_gate_ref, w_up_ref, w_down_ref, g_final_ref, out_ref, *, d_ff):
    blk = SEQ_BLOCK

    def group(o_ref, g_ref, w_ref):
        parts = []
        for t in range(OUT_BLOCKS):
            o = o_ref[0, :, t].reshape(-1, blk)
            y = (o * _rms_scale(o, 0) * g_ref[...]).astype(BF16)
            parts.append(_dot_tn(y, w_ref[...]))
        return jnp.concatenate(parts, axis=0)

    h = x_ref[0] + group(o_mla_ref, g_mla_ref, w_o_mla_ref) + group(o_sb_ref, g_sb_ref, w_o_sb_ref)
    f = (h * _rms_scale(h, -1) * g_ffn_ref[...]).astype(BF16)
    ffn = jnp.zeros_like(h)
    for c in range(0, d_ff, FF_CHUNK):
        gate = _dot(f, w_gate_ref[:, c:c + FF_CHUNK])
        up = _dot(f, w_up_ref[:, c:c + FF_CHUNK])
        act = (gate * jax.nn.sigmoid(gate) * up).astype(BF16)
        ffn = ffn + _dot(act, w_down_ref[c:c + FF_CHUNK, :])
    h = h + ffn
    out_ref[0] = h * _rms_scale(h, -1) * g_final_ref[...]


def _rotate_half_cols(w):
    half = w.shape[-1] // 2
    return jnp.concatenate([-w[..., half:], w[..., :half]], axis=-1)


def _head_slots(nope, rope):
    ref = nope if nope is not None else rope
    r, h = ref.shape[0], ref.shape[1]
    nope = jnp.zeros((r, h, MLA_NOPE_DIM), ref.dtype) if nope is None else nope
    rope = jnp.zeros((r, h, MLA_ROPE_DIM), ref.dtype) if rope is None else rope
    pad = jnp.zeros((r, h, MLA_PAD_DIM - MLA_QK_DIM), ref.dtype)
    return jnp.concatenate([nope, rope, pad], axis=-1).reshape(r, h * MLA_PAD_DIM)


def _const_spec(shape):
    return pl.BlockSpec(shape, lambda *_: (0,) * len(shape))


def kernel(x, positions, norm_mix, w_in, q_latent_norm, w_uq, kv_latent_norm, w_ukv,
           out_norm_mla, out_norm_sb, w_o, norm_ffn, w_gate, w_up, w_down, norm_final):
    b, s, d = x.shape
    depth = w_in.shape[0]
    d_ff = w_gate.shape[-1]
    blk = SEQ_BLOCK
    nb = s // blk
    assert s % (blk * N_CHAINS) == 0 and N_CHAINS % 2 == 0 and d_ff % FF_CHUNK == 0

    inv_freq = ROPE_THETA ** (-jnp.arange(0, MLA_ROPE_DIM, 2, dtype=F32) / MLA_ROPE_DIM)
    half = MLA_ROPE_DIM // 2
    lane = jnp.arange(MLA_PAD_DIM)[None, :]
    on_rope = (lane >= MLA_NOPE_DIM) & (lane < MLA_QK_DIM)
    spread = (on_rope & ((lane - MLA_NOPE_DIM) % half == jnp.arange(half)[:, None])).astype(BF16)
    spread = jnp.tile(spread, (3, 1))
    off_rope = (~on_rope).astype(F32)
    tri = jnp.where(jnp.arange(blk)[None, :] > jnp.arange(blk)[:, None], -1.0, 0.0).astype(BF16)
    pos = positions.reshape(b, 1, s)

    params = pltpu.CompilerParams(
        dimension_semantics=("arbitrary", "arbitrary"), vmem_limit_bytes=VMEM_LIMIT_BYTES)

    h = x
    for l in range(depth):
        o0 = Q_LORA_RANK
        o1 = o0 + KV_LORA_RANK
        o2 = o1 + MLA_ROPE_DIM
        w_cq, w_ckv, w_kr, w_sb = w_in[l][:, :o0], w_in[l][:, o0:o1], w_in[l][:, o1:o2], w_in[l][:, o2:]
        kr_slot = lambda w: _head_slots(None, w[:, None, :])
        w_lat = jnp.concatenate([w_cq, w_ckv, kr_slot(w_kr), kr_slot(_rotate_half_cols(w_kr))],
                                axis=1).astype(BF16)
        uq = w_uq[l].reshape(Q_LORA_RANK, MLA_HEADS, MLA_QK_DIM)
        uq_nope, uq_rope = uq[..., :MLA_NOPE_DIM], uq[..., MLA_NOPE_DIM:]
        w_uq_lin = _head_slots(uq_nope, uq_rope).astype(BF16)
        w_uq_rot = _head_slots(None, _rotate_half_cols(uq_rope)).astype(BF16)
        ukv = w_ukv[l].reshape(KV_LORA_RANK, MLA_HEADS, MLA_NOPE_DIM + MLA_V_DIM)
        w_uk = _head_slots(ukv[..., :MLA_NOPE_DIM], None).astype(BF16)
        w_uv = ukv[..., MLA_NOPE_DIM:].reshape(KV_LORA_RANK, MLA_WIDTH).astype(BF16)

        head_major = lambda width: jax.ShapeDtypeStruct((b, MLA_HEADS, s, width), BF16)
        tile = PROJ_BLOCKS * blk
        head_spec = lambda width: pl.BlockSpec((1, MLA_HEADS, tile, width), lambda bi, ti: (bi, 0, ti, 0))
        vt_shape = lambda rows: jax.ShapeDtypeStruct((b, MLA_HEADS, nb, rows, blk), BF16)
        vt_spec = lambda rows: pl.BlockSpec((1, MLA_HEADS, PROJ_BLOCKS, rows, blk),
                                            lambda bi, ti: (bi, 0, ti, 0, 0))
        q_mla, k_mla, vt_mla, q_sb, k_sb, vt_sb = pl.pallas_call(
            _proj_kernel,
            grid=(b, nb // PROJ_BLOCKS),
            in_specs=[
                pl.BlockSpec((1, tile, d), lambda bi, ti: (bi, ti, 0)),
                pl.BlockSpec((1, 1, tile), lambda bi, ti: (bi, 0, ti)),
                _const_spec((half, 1)),
                _const_spec((3 * half, MLA_PAD_DIM)),
                _const_spec((1, MLA_PAD_DIM)),
                _const_spec((1, d)),
                _const_spec((1, Q_LORA_RANK)),
                _const_spec((1, KV_LORA_RANK)),
                _const_spec(w_lat.shape),
                _const_spec((d, 3 * SB_WIDTH)),
                _const_spec(w_uq_lin.shape),
                _const_spec(w_uq_rot.shape),
                _const_spec(w_uk.shape),
                _const_spec(w_uv.shape),
            ],
            out_specs=[head_spec(MLA_PAD_DIM), head_spec(MLA_PAD_DIM), vt_spec(MLA_VT_ROWS),
                       head_spec(SB_HEAD_DIM), head_spec(SB_HEAD_DIM), vt_spec(SB_HEAD_DIM)],
            out_shape=[head_major(MLA_PAD_DIM), head_major(MLA_PAD_DIM), vt_shape(MLA_VT_ROWS),
                       head_major(SB_HEAD_DIM), head_major(SB_HEAD_DIM), vt_shape(SB_HEAD_DIM)],
            compiler_params=params,
            name="proj",
        )(h, pos, inv_freq[:, None], spread, off_rope,
          norm_mix[l][None, :], q_latent_norm[l][None, :], kv_latent_norm[l][None, :],
          w_lat, w_sb.astype(BF16), w_uq_lin, w_uq_rot, w_uk, w_uv)

        seq_spec = lambda width: pl.BlockSpec((1, 1, s, width), lambda bi, hi: (bi, hi, 0, 0))
        blocked = lambda rows: pl.BlockSpec((1, 1, nb, rows, blk), lambda bi, hi: (bi, hi, 0, 0, 0))
        o_shape = jax.ShapeDtypeStruct((b, MLA_HEADS, nb, MLA_V_DIM, blk), F32)
        score_scratch = [pltpu.VMEM((2, N_CHAINS, blk, blk), F32),
                         pltpu.VMEM((N_CHAINS, blk, blk), BF16)]
        assert MLA_HEADS == SB_HEADS
        o_mla, o_sb = pl.pallas_call(
            functools.partial(_attn_kernel, n_super=nb // N_CHAINS),
            grid=(b, MLA_HEADS),
            in_specs=[seq_spec(MLA_PAD_DIM), seq_spec(MLA_PAD_DIM), blocked(MLA_VT_ROWS),
                      seq_spec(SB_HEAD_DIM), seq_spec(SB_HEAD_DIM), blocked(SB_HEAD_DIM),
                      _const_spec((blk, blk))],
            out_specs=[blocked(MLA_V_DIM), blocked(SB_HEAD_DIM)],
            out_shape=[o_shape, o_shape],
            scratch_shapes=score_scratch,
            compiler_params=params,
            name="attn",
        )(q_mla, k_mla, vt_mla, q_sb, k_sb, vt_sb, tri)

        assert depth == 1
        o_spec = pl.BlockSpec((1, MLA_HEADS, OUT_BLOCKS, MLA_V_DIM, blk), lambda bi, ti: (bi, 0, ti, 0, 0))
        tok_spec = pl.BlockSpec((1, OUT_BLOCKS * blk, d), lambda bi, ti: (bi, ti, 0))
        resident = lambda shape: pl.BlockSpec(shape, lambda *_: (0,) * len(shape),
                                              pipeline_mode=pl.Buffered(1))
        h = pl.pallas_call(
            functools.partial(_out_kernel, d_ff=d_ff),
            grid=(b, nb // OUT_BLOCKS),
            in_specs=[
                tok_spec, o_spec, o_spec,
                _const_spec((MLA_WIDTH, 1)), _const_spec((SB_WIDTH, 1)),
                resident((MLA_WIDTH, d)), resident((SB_WIDTH, d)),
                _const_spec((1, d)),
                resident((d, d_ff)), resident((d, d_ff)), resident((d_ff, d)),
                _const_spec((1, d)),
            ],
            out_specs=tok_spec,
            out_shape=jax.ShapeDtypeStruct((b, s, d), F32),
            compiler_params=params,
            name="out_ffn",
        )(h, o_mla, o_sb, out_norm_mla[l][:, None], out_norm_sb[l][:, None],
          w_o[l][:MLA_WIDTH].astype(BF16), w_o[l][MLA_WIDTH:].astype(BF16),
          norm_ffn[l][None, :], w_gate[l].astype(BF16), w_up[l].astype(BF16),
          w_down[l].astype(BF16), norm_final[None, :])
    return h
```

```python
import functools
import math

import jax
import jax.numpy as jnp
from jax import lax
from jax.experimental import pallas as pl
from jax.experimental.pallas import tpu as pltpu

EPS = 1e-6
ROPE_THETA = 10000.0
LOG2_E = 1.4426950408889634

LANES = 128
BF16_SUBLANE_TILE = 16
MXU_DIM_V7X = 256
VMEM_BYTES_V7X = 64 * 1024 * 1024

MLA_HEADS = 8
MLA_NOPE_DIM = 64
MLA_ROPE_DIM = 32
MLA_V_DIM = 64
MLA_QK_DIM = MLA_NOPE_DIM + MLA_ROPE_DIM
MLA_PAD_DIM = LANES
MLA_VT_ROWS = MLA_V_DIM + BF16_SUBLANE_TILE
Q_LORA_RANK = 256
KV_LORA_RANK = 128
SB_HEADS = 8
SB_HEAD_DIM = 64
MLA_WIDTH = MLA_HEADS * MLA_V_DIM
SB_WIDTH = SB_HEADS * SB_HEAD_DIM

SEQ_BLOCK = MXU_DIM_V7X
N_CHAINS = 4
MLA_K_PER_STEP = 4
SB_DEAD_BITS = 160.0
FF_CHUNK = MXU_DIM_V7X
OUT_BLOCKS = 2
PROJ_BLOCKS = 4
VMEM_LIMIT_BYTES = VMEM_BYTES_V7X * 7 // 8

F32 = jnp.float32
BF16 = jnp.bfloat16


def _rms_scale(v, axis):
    return lax.rsqrt(jnp.mean(v * v, axis=axis, keepdims=True) + EPS)


def _dot(a, b):
    return jnp.dot(a, b, preferred_element_type=F32)


def _dot_nt(a, b):
    return lax.dot_general(a, b, (((1,), (1,)), ((), ())), preferred_element_type=F32)


def _dot_tn(a, b):
    return lax.dot_general(a, b, (((0,), (0,)), ((), ())), preferred_element_type=F32)


def _proj_kernel(x_ref, pos_ref, freq_ref, spread_ref, off_rope_ref, g_mix_ref, g_q_ref, g_kv_ref,
                 w_lat_ref, w_sb_ref, w_uq_ref, w_uq_rot_ref, w_uk_ref, w_uv_ref,
                 q_mla_ref, k_mla_ref, vt_mla_ref, q_sb_ref, k_sb_ref, vt_sb_ref):
    x = x_ref[0]
    u = (x * _rms_scale(x, -1) * g_mix_ref[...]).astype(BF16)

    ang_t = freq_ref[...] * pos_ref[0].astype(F32)
    cos_t = jnp.cos(ang_t)
    sin_t = jnp.sin(ang_t)

    lat = _dot(u, w_lat_ref[...])
    sb = _dot(u, w_sb_ref[...])
    def spread(t):
        t1 = t.astype(BF16)
        r1 = t - t1.astype(F32)
        t2 = r1.astype(BF16)
        t3 = (r1 - t2.astype(F32)).astype(BF16)
        return _dot_tn(jnp.concatenate([t1, t2, t3], axis=0), spread_ref[...])

    cos = spread(cos_t) + off_rope_ref[...]
    sin = spread(sin_t)
    c_q = lat[:, :Q_LORA_RANK]
    c_kv = lat[:, Q_LORA_RANK:Q_LORA_RANK + KV_LORA_RANK]
    k_r = lat[:, Q_LORA_RANK + KV_LORA_RANK:Q_LORA_RANK + KV_LORA_RANK + MLA_PAD_DIM]
    k_r_rot = lat[:, Q_LORA_RANK + KV_LORA_RANK + MLA_PAD_DIM:]
    k_rope = k_r * cos + k_r_rot * sin

    ql = (c_q * _rms_scale(c_q, -1) * g_q_ref[...]).astype(BF16)
    kvl = (c_kv * _rms_scale(c_kv, -1) * g_kv_ref[...]).astype(BF16)

    q_lin = _dot(ql, w_uq_ref[...])
    q_rot = _dot(ql, w_uq_rot_ref[...])
    k_nope = _dot(kvl, w_uk_ref[...])
    v_mla = _dot(kvl, w_uv_ref[...])
    q_scale = LOG2_E / math.sqrt(MLA_QK_DIM)
    for h in range(MLA_HEADS):
        sl = slice(h * MLA_PAD_DIM, (h + 1) * MLA_PAD_DIM)
        q_h = (q_lin[:, sl] * cos + q_rot[:, sl] * sin) * q_scale
        q_mla_ref[0, h] = q_h.astype(BF16)
        k_mla_ref[0, h] = (k_nope[:, sl] + k_rope).astype(BF16)
    v_mla_t = v_mla.T
    pad_row = lax.broadcasted_iota(jnp.int32, (MLA_VT_ROWS - MLA_V_DIM, SEQ_BLOCK), 0)
    ones_then_zeros = jnp.where(pad_row == 0, 1.0, 0.0).astype(BF16)
    for h in range(MLA_HEADS):
        for t in range(PROJ_BLOCKS):
            vt_mla_ref[0, h, t, :MLA_V_DIM, :] = v_mla_t[h * MLA_V_DIM:(h + 1) * MLA_V_DIM,
                                                         t * SEQ_BLOCK:(t + 1) * SEQ_BLOCK].astype(BF16)
            vt_mla_ref[0, h, t, MLA_V_DIM:, :] = ones_then_zeros

    sb_scale = LOG2_E / math.sqrt(SB_HEAD_DIM)
    q_sb = sb[:, :SB_WIDTH] * sb_scale
    k_sb = sb[:, SB_WIDTH:2 * SB_WIDTH]
    v_sb_t = sb[:, 2 * SB_WIDTH:].T
    for h in range(SB_HEADS):
        sl = slice(h * SB_HEAD_DIM, (h + 1) * SB_HEAD_DIM)
        q_sb_ref[0, h] = q_sb[:, sl].astype(BF16)
        k_sb_ref[0, h] = k_sb[:, sl].astype(BF16)
        for t in range(PROJ_BLOCKS):
            vt_sb_ref[0, h, t] = v_sb_t[sl, t * SEQ_BLOCK:(t + 1) * SEQ_BLOCK].astype(BF16)


def _block_iotas():
    key_idx = lax.broadcasted_iota(jnp.int32, (SEQ_BLOCK, SEQ_BLOCK), 0)
    qry_idx = lax.broadcasted_iota(jnp.int32, (SEQ_BLOCK, SEQ_BLOCK), 1)
    return key_idx, qry_idx


def _attn_kernel(q_ref, k_ref, vt_ref, q_sb_ref, k_sb_ref, vt_sb_ref, tri_ref, o_ref, o_sb_ref,
                 z_scr, w_scr, *, n_super):
    blk, n, k_per_step = SEQ_BLOCK, N_CHAINS, MLA_K_PER_STEP
    key_idx, qry_idx = _block_iotas()
    causal = key_idx <= qry_idx
    unit = jnp.ones((1, blk), F32)
    sb_stages, sb_tail = _sb_super_block(q_sb_ref, k_sb_ref, vt_sb_ref, tri_ref, o_sb_ref)

    def rows(ref, j):
        return ref[0, 0, pl.ds(pl.multiple_of(j * blk, blk), blk), :]

    def softmax_blocks(zs, ms, diag_chain):
        new_ms, ws, alphas = [], [], []
        for i, (s, m) in enumerate(zip(zs, ms)):
            if i == diag_chain:
                s = jnp.where(causal, s, -jnp.inf)
            s_max = jnp.max(s, axis=0, keepdims=True)
            if m is None:
                m_new, alpha = s_max, None
            else:
                m_new = jnp.maximum(m, s_max)
                alpha = jnp.exp2(m - m_new)
            new_ms.append(m_new)
            ws.append(jnp.exp2(s - m_new).astype(BF16))
            alphas.append(alpha)
        return new_ms, ws, alphas

    def rescale_add(acc, pv, alpha):
        return pv if acc is None else alpha * acc + pv

    def super_body(sb, carry):
        base = sb * n

        def scores(j, chains):
            k = rows(k_ref, j)
            return [_dot_nt(k, rows(q_ref, base + a)) for a in chains]

        ms, accs = [None] * n, [None] * n
        side, sb_box = sb_stages(base)
        assert len(side) == n + 1
        side.pop(0)()
        zs = scores(base + n - 1, [n - 1])
        pending = None
        for kb in reversed(range(n)):
            chains = list(range(kb, n))
            if kb > 0:
                zs_next = scores(base + kb - 1, range(kb - 1, n))
            else:
                for a, z in enumerate(scores(jnp.maximum(base - 1, 0), range(n))):
                    z_scr[0, a] = z
            if pending is not None:
                vt = vt_ref[0, 0, base + kb + 1]
                for a, w, alpha in zip(*pending):
                    accs[a] = rescale_add(accs[a], _dot(vt, w), alpha)
            new_ms, ws, alphas = softmax_blocks(zs, [ms[a] for a in chains], 0)
            for a, m in zip(chains, new_ms):
                ms[a] = m
            side.pop(0)()
            if kb > 0:
                pending = (chains, ws, alphas)
                zs = zs_next
            else:
                for a in range(n):
                    w_scr[a] = ws[a]
                accs[0] = jnp.zeros((vt_ref.shape[3], blk), F32)
                alphas[0] = unit

        def k_body(it, state):
            ms, accs, alphas = (list(t) for t in state)
            j = base - 1 - k_per_step * it
            for u in range(k_per_step):
                slot, j_cur, j_next = u % 2, j - u, jnp.maximum(j - u - 1, 0)
                k_next, vt_prev = rows(k_ref, j_next), vt_ref[0, 0, j_cur + 1]
                for a in range(n):
                    z_scr[1 - slot, a] = _dot_nt(k_next, rows(q_ref, base + a))
                    accs[a] = rescale_add(accs[a], _dot(vt_prev, w_scr[a]), alphas[a])
                ms, ws, alphas = softmax_blocks([z_scr[slot, a] for a in range(n)], ms, None)
                for a in range(n):
                    w_scr[a] = ws[a]
            return tuple(ms), tuple(accs), tuple(alphas)

        ms, accs, alphas = lax.fori_loop(
            0, sb * (n // k_per_step), k_body, (tuple(ms), tuple(accs), tuple(alphas)))
        vt = vt_ref[0, 0, 0]
        accs = [rescale_add(acc, _dot(vt, w_scr[a]), alpha)
                for a, (acc, alpha) in enumerate(zip(accs, alphas))]
        for a in range(n):
            o_ref[0, 0, base + a] = accs[a][:MLA_V_DIM] / accs[a][MLA_V_DIM:MLA_V_DIM + 1]
        sb_tail(base, sb_box)
        return carry

    lax.fori_loop(0, n_super, super_body, 0)


def _softplus2(z):
    return jnp.maximum(z, 0.0) + jnp.log2(1.0 + jnp.exp2(-jnp.abs(z)))


def _sb_super_block(q_ref, k_ref, vt_ref, tri_ref, o_ref):
    blk, n = SEQ_BLOCK, N_CHAINS
    key_idx, qry_idx = _block_iotas()
    strict = key_idx < qry_idx

    def rows(ref, j):
        return ref[0, 0, pl.ds(pl.multiple_of(j * blk, blk), blk), :]

    def logits(base, specs):
        return [_dot_nt(rows(k_ref, j), rows(q_ref, base + a)) for a, j, _ in specs]

    def suffix_sums(zs, specs):
        log_betas, sps = [], []
        for z, (_, _, diag) in zip(zs, specs):
            sp = _softplus2(z)
            log_betas.append(z - sp)
            if diag:
                sp = jnp.where(strict, sp, 0.0)
            sps.append(sp.astype(BF16))
        return log_betas, sps, [_dot(tri_ref[...], sp) for sp in sps]

    def products(log_betas, sps, sufs, specs):
        ws = []
        for log_beta, suf, (_, _, diag) in zip(log_betas, sufs, specs):
            w = jnp.exp2(log_beta + suf)
            if diag:
                w = jnp.where(strict, w, 0.0)
            ws.append(w.astype(BF16))
        return [(suf[0:1, :] - sp[0:1, :].astype(F32), _dot(vt_ref[0, 0, j], w))
                for w, sp, suf, (_, j, _) in zip(ws, sps, sufs, specs)]

    def tiles(base, groups):
        zs = [logits(base, g) for g in groups]
        out, prev = [], None
        for g, z in zip(groups, zs):
            cur = suffix_sums(z, g) + (g,)
            if prev is not None:
                out += products(*prev)
            prev = cur
        return out + products(*prev)

    def band(base, d):
        return [(a, jnp.maximum(base + a - d, 0), False) for a in range(n)]

    def factor(base, a, d, c):
        return jnp.where(base + a - d >= 0, jnp.exp2(c), 0.0)

    def live(cs):
        return (jnp.max(functools.reduce(jnp.maximum, cs)) > -SB_DEAD_BITS).astype(jnp.int32)

    def stages(base):
        groups = [[(a, base + a, True) for a in range(n)], band(base, 1)]
        box = {}

        def s0():
            box["z"] = [logits(base, g) for g in groups]

        def s1():
            box["c0"] = suffix_sums(box["z"][0], groups[0])

        def s2():
            box["c1"] = suffix_sums(box["z"][1], groups[1])
            box["r0"] = products(*box["c0"], groups[0])

        def s3():
            box["r1"] = products(*box["c1"], groups[1])

        def s4():
            cs = []
            for a in range(n):
                (sum0, pv0), (sum1, pv1) = box["r0"][a], box["r1"][a]
                o_ref[0, 0, base + a] = pv0 + pv1 * factor(base, a, 1, sum0)
                cs.append(sum0 + sum1)
            box["carry"] = cs

        return [s0, s1, s2, s3, s4], box

    def tail(base, box):
        def w_body(state):
            d, _, cs = state
            res = tiles(base, [band(base, d)])
            for a, (c, (_, pv)) in enumerate(zip(cs, res)):
                o_ref[0, 0, base + a] = o_ref[0, 0, base + a] + pv * factor(base, a, d, c)
            cs = tuple(c + block_sum for c, (block_sum, _) in zip(cs, res))
            return d + 1, live(cs), cs

        cs = box["carry"]
        lax.while_loop(lambda state: jnp.logical_and(state[0] < base + n, state[1] > 0), w_body,
                       (jnp.int32(2), live(cs), tuple(cs)))

    return stages, tail


def _out_kernel(x_ref, o_mla_ref, o_sb_ref, g_mla_ref, g_sb_ref, w_o_mla_ref, w_o_sb_ref,
                g_ffn_ref, w_gate_ref, w_up_ref, w_down_ref, g_final_ref, out_ref, *, d_ff):
    blk = SEQ_BLOCK

    def group(o_ref, g_ref, w_ref):
        parts = []
        for t in range(OUT_BLOCKS):
            o = o_ref[0, :, t].reshape(-1, blk)
            y = (o * _rms_scale(o, 0) * g_ref[...]).astype(BF16)
            parts.append(_dot_tn(y, w_ref[...]))
        return jnp.concatenate(parts, axis=0)

    h = x_ref[0] + group(o_mla_ref, g_mla_ref, w_o_mla_ref) + group(o_sb_ref, g_sb_ref, w_o_sb_ref)
    f = (h * _rms_scale(h, -1) * g_ffn_ref[...]).astype(BF16)
    ffn = jnp.zeros_like(h)
    for c in range(0, d_ff, FF_CHUNK):
        gate = _dot(f, w_gate_ref[:, c:c + FF_CHUNK])
        up = _dot(f, w_up_ref[:, c:c + FF_CHUNK])
        act = (gate * jax.nn.sigmoid(gate) * up).astype(BF16)
        ffn = ffn + _dot(act, w_down_ref[c:c + FF_CHUNK, :])
    h = h + ffn
    out_ref[0] = h * _rms_scale(h, -1) * g_final_ref[...]


def _rotate_half_cols(w):
    half = w.shape[-1] // 2
    return jnp.concatenate([-w[..., half:], w[..., :half]], axis=-1)


def _head_slots(nope, rope):
    ref = nope if nope is not None else rope
    r, h = ref.shape[0], ref.shape[1]
    nope = jnp.zeros((r, h, MLA_NOPE_DIM), ref.dtype) if nope is None else nope
    rope = jnp.zeros((r, h, MLA_ROPE_DIM), ref.dtype) if rope is None else rope
    pad = jnp.zeros((r, h, MLA_PAD_DIM - MLA_QK_DIM), ref.dtype)
    return jnp.concatenate([nope, rope, pad], axis=-1).reshape(r, h * MLA_PAD_DIM)


def _const_spec(shape):
    return pl.BlockSpec(shape, lambda *_: (0,) * len(shape))


def kernel(x, positions, norm_mix, w_in, q_latent_norm, w_uq, kv_latent_norm, w_ukv,
           out_norm_mla, out_norm_sb, w_o, norm_ffn, w_gate, w_up, w_down, norm_final):
    b, s, d = x.shape
    depth = w_in.shape[0]
    d_ff = w_gate.shape[-1]
    blk = SEQ_BLOCK
    nb = s // blk
    assert s % (blk * N_CHAINS) == 0 and nb % PROJ_BLOCKS == 0 and nb % OUT_BLOCKS == 0
    assert N_CHAINS % MLA_K_PER_STEP == 0 and MLA_K_PER_STEP % 2 == 0 and d_ff % FF_CHUNK == 0

    inv_freq = ROPE_THETA ** (-jnp.arange(0, MLA_ROPE_DIM, 2, dtype=F32) / MLA_ROPE_DIM)
    half = MLA_ROPE_DIM // 2
    lane = jnp.arange(MLA_PAD_DIM)[None, :]
    on_rope = (lane >= MLA_NOPE_DIM) & (lane < MLA_QK_DIM)
    spread = (on_rope & ((lane - MLA_NOPE_DIM) % half == jnp.arange(half)[:, None])).astype(BF16)
    spread = jnp.tile(spread, (3, 1))
    off_rope = (~on_rope).astype(F32)
    tri = jnp.where(jnp.arange(blk)[None, :] > jnp.arange(blk)[:, None], -1.0, 0.0).astype(BF16)
    pos = positions.reshape(b, 1, s)

    params = pltpu.CompilerParams(
        dimension_semantics=("arbitrary", "arbitrary"), vmem_limit_bytes=VMEM_LIMIT_BYTES)

    h = x
    for l in range(depth):
        o0 = Q_LORA_RANK
        o1 = o0 + KV_LORA_RANK
        o2 = o1 + MLA_ROPE_DIM
        w_cq, w_ckv, w_kr, w_sb = w_in[l][:, :o0], w_in[l][:, o0:o1], w_in[l][:, o1:o2], w_in[l][:, o2:]
        kr_slot = lambda w: _head_slots(None, w[:, None, :])
        w_lat = jnp.concatenate([w_cq, w_ckv, kr_slot(w_kr), kr_slot(_rotate_half_cols(w_kr))],
                                axis=1).astype(BF16)
        uq = w_uq[l].reshape(Q_LORA_RANK, MLA_HEADS, MLA_QK_DIM)
        uq_nope, uq_rope = uq[..., :MLA_NOPE_DIM], uq[..., MLA_NOPE_DIM:]
        w_uq_lin = _head_slots(uq_nope, uq_rope).astype(BF16)
        w_uq_rot = _head_slots(None, _rotate_half_cols(uq_rope)).astype(BF16)
        ukv = w_ukv[l].reshape(KV_LORA_RANK, MLA_HEADS, MLA_NOPE_DIM + MLA_V_DIM)
        w_uk = _head_slots(ukv[..., :MLA_NOPE_DIM], None).astype(BF16)
        w_uv = ukv[..., MLA_NOPE_DIM:].reshape(KV_LORA_RANK, MLA_WIDTH).astype(BF16)

        head_major = lambda width: jax.ShapeDtypeStruct((b, MLA_HEADS, s, width), BF16)
        tile = PROJ_BLOCKS * blk
        head_spec = lambda width: pl.BlockSpec((1, MLA_HEADS, tile, width), lambda bi, ti: (bi, 0, ti, 0))
        vt_shape = lambda rows: jax.ShapeDtypeStruct((b, MLA_HEADS, nb, rows, blk), BF16)
        vt_spec = lambda rows: pl.BlockSpec((1, MLA_HEADS, PROJ_BLOCKS, rows, blk),
                                            lambda bi, ti: (bi, 0, ti, 0, 0))
        q_mla, k_mla, vt_mla, q_sb, k_sb, vt_sb = pl.pallas_call(
            _proj_kernel,
            grid=(b, nb // PROJ_BLOCKS),
            in_specs=[
                pl.BlockSpec((1, tile, d), lambda bi, ti: (bi, ti, 0)),
                pl.BlockSpec((1, 1, tile), lambda bi, ti: (bi, 0, ti)),
                _const_spec((half, 1)),
                _const_spec((3 * half, MLA_PAD_DIM)),
                _const_spec((1, MLA_PAD_DIM)),
                _const_spec((1, d)),
                _const_spec((1, Q_LORA_RANK)),
                _const_spec((1, KV_LORA_RANK)),
                _const_spec(w_lat.shape),
                _const_spec((d, 3 * SB_WIDTH)),
                _const_spec(w_uq_lin.shape),
                _const_spec(w_uq_rot.shape),
                _const_spec(w_uk.shape),
                _const_spec(w_uv.shape),
            ],
            out_specs=[head_spec(MLA_PAD_DIM), head_spec(MLA_PAD_DIM), vt_spec(MLA_VT_ROWS),
                       head_spec(SB_HEAD_DIM), head_spec(SB_HEAD_DIM), vt_spec(SB_HEAD_DIM)],
            out_shape=[head_major(MLA_PAD_DIM), head_major(MLA_PAD_DIM), vt_shape(MLA_VT_ROWS),
                       head_major(SB_HEAD_DIM), head_major(SB_HEAD_DIM), vt_shape(SB_HEAD_DIM)],
            compiler_params=params,
            name="proj",
        )(h, pos, inv_freq[:, None], spread, off_rope,
          norm_mix[l][None, :], q_latent_norm[l][None, :], kv_latent_norm[l][None, :],
          w_lat, w_sb.astype(BF16), w_uq_lin, w_uq_rot, w_uk, w_uv)

        seq_spec = lambda width: pl.BlockSpec((1, 1, s, width), lambda bi, hi: (bi, hi, 0, 0))
        blocked = lambda rows: pl.BlockSpec((1, 1, nb, rows, blk), lambda bi, hi: (bi, hi, 0, 0, 0))
        o_shape = jax.ShapeDtypeStruct((b, MLA_HEADS, nb, MLA_V_DIM, blk), F32)
        score_scratch = [pltpu.VMEM((2, N_CHAINS, blk, blk), F32),
                         pltpu.VMEM((N_CHAINS, blk, blk), BF16)]
        assert MLA_HEADS == SB_HEADS
        o_mla, o_sb = pl.pallas_call(
            functools.partial(_attn_kernel, n_super=nb // N_CHAINS),
            grid=(b, MLA_HEADS),
            in_specs=[seq_spec(MLA_PAD_DIM), seq_spec(MLA_PAD_DIM), blocked(MLA_VT_ROWS),
                      seq_spec(SB_HEAD_DIM), seq_spec(SB_HEAD_DIM), blocked(SB_HEAD_DIM),
                      _const_spec((blk, blk))],
            out_specs=[blocked(MLA_V_DIM), blocked(SB_HEAD_DIM)],
            out_shape=[o_shape, o_shape],
            scratch_shapes=score_scratch,
            compiler_params=params,
            name="attn",
        )(q_mla, k_mla, vt_mla, q_sb, k_sb, vt_sb, tri)

        assert depth == 1
        o_spec = pl.BlockSpec((1, MLA_HEADS, OUT_BLOCKS, MLA_V_DIM, blk), lambda bi, ti: (bi, 0, ti, 0, 0))
        tok_spec = pl.BlockSpec((1, OUT_BLOCKS * blk, d), lambda bi, ti: (bi, ti, 0))
        resident = lambda shape: pl.BlockSpec(shape, lambda *_: (0,) * len(shape),
                                              pipeline_mode=pl.Buffered(1))
        h = pl.pallas_call(
            functools.partial(_out_kernel, d_ff=d_ff),
            grid=(b, nb // OUT_BLOCKS),
            in_specs=[
                tok_spec, o_spec, o_spec,
                _const_spec((MLA_WIDTH, 1)), _const_spec((SB_WIDTH, 1)),
                resident((MLA_WIDTH, d)), resident((SB_WIDTH, d)),
                _const_spec((1, d)),
                resident((d, d_ff)), resident((d, d_ff)), resident((d_ff, d)),
                _const_spec((1, d)),
            ],
            out_specs=tok_spec,
            out_shape=jax.ShapeDtypeStruct((b, s, d), F32),
            compiler_params=params,
            name="out_ffn",
        )(h, o_mla, o_sb, out_norm_mla[l][:, None], out_norm_sb[l][:, None],
          w_o[l][:MLA_WIDTH].astype(BF16), w_o[l][MLA_WIDTH:].astype(BF16),
          norm_ffn[l][None, :], w_gate[l].astype(BF16), w_up[l].astype(BF16),
          w_down[l].astype(BF16), norm_final[None, :])
    return h
```

```python
import functools
import math

import jax
import jax.numpy as jnp
from jax import lax
from jax.experimental import pallas as pl
from jax.experimental.pallas import tpu as pltpu

EPS = 1e-6
ROPE_THETA = 10000.0
LOG2_E = 1.4426950408889634

LANES = 128
BF16_SUBLANE_TILE = 16
MXU_DIM_V7X = 256
VMEM_BYTES_V7X = 64 * 1024 * 1024

MLA_HEADS = 8
MLA_NOPE_DIM = 64
MLA_ROPE_DIM = 32
MLA_V_DIM = 64
MLA_QK_DIM = MLA_NOPE_DIM + MLA_ROPE_DIM
MLA_PAD_DIM = LANES
MLA_VT_ROWS = MLA_V_DIM + BF16_SUBLANE_TILE
Q_LORA_RANK = 256
KV_LORA_RANK = 128
SB_HEADS = 8
SB_HEAD_DIM = 64
MLA_WIDTH = MLA_HEADS * MLA_V_DIM
SB_WIDTH = SB_HEADS * SB_HEAD_DIM

SEQ_BLOCK = MXU_DIM_V7X
N_CHAINS = 4
MLA_K_PER_STEP = 4
SB_DEAD_BITS = 160.0
FF_CHUNK = MXU_DIM_V7X
OUT_BLOCKS = 2
PROJ_BLOCKS = 4
VMEM_LIMIT_BYTES = VMEM_BYTES_V7X * 7 // 8

F32 = jnp.float32
BF16 = jnp.bfloat16


def _rms_scale(v, axis):
    return lax.rsqrt(jnp.mean(v * v, axis=axis, keepdims=True) + EPS)


def _dot(a, b):
    return jnp.dot(a, b, preferred_element_type=F32)


def _dot_nt(a, b):
    return lax.dot_general(a, b, (((1,), (1,)), ((), ())), preferred_element_type=F32)


def _dot_tn(a, b):
    return lax.dot_general(a, b, (((0,), (0,)), ((), ())), preferred_element_type=F32)


def _proj_kernel(x_ref, pos_ref, freq_ref, spread_ref, off_rope_ref, g_mix_ref, g_q_ref, g_kv_ref,
                 w_lat_ref, w_sb_ref, w_uq_ref, w_uq_rot_ref, w_uk_ref, w_uv_ref,
                 q_mla_ref, k_mla_ref, vt_mla_ref, q_sb_ref, k_sb_ref, vt_sb_ref):
    x = x_ref[0]
    u = (x * _rms_scale(x, -1) * g_mix_ref[...]).astype(BF16)

    ang_t = freq_ref[...] * pos_ref[0].astype(F32)
    cos_t = jnp.cos(ang_t)
    sin_t = jnp.sin(ang_t)

    lat = _dot(u, w_lat_ref[...])
    sb = _dot(u, w_sb_ref[...])
    def spread(t):
        t1 = t.astype(BF16)
        r1 = t - t1.astype(F32)
        t2 = r1.astype(BF16)
        t3 = (r1 - t2.astype(F32)).astype(BF16)
        return _dot_tn(jnp.concatenate([t1, t2, t3], axis=0), spread_ref[...])

    cos = spread(cos_t) + off_rope_ref[...]
    sin = spread(sin_t)
    c_q = lat[:, :Q_LORA_RANK]
    c_kv = lat[:, Q_LORA_RANK:Q_LORA_RANK + KV_LORA_RANK]
    k_r = lat[:, Q_LORA_RANK + KV_LORA_RANK:Q_LORA_RANK + KV_LORA_RANK + MLA_PAD_DIM]
    k_r_rot = lat[:, Q_LORA_RANK + KV_LORA_RANK + MLA_PAD_DIM:]
    k_rope = k_r * cos + k_r_rot * sin

    ql = (c_q * _rms_scale(c_q, -1) * g_q_ref[...]).astype(BF16)
    kvl = (c_kv * _rms_scale(c_kv, -1) * g_kv_ref[...]).astype(BF16)

    q_lin = _dot(ql, w_uq_ref[...])
    q_rot = _dot(ql, w_uq_rot_ref[...])
    k_nope = _dot(kvl, w_uk_ref[...])
    v_mla = _dot(kvl, w_uv_ref[...])
    q_scale = LOG2_E / math.sqrt(MLA_QK_DIM)
    for h in range(MLA_HEADS):
        sl = slice(h * MLA_PAD_DIM, (h + 1) * MLA_PAD_DIM)
        q_h = (q_lin[:, sl] * cos + q_rot[:, sl] * sin) * q_scale
        q_mla_ref[0, h] = q_h.astype(BF16)
        k_mla_ref[0, h] = (k_nope[:, sl] + k_rope).astype(BF16)
    v_mla_t = v_mla.T
    pad_row = lax.broadcasted_iota(jnp.int32, (MLA_VT_ROWS - MLA_V_DIM, SEQ_BLOCK), 0)
    ones_then_zeros = jnp.where(pad_row == 0, 1.0, 0.0).astype(BF16)
    for h in range(MLA_HEADS):
        for t in range(PROJ_BLOCKS):
            vt_mla_ref[0, h, t, :MLA_V_DIM, :] = v_mla_t[h * MLA_V_DIM:(h + 1) * MLA_V_DIM,
                                                         t * SEQ_BLOCK:(t + 1) * SEQ_BLOCK].astype(BF16)
            vt_mla_ref[0, h, t, MLA_V_DIM:, :] = ones_then_zeros

    sb_scale = LOG2_E / math.sqrt(SB_HEAD_DIM)
    q_sb = sb[:, :SB_WIDTH] * sb_scale
    k_sb = sb[:, SB_WIDTH:2 * SB_WIDTH]
    v_sb_t = sb[:, 2 * SB_WIDTH:].T
    for h in range(SB_HEADS):
        sl = slice(h * SB_HEAD_DIM, (h + 1) * SB_HEAD_DIM)
        q_sb_ref[0, h] = q_sb[:, sl].astype(BF16)
        k_sb_ref[0, h] = k_sb[:, sl].astype(BF16)
        for t in range(PROJ_BLOCKS):
            vt_sb_ref[0, h, t] = v_sb_t[sl, t * SEQ_BLOCK:(t + 1) * SEQ_BLOCK].astype(BF16)


def _block_iotas():
    key_idx = lax.broadcasted_iota(jnp.int32, (SEQ_BLOCK, SEQ_BLOCK), 0)
    qry_idx = lax.broadcasted_iota(jnp.int32, (SEQ_BLOCK, SEQ_BLOCK), 1)
    return key_idx, qry_idx


def _attn_kernel(q_ref, k_ref, vt_ref, q_sb_ref, k_sb_ref, vt_sb_ref, tri_ref, o_ref, o_sb_ref,
                 z_scr, w_scr, *, n_super):
    blk, n, k_per_step = SEQ_BLOCK, N_CHAINS, MLA_K_PER_STEP
    key_idx, qry_idx = _block_iotas()
    causal = key_idx <= qry_idx
    unit = jnp.ones((1, blk), F32)
    sb_stages, sb_tail = _sb_super_block(q_sb_ref, k_sb_ref, vt_sb_ref, tri_ref, o_sb_ref)

    def rows(ref, j):
        return ref[0, 0, pl.ds(pl.multiple_of(j * blk, blk), blk), :]

    def softmax_blocks(zs, ms, diag_chain):
        new_ms, ws, alphas = [], [], []
        for i, (s, m) in enumerate(zip(zs, ms)):
            if i == diag_chain:
                s = jnp.where(causal, s, -jnp.inf)
            s_max = jnp.max(s, axis=0, keepdims=True)
            if m is None:
                m_new, alpha = s_max, None
            else:
                m_new = jnp.maximum(m, s_max)
                alpha = jnp.exp2(m - m_new)
            new_ms.append(m_new)
            ws.append(jnp.exp2(s - m_new).astype(BF16))
            alphas.append(alpha)
        return new_ms, ws, alphas

    def rescale_add(acc, pv, alpha):
        return pv if acc is None else alpha * acc + pv

    def super_body(sb, carry):
        base = sb * n

        def scores(j, chains):
            k = rows(k_ref, j)
            return [_dot_nt(k, rows(q_ref, base + a)) for a in chains]

        ms, accs = [None] * n, [None] * n
        side, sb_box = sb_stages(base)
        assert len(side) == n + 1
        zs_of = {n - 1: scores(base + n - 1, [n - 1])}
        side.pop(0)()
        for kb in reversed(range(n - 1)):
            zs_of[kb] = scores(base + kb, range(kb, n))
        for a, z in enumerate(scores(jnp.maximum(base - 1, 0), range(n))):
            z_scr[0, a] = z
        pending = None
        for kb in reversed(range(n)):
            chains = list(range(kb, n))
            zs = zs_of[kb]
            if pending is not None:
                vt = vt_ref[0, 0, base + kb + 1]
                for a, w, alpha in zip(*pending):
                    accs[a] = rescale_add(accs[a], _dot(vt, w), alpha)
            new_ms, ws, alphas = softmax_blocks(zs, [ms[a] for a in chains], 0)
            for a, m in zip(chains, new_ms):
                ms[a] = m
            side.pop(0)()
            if kb > 0:
                pending = (chains, ws, alphas)
            else:
                for a in range(n):
                    w_scr[a] = ws[a]
                accs[0] = jnp.zeros((vt_ref.shape[3], blk), F32)
                alphas[0] = unit

        def k_body(it, state):
            ms, accs, alphas = (list(t) for t in state)
            j = base - 1 - k_per_step * it
            for u in range(k_per_step):
                slot, j_cur, j_next = u % 2, j - u, jnp.maximum(j - u - 1, 0)
                k_next, vt_prev = rows(k_ref, j_next), vt_ref[0, 0, j_cur + 1]
                for a in range(n):
                    z_scr[1 - slot, a] = _dot_nt(k_next, rows(q_ref, base + a))
                    accs[a] = rescale_add(accs[a], _dot(vt_prev, w_scr[a]), alphas[a])
                ms, ws, alphas = softmax_blocks([z_scr[slot, a] for a in range(n)], ms, None)
                for a in range(n):
                    w_scr[a] = ws[a]
            return tuple(ms), tuple(accs), tuple(alphas)

        ms, accs, alphas = lax.fori_loop(
            0, sb * (n // k_per_step), k_body, (tuple(ms), tuple(accs), tuple(alphas)))
        vt = vt_ref[0, 0, 0]
        accs = [rescale_add(acc, _dot(vt, w_scr[a]), alpha)
                for a, (acc, alpha) in enumerate(zip(accs, alphas))]
        for a in range(n):
            o_ref[0, 0, base + a] = accs[a][:MLA_V_DIM] / accs[a][MLA_V_DIM:MLA_V_DIM + 1]
        sb_tail(base, sb_box)
        return carry

    lax.fori_loop(0, n_super, super_body, 0)


def _softplus2(z):
    return jnp.maximum(z, 0.0) + jnp.log2(1.0 + jnp.exp2(-jnp.abs(z)))


def _sb_super_block(q_ref, k_ref, vt_ref, tri_ref, o_ref):
    blk, n = SEQ_BLOCK, N_CHAINS
    key_idx, qry_idx = _block_iotas()
    strict = key_idx < qry_idx

    def rows(ref, j):
        return ref[0, 0, pl.ds(pl.multiple_of(j * blk, blk), blk), :]

    def logits(base, specs):
        return [_dot_nt(rows(k_ref, j), rows(q_ref, base + a)) for a, j, _ in specs]

    def suffix_sums(zs, specs):
        log_betas, sps = [], []
        for z, (_, _, diag) in zip(zs, specs):
            sp = _softplus2(z)
            log_betas.append(z - sp)
            if diag:
                sp = jnp.where(strict, sp, 0.0)
            sps.append(sp.astype(BF16))
        return log_betas, sps, [_dot(tri_ref[...], sp) for sp in sps]

    def products(log_betas, sps, sufs, specs):
        ws = []
        for log_beta, suf, (_, _, diag) in zip(log_betas, sufs, specs):
            w = jnp.exp2(log_beta + suf)
            if diag:
                w = jnp.where(strict, w, 0.0)
            ws.append(w.astype(BF16))
        return [(suf[0:1, :] - sp[0:1, :].astype(F32), _dot(vt_ref[0, 0, j], w))
                for w, sp, suf, (_, j, _) in zip(ws, sps, sufs, specs)]

    def tiles(base, groups):
        zs = [logits(base, g) for g in groups]
        out, prev = [], None
        for g, z in zip(groups, zs):
            cur = suffix_sums(z, g) + (g,)
            if prev is not None:
                out += products(*prev)
            prev = cur
        return out + products(*prev)

    def band(base, d):
        return [(a, jnp.maximum(base + a - d, 0), False) for a in range(n)]

    def factor(base, a, d, c):
        return jnp.where(base + a - d >= 0, jnp.exp2(c), 0.0)

    def live(cs):
        return (jnp.max(functools.reduce(jnp.maximum, cs)) > -SB_DEAD_BITS).astype(jnp.int32)

    def stages(base):
        groups = [[(a, base + a, True) for a in range(n)], band(base, 1)]
        box = {}

        def s0():
            box["z"] = [logits(base, g) for g in groups]

        def s1():
            box["c0"] = suffix_sums(box["z"][0], groups[0])

        def s2():
            box["c1"] = suffix_sums(box["z"][1], groups[1])
            box["r0"] = products(*box["c0"], groups[0])

        def s3():
            box["r1"] = products(*box["c1"], groups[1])

        def s4():
            cs = []
            for a in range(n):
                (sum0, pv0), (sum1, pv1) = box["r0"][a], box["r1"][a]
                o_ref[0, 0, base + a] = pv0 + pv1 * factor(base, a, 1, sum0)
                cs.append(sum0 + sum1)
            box["carry"] = cs

        return [s0, s1, s2, s3, s4], box

    def tail(base, box):
        def w_body(state):
            d, _, cs = state
            res = tiles(base, [band(base, d)])
            for a, (c, (_, pv)) in enumerate(zip(cs, res)):
                o_ref[0, 0, base + a] = o_ref[0, 0, base + a] + pv * factor(base, a, d, c)
            cs = tuple(c + block_sum for c, (block_sum, _) in zip(cs, res))
            return d + 1, live(cs), cs

        cs = box["carry"]
        lax.while_loop(lambda state: jnp.logical_and(state[0] < base + n, state[1] > 0), w_body,
                       (jnp.int32(2), live(cs), tuple(cs)))

    return stages, tail


def _out_kernel(x_ref, o_mla_ref, o_sb_ref, g_mla_ref, g_sb_ref, w_o_mla_ref, w_o_sb_ref,
                g_ffn_ref, w_gate_ref, w_up_ref, w_down_ref, g_final_ref, out_ref, *, d_ff):
    blk = SEQ_BLOCK

    def group(o_ref, g_ref, w_ref):
        parts = []
        for t in range(OUT_BLOCKS):
            o = o_ref[0, :, t].reshape(-1, blk)
            y = (o * _rms_scale(o, 0) * g_ref[...]).astype(BF16)
            parts.append(_dot_tn(y, w_ref[...]))
        return jnp.concatenate(parts, axis=0)

    h = x_ref[0] + group(o_mla_ref, g_mla_ref, w_o_mla_ref) + group(o_sb_ref, g_sb_ref, w_o_sb_ref)
    f = (h * _rms_scale(h, -1) * g_ffn_ref[...]).astype(BF16)
    ffn = jnp.zeros_like(h)
    for c in range(0, d_ff, FF_CHUNK):
        gate = _dot(f, w_gate_ref[:, c:c + FF_CHUNK])
        up = _dot(f, w_up_ref[:, c:c + FF_CHUNK])
        act = (gate * jax.nn.sigmoid(gate) * up).astype(BF16)
        ffn = ffn + _dot(act, w_down_ref[c:c + FF_CHUNK, :])
    h = h + ffn
    out_ref[0] = h * _rms_scale(h, -1) * g_final_ref[...]


def _rotate_half_cols(w):
    half = w.shape[-1] // 2
    return jnp.concatenate([-w[..., half:], w[..., :half]], axis=-1)


def _head_slots(nope, rope):
    ref = nope if nope is not None else rope
    r, h = ref.shape[0], ref.shape[1]
    nope = jnp.zeros((r, h, MLA_NOPE_DIM), ref.dtype) if nope is None else nope
    rope = jnp.zeros((r, h, MLA_ROPE_DIM), ref.dtype) if rope is None else rope
    pad = jnp.zeros((r, h, MLA_PAD_DIM - MLA_QK_DIM), ref.dtype)
    return jnp.concatenate([nope, rope, pad], axis=-1).reshape(r, h * MLA_PAD_DIM)


def _const_spec(shape):
    return pl.BlockSpec(shape, lambda *_: (0,) * len(shape))


def kernel(x, positions, norm_mix, w_in, q_latent_norm, w_uq, kv_latent_norm, w_ukv,
           out_norm_mla, out_norm_sb, w_o, norm_ffn, w_gate, w_up, w_down, norm_final):
    b, s, d = x.shape
    depth = w_in.shape[0]
    d_ff = w_gate.shape[-1]
    blk = SEQ_BLOCK
    nb = s // blk
    assert s % (blk * N_CHAINS) == 0 and nb % PROJ_BLOCKS == 0 and nb % OUT_BLOCKS == 0
    assert N_CHAINS % MLA_K_PER_STEP == 0 and MLA_K_PER_STEP % 2 == 0 and d_ff % FF_CHUNK == 0

    inv_freq = ROPE_THETA ** (-jnp.arange(0, MLA_ROPE_DIM, 2, dtype=F32) / MLA_ROPE_DIM)
    half = MLA_ROPE_DIM // 2
    lane = jnp.arange(MLA_PAD_DIM)[None, :]
    on_rope = (lane >= MLA_NOPE_DIM) & (lane < MLA_QK_DIM)
    spread = (on_rope & ((lane - MLA_NOPE_DIM) % half == jnp.arange(half)[:, None])).astype(BF16)
    spread = jnp.tile(spread, (3, 1))
    off_rope = (~on_rope).astype(F32)
    tri = jnp.where(jnp.arange(blk)[None, :] > jnp.arange(blk)[:, None], -1.0, 0.0).astype(BF16)
    pos = positions.reshape(b, 1, s)

    params = pltpu.CompilerParams(
        dimension_semantics=("arbitrary", "arbitrary"), vmem_limit_bytes=VMEM_LIMIT_BYTES)

    h = x
    for l in range(depth):
        o0 = Q_LORA_RANK
        o1 = o0 + KV_LORA_RANK
        o2 = o1 + MLA_ROPE_DIM
        w_cq, w_ckv, w_kr, w_sb = w_in[l][:, :o0], w_in[l][:, o0:o1], w_in[l][:, o1:o2], w_in[l][:, o2:]
        kr_slot = lambda w: _head_slots(None, w[:, None, :])
        w_lat = jnp.concatenate([w_cq, w_ckv, kr_slot(w_kr), kr_slot(_rotate_half_cols(w_kr))],
                                axis=1).astype(BF16)
        uq = w_uq[l].reshape(Q_LORA_RANK, MLA_HEADS, MLA_QK_DIM)
        uq_nope, uq_rope = uq[..., :MLA_NOPE_DIM], uq[..., MLA_NOPE_DIM:]
        w_uq_lin = _head_slots(uq_nope, uq_rope).astype(BF16)
        w_uq_rot = _head_slots(None, _rotate_half_cols(uq_rope)).astype(BF16)
        ukv = w_ukv[l].reshape(KV_LORA_RANK, MLA_HEADS, MLA_NOPE_DIM + MLA_V_DIM)
        w_uk = _head_slots(ukv[..., :MLA_NOPE_DIM], None).astype(BF16)
        w_uv = ukv[..., MLA_NOPE_DIM:].reshape(KV_LORA_RANK, MLA_WIDTH).astype(BF16)

        head_major = lambda width: jax.ShapeDtypeStruct((b, MLA_HEADS, s, width), BF16)
        tile = PROJ_BLOCKS * blk
        head_spec = lambda width: pl.BlockSpec((1, MLA_HEADS, tile, width), lambda bi, ti: (bi, 0, ti, 0))
        vt_shape = lambda rows: jax.ShapeDtypeStruct((b, MLA_HEADS, nb, rows, blk), BF16)
        vt_spec = lambda rows: pl.BlockSpec((1, MLA_HEADS, PROJ_BLOCKS, rows, blk),
                                            lambda bi, ti: (bi, 0, ti, 0, 0))
        q_mla, k_mla, vt_mla, q_sb, k_sb, vt_sb = pl.pallas_call(
            _proj_kernel,
            grid=(b, nb // PROJ_BLOCKS),
            in_specs=[
                pl.BlockSpec((1, tile, d), lambda bi, ti: (bi, ti, 0)),
                pl.BlockSpec((1, 1, tile), lambda bi, ti: (bi, 0, ti)),
                _const_spec((half, 1)),
                _const_spec((3 * half, MLA_PAD_DIM)),
                _const_spec((1, MLA_PAD_DIM)),
                _const_spec((1, d)),
                _const_spec((1, Q_LORA_RANK)),
                _const_spec((1, KV_LORA_RANK)),
                _const_spec(w_lat.shape),
                _const_spec((d, 3 * SB_WIDTH)),
                _const_spec(w_uq_lin.shape),
                _const_spec(w_uq_rot.shape),
                _const_spec(w_uk.shape),
                _const_spec(w_uv.shape),
            ],
            out_specs=[head_spec(MLA_PAD_DIM), head_spec(MLA_PAD_DIM), vt_spec(MLA_VT_ROWS),
                       head_spec(SB_HEAD_DIM), head_spec(SB_HEAD_DIM), vt_spec(SB_HEAD_DIM)],
            out_shape=[head_major(MLA_PAD_DIM), head_major(MLA_PAD_DIM), vt_shape(MLA_VT_ROWS),
                       head_major(SB_HEAD_DIM), head_major(SB_HEAD_DIM), vt_shape(SB_HEAD_DIM)],
            compiler_params=params,
            name="proj",
        )(h, pos, inv_freq[:, None], spread, off_rope,
          norm_mix[l][None, :], q_latent_norm[l][None, :], kv_latent_norm[l][None, :],
          w_lat, w_sb.astype(BF16), w_uq_lin, w_uq_rot, w_uk, w_uv)

        seq_spec = lambda width: pl.BlockSpec((1, 1, s, width), lambda bi, hi: (bi, hi, 0, 0))
        blocked = lambda rows: pl.BlockSpec((1, 1, nb, rows, blk), lambda bi, hi: (bi, hi, 0, 0, 0))
        o_shape = jax.ShapeDtypeStruct((b, MLA_HEADS, nb, MLA_V_DIM, blk), F32)
        score_scratch = [pltpu.VMEM((2, N_CHAINS, blk, blk), F32),
                         pltpu.VMEM((N_CHAINS, blk, blk), BF16)]
        assert MLA_HEADS == SB_HEADS
        o_mla, o_sb = pl.pallas_call(
            functools.partial(_attn_kernel, n_super=nb // N_CHAINS),
            grid=(b, MLA_HEADS),
            in_specs=[seq_spec(MLA_PAD_DIM), seq_spec(MLA_PAD_DIM), blocked(MLA_VT_ROWS),
                      seq_spec(SB_HEAD_DIM), seq_spec(SB_HEAD_DIM), blocked(SB_HEAD_DIM),
                      _const_spec((blk, blk))],
            out_specs=[blocked(MLA_V_DIM), blocked(SB_HEAD_DIM)],
            out_shape=[o_shape, o_shape],
            scratch_shapes=score_scratch,
            compiler_params=params,
            name="attn",
        )(q_mla, k_mla, vt_mla, q_sb, k_sb, vt_sb, tri)

        assert depth == 1
        o_spec = pl.BlockSpec((1, MLA_HEADS, OUT_BLOCKS, MLA_V_DIM, blk), lambda bi, ti: (bi, 0, ti, 0, 0))
        tok_spec = pl.BlockSpec((1, OUT_BLOCKS * blk, d), lambda bi, ti: (bi, ti, 0))
        resident = lambda shape: pl.BlockSpec(shape, lambda *_: (0,) * len(shape),
                                              pipeline_mode=pl.Buffered(1))
        h = pl.pallas_call(
            functools.partial(_out_kernel, d_ff=d_ff),
            grid=(b, nb // OUT_BLOCKS),
            in_specs=[
                tok_spec, o_spec, o_spec,
                _const_spec((MLA_WIDTH, 1)), _const_spec((SB_WIDTH, 1)),
                resident((MLA_WIDTH, d)), resident((SB_WIDTH, d)),
                _const_spec((1, d)),
                resident((d, d_ff)), resident((d, d_ff)), resident((d_ff, d)),
                _const_spec((1, d)),
            ],
            out_specs=tok_spec,
            out_shape=jax.ShapeDtypeStruct((b, s, d), F32),
            compiler_params=params,
            name="out_ffn",
        )(h, o_mla, o_sb, out_norm_mla[l][:, None], out_norm_sb[l][:, None],
          w_o[l][:MLA_WIDTH].astype(BF16), w_o[l][MLA_WIDTH:].astype(BF16),
          norm_ffn[l][None, :], w_gate[l].astype(BF16), w_up[l].astype(BF16),
          w_down[l].astype(BF16), norm_final[None, :])
    return h
```

```python
import functools
import math

import jax
import jax.numpy as jnp
from jax import lax
from jax.experimental import pallas as pl
from jax.experimental.pallas import tpu as pltpu

EPS = 1e-6
ROPE_THETA = 10000.0
LOG2_E = 1.4426950408889634

LANES = 128
BF16_SUBLANE_TILE = 16
MXU_DIM_V7X = 256
VMEM_BYTES_V7X = 64 * 1024 * 1024

MLA_HEADS = 8
MLA_NOPE_DIM = 64
MLA_ROPE_DIM = 32
MLA_V_DIM = 64
MLA_QK_DIM = MLA_NOPE_DIM + MLA_ROPE_DIM
MLA_PAD_DIM = LANES
MLA_VT_ROWS = MLA_V_DIM + BF16_SUBLANE_TILE
Q_LORA_RANK = 256
KV_LORA_RANK = 128
SB_HEADS = 8
SB_HEAD_DIM = 64
MLA_WIDTH = MLA_HEADS * MLA_V_DIM
SB_WIDTH = SB_HEADS * SB_HEAD_DIM

SEQ_BLOCK = MXU_DIM_V7X
N_CHAINS = 4
MLA_K_PER_STEP = 4
SB_DEAD_BITS = 160.0
FF_CHUNK = MXU_DIM_V7X
OUT_BLOCKS = 2
PROJ_BLOCKS = 4
VMEM_LIMIT_BYTES = VMEM_BYTES_V7X * 7 // 8

F32 = jnp.float32
BF16 = jnp.bfloat16


def _rms_scale(v, axis):
    return lax.rsqrt(jnp.mean(v * v, axis=axis, keepdims=True) + EPS)


def _dot(a, b):
    return jnp.dot(a, b, preferred_element_type=F32)


def _dot_nt(a, b):
    return lax.dot_general(a, b, (((1,), (1,)), ((), ())), preferred_element_type=F32)


def _dot_tn(a, b):
    return lax.dot_general(a, b, (((0,), (0,)), ((), ())), preferred_element_type=F32)


def _proj_kernel(x_ref, pos_ref, freq_ref, spread_ref, off_rope_ref, g_mix_ref, g_q_ref, g_kv_ref,
                 w_lat_ref, w_sb_ref, w_uq_ref, w_uq_rot_ref, w_uk_ref, w_uv_ref,
                 q_mla_ref, k_mla_ref, vt_mla_ref, q_sb_ref, k_sb_ref, vt_sb_ref):
    x = x_ref[0]
    u = (x * _rms_scale(x, -1) * g_mix_ref[...]).astype(BF16)

    ang_t = freq_ref[...] * pos_ref[0].astype(F32)
    cos_t = jnp.cos(ang_t)
    sin_t = jnp.sin(ang_t)

    lat = _dot_nt(u, w_lat_ref[...])
    sb = _dot_nt(u, w_sb_ref[...])
    def spread(t):
        t1 = t.astype(BF16)
        r1 = t - t1.astype(F32)
        t2 = r1.astype(BF16)
        t3 = (r1 - t2.astype(F32)).astype(BF16)
        return _dot_tn(jnp.concatenate([t1, t2, t3], axis=0), spread_ref[...])

    cos = spread(cos_t) + off_rope_ref[...]
    sin = spread(sin_t)
    c_q = lat[:, :Q_LORA_RANK]
    c_kv = lat[:, Q_LORA_RANK:Q_LORA_RANK + KV_LORA_RANK]
    k_r = lat[:, Q_LORA_RANK + KV_LORA_RANK:Q_LORA_RANK + KV_LORA_RANK + MLA_PAD_DIM]
    k_r_rot = lat[:, Q_LORA_RANK + KV_LORA_RANK + MLA_PAD_DIM:]
    k_rope = k_r * cos + k_r_rot * sin

    ql = (c_q * _rms_scale(c_q, -1) * g_q_ref[...]).astype(BF16)
    kvl = (c_kv * _rms_scale(c_kv, -1) * g_kv_ref[...]).astype(BF16)

    q_lin = _dot(ql, w_uq_ref[...])
    q_rot = _dot(ql, w_uq_rot_ref[...])
    k_nope = _dot(kvl, w_uk_ref[...])
    v_mla = _dot(kvl, w_uv_ref[...])
    q_scale = LOG2_E / math.sqrt(MLA_QK_DIM)
    for h in range(MLA_HEADS):
        sl = slice(h * MLA_PAD_DIM, (h + 1) * MLA_PAD_DIM)
        q_h = (q_lin[:, sl] * cos + q_rot[:, sl] * sin) * q_scale
        q_mla_ref[0, h] = q_h.astype(BF16)
        k_mla_ref[0, h] = (k_nope[:, sl] + k_rope).astype(BF16)
    v_mla_t = v_mla.T
    pad_row = lax.broadcasted_iota(jnp.int32, (MLA_VT_ROWS - MLA_V_DIM, SEQ_BLOCK), 0)
    ones_then_zeros = jnp.where(pad_row == 0, 1.0, 0.0).astype(BF16)
    for h in range(MLA_HEADS):
        for t in range(PROJ_BLOCKS):
            vt_mla_ref[0, h, t, :MLA_V_DIM, :] = v_mla_t[h * MLA_V_DIM:(h + 1) * MLA_V_DIM,
                                                         t * SEQ_BLOCK:(t + 1) * SEQ_BLOCK].astype(BF16)
            vt_mla_ref[0, h, t, MLA_V_DIM:, :] = ones_then_zeros

    sb_scale = LOG2_E / math.sqrt(SB_HEAD_DIM)
    q_sb = sb[:, :SB_WIDTH] * sb_scale
    k_sb = sb[:, SB_WIDTH:2 * SB_WIDTH]
    v_sb_t = sb[:, 2 * SB_WIDTH:].T
    for h in range(SB_HEADS):
        sl = slice(h * SB_HEAD_DIM, (h + 1) * SB_HEAD_DIM)
        q_sb_ref[0, h] = q_sb[:, sl].astype(BF16)
        k_sb_ref[0, h] = k_sb[:, sl].astype(BF16)
        for t in range(PROJ_BLOCKS):
            vt_sb_ref[0, h, t] = v_sb_t[sl, t * SEQ_BLOCK:(t + 1) * SEQ_BLOCK].astype(BF16)


def _block_iotas():
    key_idx = lax.broadcasted_iota(jnp.int32, (SEQ_BLOCK, SEQ_BLOCK), 0)
    qry_idx = lax.broadcasted_iota(jnp.int32, (SEQ_BLOCK, SEQ_BLOCK), 1)
    return key_idx, qry_idx


def _attn_kernel(q_ref, k_ref, vt_ref, q_sb_ref, k_sb_ref, vt_sb_ref, tri_ref, o_ref, o_sb_ref,
                 z_scr, w_scr, *, n_super):
    blk, n, k_per_step = SEQ_BLOCK, N_CHAINS, MLA_K_PER_STEP
    key_idx, qry_idx = _block_iotas()
    causal = key_idx <= qry_idx
    unit = jnp.ones((1, blk), F32)
    sb_stages, sb_tail = _sb_super_block(q_sb_ref, k_sb_ref, vt_sb_ref, tri_ref, o_sb_ref)

    def rows(ref, j):
        return ref[0, 0, pl.ds(pl.multiple_of(j * blk, blk), blk), :]

    def softmax_blocks(zs, ms, diag_chain):
        new_ms, ws, alphas = [], [], []
        for i, (s, m) in enumerate(zip(zs, ms)):
            if i == diag_chain:
                s = jnp.where(causal, s, -jnp.inf)
            s_max = jnp.max(s, axis=0, keepdims=True)
            if m is None:
                m_new, alpha = s_max, None
            else:
                m_new = jnp.maximum(m, s_max)
                alpha = jnp.exp2(m - m_new)
            new_ms.append(m_new)
            ws.append(jnp.exp2(s - m_new).astype(BF16))
            alphas.append(alpha)
        return new_ms, ws, alphas

    def rescale_add(acc, pv, alpha):
        return pv if acc is None else alpha * acc + pv

    def super_body(sb, carry):
        base = sb * n

        def scores(j, chains):
            k = rows(k_ref, j)
            return [_dot_nt(k, rows(q_ref, base + a)) for a in chains]

        ms, accs = [None] * n, [None] * n
        side, sb_box = sb_stages(base)
        assert len(side) == n + 1
        zs_of = {n - 1: scores(base + n - 1, [n - 1])}
        side.pop(0)()
        for kb in reversed(range(n - 1)):
            zs_of[kb] = scores(base + kb, range(kb, n))
        for a, z in enumerate(scores(jnp.maximum(base - 1, 0), range(n))):
            z_scr[0, a] = z
        pending = None
        for kb in reversed(range(n)):
            chains = list(range(kb, n))
            zs = zs_of[kb]
            if pending is not None:
                vt = vt_ref[0, 0, base + kb + 1]
                for a, w, alpha in zip(*pending):
                    accs[a] = rescale_add(accs[a], _dot(vt, w), alpha)
            new_ms, ws, alphas = softmax_blocks(zs, [ms[a] for a in chains], 0)
            for a, m in zip(chains, new_ms):
                ms[a] = m
            side.pop(0)()
            if kb > 0:
                pending = (chains, ws, alphas)
            else:
                for a in range(n):
                    w_scr[a] = ws[a]
                accs[0] = jnp.zeros((vt_ref.shape[3], blk), F32)
                alphas[0] = unit

        def k_body(it, state):
            ms, accs, alphas = (list(t) for t in state)
            j = base - 1 - k_per_step * it
            for u in range(k_per_step):
                slot, j_cur, j_next = u % 2, j - u, jnp.maximum(j - u - 1, 0)
                k_next, vt_prev = rows(k_ref, j_next), vt_ref[0, 0, j_cur + 1]
                for a in range(n):
                    z_scr[1 - slot, a] = _dot_nt(k_next, rows(q_ref, base + a))
                    accs[a] = rescale_add(accs[a], _dot(vt_prev, w_scr[a]), alphas[a])
                ms, ws, alphas = softmax_blocks([z_scr[slot, a] for a in range(n)], ms, None)
                for a in range(n):
                    w_scr[a] = ws[a]
            return tuple(ms), tuple(accs), tuple(alphas)

        ms, accs, alphas = lax.fori_loop(
            0, sb * (n // k_per_step), k_body, (tuple(ms), tuple(accs), tuple(alphas)))
        vt = vt_ref[0, 0, 0]
        accs = [rescale_add(acc, _dot(vt, w_scr[a]), alpha)
                for a, (acc, alpha) in enumerate(zip(accs, alphas))]
        for a in range(n):
            o_ref[0, 0, base + a] = accs[a][:MLA_V_DIM] / accs[a][MLA_V_DIM:MLA_V_DIM + 1]
        sb_tail(base, sb_box)
        return carry

    lax.fori_loop(0, n_super, super_body, 0)


def _softplus2(z):
    return jnp.maximum(z, 0.0) + jnp.log2(1.0 + jnp.exp2(-jnp.abs(z)))


def _sb_super_block(q_ref, k_ref, vt_ref, tri_ref, o_ref):
    blk, n = SEQ_BLOCK, N_CHAINS
    key_idx, qry_idx = _block_iotas()
    strict = key_idx < qry_idx

    def rows(ref, j):
        return ref[0, 0, pl.ds(pl.multiple_of(j * blk, blk), blk), :]

    def logits(base, specs):
        return [_dot_nt(rows(k_ref, j), rows(q_ref, base + a)) for a, j, _ in specs]

    def suffix_sums(zs, specs):
        log_betas, sps = [], []
        for z, (_, _, diag) in zip(zs, specs):
            sp = _softplus2(z)
            log_betas.append(z - sp)
            if diag:
                sp = jnp.where(strict, sp, 0.0)
            sps.append(sp.astype(BF16))
        return log_betas, sps, [_dot(tri_ref[...], sp) for sp in sps]

    def products(log_betas, sps, sufs, specs):
        ws = []
        for log_beta, suf, (_, _, diag) in zip(log_betas, sufs, specs):
            w = jnp.exp2(log_beta + suf)
            if diag:
                w = jnp.where(strict, w, 0.0)
            ws.append(w.astype(BF16))
        return [(suf[0:1, :] - sp[0:1, :].astype(F32), _dot(vt_ref[0, 0, j], w))
                for w, sp, suf, (_, j, _) in zip(ws, sps, sufs, specs)]

    def tiles(base, groups):
        zs = [logits(base, g) for g in groups]
        out, prev = [], None
        for g, z in zip(groups, zs):
            cur = suffix_sums(z, g) + (g,)
            if prev is not None:
                out += products(*prev)
            prev = cur
        return out + products(*prev)

    def band(base, d):
        return [(a, jnp.maximum(base + a - d, 0), False) for a in range(n)]

    def factor(base, a, d, c):
        return jnp.where(base + a - d >= 0, jnp.exp2(c), 0.0)

    def live(cs):
        return (jnp.max(functools.reduce(jnp.maximum, cs)) > -SB_DEAD_BITS).astype(jnp.int32)

    def stages(base):
        groups = [[(a, base + a, True) for a in range(n)], band(base, 1)]
        box = {}

        def s0():
            box["z"] = [logits(base, g) for g in groups]

        def s1():
            box["c0"] = suffix_sums(box["z"][0], groups[0])

        def s2():
            box["c1"] = suffix_sums(box["z"][1], groups[1])
            box["r0"] = products(*box["c0"], groups[0])

        def s3():
            box["r1"] = products(*box["c1"], groups[1])

        def s4():
            cs = []
            for a in range(n):
                (sum0, pv0), (sum1, pv1) = box["r0"][a], box["r1"][a]
                o_ref[0, 0, base + a] = pv0 + pv1 * factor(base, a, 1, sum0)
                cs.append(sum0 + sum1)
            box["carry"] = cs

        return [s0, s1, s2, s3, s4], box

    def tail(base, box):
        def w_body(state):
            d, _, cs = state
            res = tiles(base, [band(base, d)])
            for a, (c, (_, pv)) in enumerate(zip(cs, res)):
                o_ref[0, 0, base + a] = o_ref[0, 0, base + a] + pv * factor(base, a, d, c)
            cs = tuple(c + block_sum for c, (block_sum, _) in zip(cs, res))
            return d + 1, live(cs), cs

        cs = box["carry"]
        lax.while_loop(lambda state: jnp.logical_and(state[0] < base + n, state[1] > 0), w_body,
                       (jnp.int32(2), live(cs), tuple(cs)))

    return stages, tail


def _out_kernel(x_ref, o_mla_ref, o_sb_ref, g_mla_ref, g_sb_ref, w_o_mla_ref, w_o_sb_ref,
                g_ffn_ref, w_gate_ref, w_up_ref, w_down_ref, g_final_ref, out_ref, *, d_ff):
    blk = SEQ_BLOCK

    def group(o_ref, g_ref, w_ref):
        parts = []
        for t in range(OUT_BLOCKS):
            o = o_ref[0, :, t].reshape(-1, blk)
            y = (o * _rms_scale(o, 0) * g_ref[...]).astype(BF16)
            parts.append(_dot_tn(y, w_ref[...]))
        return jnp.concatenate(parts, axis=0)

    h = x_ref[0] + group(o_mla_ref, g_mla_ref, w_o_mla_ref) + group(o_sb_ref, g_sb_ref, w_o_sb_ref)
    f = (h * _rms_scale(h, -1) * g_ffn_ref[...]).astype(BF16)
    ffn = jnp.zeros_like(h)
    for c in range(0, d_ff, FF_CHUNK):
        gate = _dot(f, w_gate_ref[:, c:c + FF_CHUNK])
        up = _dot(f, w_up_ref[:, c:c + FF_CHUNK])
        act = (gate * jax.nn.sigmoid(gate) * up).astype(BF16)
        ffn = ffn + _dot(act, w_down_ref[c:c + FF_CHUNK, :])
    h = h + ffn
    out_ref[0] = h * _rms_scale(h, -1) * g_final_ref[...]


def _rotate_half_cols(w):
    half = w.shape[-1] // 2
    return jnp.concatenate([-w[..., half:], w[..., :half]], axis=-1)


def _head_slots(nope, rope):
    ref = nope if nope is not None else rope
    r, h = ref.shape[0], ref.shape[1]
    nope = jnp.zeros((r, h, MLA_NOPE_DIM), ref.dtype) if nope is None else nope
    rope = jnp.zeros((r, h, MLA_ROPE_DIM), ref.dtype) if rope is None else rope
    pad = jnp.zeros((r, h, MLA_PAD_DIM - MLA_QK_DIM), ref.dtype)
    return jnp.concatenate([nope, rope, pad], axis=-1).reshape(r, h * MLA_PAD_DIM)


def _const_spec(shape):
    return pl.BlockSpec(shape, lambda *_: (0,) * len(shape))


def kernel(x, positions, norm_mix, w_in, q_latent_norm, w_uq, kv_latent_norm, w_ukv,
           out_norm_mla, out_norm_sb, w_o, norm_ffn, w_gate, w_up, w_down, norm_final):
    b, s, d = x.shape
    depth = w_in.shape[0]
    d_ff = w_gate.shape[-1]
    blk = SEQ_BLOCK
    nb = s // blk
    assert s % (blk * N_CHAINS) == 0 and nb % PROJ_BLOCKS == 0 and nb % OUT_BLOCKS == 0
    assert N_CHAINS % MLA_K_PER_STEP == 0 and MLA_K_PER_STEP % 2 == 0 and d_ff % FF_CHUNK == 0

    inv_freq = ROPE_THETA ** (-jnp.arange(0, MLA_ROPE_DIM, 2, dtype=F32) / MLA_ROPE_DIM)
    half = MLA_ROPE_DIM // 2
    lane = jnp.arange(MLA_PAD_DIM)[None, :]
    on_rope = (lane >= MLA_NOPE_DIM) & (lane < MLA_QK_DIM)
    spread = (on_rope & ((lane - MLA_NOPE_DIM) % half == jnp.arange(half)[:, None])).astype(BF16)
    spread = jnp.tile(spread, (3, 1))
    off_rope = (~on_rope).astype(F32)
    tri = jnp.where(jnp.arange(blk)[None, :] > jnp.arange(blk)[:, None], -1.0, 0.0).astype(BF16)
    pos = positions.reshape(b, 1, s)

    params = pltpu.CompilerParams(
        dimension_semantics=("arbitrary", "arbitrary"), vmem_limit_bytes=VMEM_LIMIT_BYTES)

    h = x
    for l in range(depth):
        o0 = Q_LORA_RANK
        o1 = o0 + KV_LORA_RANK
        o2 = o1 + MLA_ROPE_DIM
        w_in_t = jnp.swapaxes(w_in[l], 0, 1)
        w_cq, w_ckv, w_kr, w_sb = w_in_t[:o0], w_in_t[o0:o1], w_in_t[o1:o2], w_in_t[o2:]
        kr_slot = lambda w: jnp.pad(w, ((MLA_NOPE_DIM, MLA_PAD_DIM - MLA_QK_DIM), (0, 0)))
        w_kr_rot = jnp.concatenate([-w_kr[MLA_ROPE_DIM // 2:], w_kr[:MLA_ROPE_DIM // 2]], axis=0)
        w_lat = jnp.concatenate([w_cq, w_ckv, kr_slot(w_kr), kr_slot(w_kr_rot)],
                                axis=0).astype(BF16)
        uq = w_uq[l].reshape(Q_LORA_RANK, MLA_HEADS, MLA_QK_DIM)
        uq_nope, uq_rope = uq[..., :MLA_NOPE_DIM], uq[..., MLA_NOPE_DIM:]
        w_uq_lin = _head_slots(uq_nope, uq_rope).astype(BF16)
        w_uq_rot = _head_slots(None, _rotate_half_cols(uq_rope)).astype(BF16)
        ukv = w_ukv[l].reshape(KV_LORA_RANK, MLA_HEADS, MLA_NOPE_DIM + MLA_V_DIM)
        w_uk = _head_slots(ukv[..., :MLA_NOPE_DIM], None).astype(BF16)
        w_uv = ukv[..., MLA_NOPE_DIM:].reshape(KV_LORA_RANK, MLA_WIDTH).astype(BF16)

        head_major = lambda width: jax.ShapeDtypeStruct((b, MLA_HEADS, s, width), BF16)
        tile = PROJ_BLOCKS * blk
        head_spec = lambda width: pl.BlockSpec((1, MLA_HEADS, tile, width), lambda bi, ti: (bi, 0, ti, 0))
        vt_shape = lambda rows: jax.ShapeDtypeStruct((b, MLA_HEADS, nb, rows, blk), BF16)
        vt_spec = lambda rows: pl.BlockSpec((1, MLA_HEADS, PROJ_BLOCKS, rows, blk),
                                            lambda bi, ti: (bi, 0, ti, 0, 0))
        q_mla, k_mla, vt_mla, q_sb, k_sb, vt_sb = pl.pallas_call(
            _proj_kernel,
            grid=(b, nb // PROJ_BLOCKS),
            in_specs=[
                pl.BlockSpec((1, tile, d), lambda bi, ti: (bi, ti, 0)),
                pl.BlockSpec((1, 1, tile), lambda bi, ti: (bi, 0, ti)),
                _const_spec((half, 1)),
                _const_spec((3 * half, MLA_PAD_DIM)),
                _const_spec((1, MLA_PAD_DIM)),
                _const_spec((1, d)),
                _const_spec((1, Q_LORA_RANK)),
                _const_spec((1, KV_LORA_RANK)),
                _const_spec(w_lat.shape),
                _const_spec((3 * SB_WIDTH, d)),
                _const_spec(w_uq_lin.shape),
                _const_spec(w_uq_rot.shape),
                _const_spec(w_uk.shape),
                _const_spec(w_uv.shape),
            ],
            out_specs=[head_spec(MLA_PAD_DIM), head_spec(MLA_PAD_DIM), vt_spec(MLA_VT_ROWS),
                       head_spec(SB_HEAD_DIM), head_spec(SB_HEAD_DIM), vt_spec(SB_HEAD_DIM)],
            out_shape=[head_major(MLA_PAD_DIM), head_major(MLA_PAD_DIM), vt_shape(MLA_VT_ROWS),
                       head_major(SB_HEAD_DIM), head_major(SB_HEAD_DIM), vt_shape(SB_HEAD_DIM)],
            compiler_params=params,
            name="proj",
        )(h, pos, inv_freq[:, None], spread, off_rope,
          norm_mix[l][None, :], q_latent_norm[l][None, :], kv_latent_norm[l][None, :],
          w_lat, w_sb.astype(BF16), w_uq_lin, w_uq_rot, w_uk, w_uv)

        seq_spec = lambda width: pl.BlockSpec((1, 1, s, width), lambda bi, hi: (bi, hi, 0, 0))
        blocked = lambda rows: pl.BlockSpec((1, 1, nb, rows, blk), lambda bi, hi: (bi, hi, 0, 0, 0))
        o_shape = jax.ShapeDtypeStruct((b, MLA_HEADS, nb, MLA_V_DIM, blk), F32)
        score_scratch = [pltpu.VMEM((2, N_CHAINS, blk, blk), F32),
                         pltpu.VMEM((N_CHAINS, blk, blk), BF16)]
        assert MLA_HEADS == SB_HEADS
        o_mla, o_sb = pl.pallas_call(
            functools.partial(_attn_kernel, n_super=nb // N_CHAINS),
            grid=(b, MLA_HEADS),
            in_specs=[seq_spec(MLA_PAD_DIM), seq_spec(MLA_PAD_DIM), blocked(MLA_VT_ROWS),
                      seq_spec(SB_HEAD_DIM), seq_spec(SB_HEAD_DIM), blocked(SB_HEAD_DIM),
                      _const_spec((blk, blk))],
            out_specs=[blocked(MLA_V_DIM), blocked(SB_HEAD_DIM)],
            out_shape=[o_shape, o_shape],
            scratch_shapes=score_scratch,
            compiler_params=params,
            name="attn",
        )(q_mla, k_mla, vt_mla, q_sb, k_sb, vt_sb, tri)

        assert depth == 1
        o_spec = pl.BlockSpec((1, MLA_HEADS, OUT_BLOCKS, MLA_V_DIM, blk), lambda bi, ti: (bi, 0, ti, 0, 0))
        tok_spec = pl.BlockSpec((1, OUT_BLOCKS * blk, d), lambda bi, ti: (bi, ti, 0))
        resident = lambda shape: pl.BlockSpec(shape, lambda *_: (0,) * len(shape),
                                              pipeline_mode=pl.Buffered(1))
        h = pl.pallas_call(
            functools.partial(_out_kernel, d_ff=d_ff),
            grid=(b, nb // OUT_BLOCKS),
            in_specs=[
                tok_spec, o_spec, o_spec,
                _const_spec((MLA_WIDTH, 1)), _const_spec((SB_WIDTH, 1)),
                resident((MLA_WIDTH, d)), resident((SB_WIDTH, d)),
                _const_spec((1, d)),
                resident((d, d_ff)), resident((d, d_ff)), resident((d_ff, d)),
                _const_spec((1, d)),
            ],
            out_specs=tok_spec,
            out_shape=jax.ShapeDtypeStruct((b, s, d), F32),
            compiler_params=params,
            name="out_ffn",
        )(h, o_mla, o_sb, out_norm_mla[l][:, None], out_norm_sb[l][:, None],
          w_o[l][:MLA_WIDTH].astype(BF16), w_o[l][MLA_WIDTH:].astype(BF16),
          norm_ffn[l][None, :], w_gate[l].astype(BF16), w_up[l].astype(BF16),
          w_down[l].astype(BF16), norm_final[None, :])
    return h
```

```python
import functools
import math

import jax
import jax.numpy as jnp
from jax import lax
from jax.experimental import pallas as pl
from jax.experimental.pallas import tpu as pltpu

EPS = 1e-6
ROPE_THETA = 10000.0
LOG2_E = 1.4426950408889634

LANES = 128
BF16_SUBLANE_TILE = 16
MXU_DIM_V7X = 256
VMEM_BYTES_V7X = 64 * 1024 * 1024

MLA_HEADS = 8
MLA_NOPE_DIM = 64
MLA_ROPE_DIM = 32
MLA_V_DIM = 64
MLA_QK_DIM = MLA_NOPE_DIM + MLA_ROPE_DIM
MLA_PAD_DIM = LANES
MLA_VT_ROWS = MLA_V_DIM + BF16_SUBLANE_TILE
Q_LORA_RANK = 256
KV_LORA_RANK = 128
SB_HEADS = 8
SB_HEAD_DIM = 64
MLA_WIDTH = MLA_HEADS * MLA_V_DIM
SB_WIDTH = SB_HEADS * SB_HEAD_DIM

SEQ_BLOCK = MXU_DIM_V7X
N_CHAINS = 4
MLA_K_PER_STEP = 4
SB_DEAD_BITS = 160.0
FF_CHUNK = MXU_DIM_V7X
OUT_BLOCKS = 2
PROJ_BLOCKS = 4
VMEM_LIMIT_BYTES = VMEM_BYTES_V7X * 7 // 8

F32 = jnp.float32
BF16 = jnp.bfloat16


def _rms_scale(v, axis):
    return lax.rsqrt(jnp.mean(v * v, axis=axis, keepdims=True) + EPS)


def _dot(a, b):
    return jnp.dot(a, b, preferred_element_type=F32)


def _dot_nt(a, b):
    return lax.dot_general(a, b, (((1,), (1,)), ((), ())), preferred_element_type=F32)


def _dot_tn(a, b):
    return lax.dot_general(a, b, (((0,), (0,)), ((), ())), preferred_element_type=F32)


def _proj_kernel(x_ref, pos_ref, freq_ref, spread_ref, off_rope_ref, g_mix_ref, g_q_ref, g_kv_ref,
                 w_lat_ref, w_sb_ref, w_uq_ref, w_uq_rot_ref, w_uk_ref, w_uv_ref,
                 w_gate_ref, w_up_ref, w_down_ref,
                 q_mla_ref, k_mla_ref, vt_mla_ref, q_sb_ref, k_sb_ref, vt_sb_ref,
                 w_gate_bf_ref, w_up_bf_ref, w_down_bf_ref):
    w_gate_bf_ref[...] = w_gate_ref[0].astype(BF16)
    w_up_bf_ref[...] = w_up_ref[0].astype(BF16)
    w_down_bf_ref[...] = w_down_ref[0].astype(BF16)

    x = x_ref[0]
    u = (x * _rms_scale(x, -1) * g_mix_ref[...]).astype(BF16)

    ang_t = freq_ref[...] * pos_ref[0].astype(F32)
    cos_t = jnp.cos(ang_t)
    sin_t = jnp.sin(ang_t)

    lat = _dot_nt(u, w_lat_ref[...])
    sb = _dot_nt(u, w_sb_ref[...])
    def spread(t):
        t1 = t.astype(BF16)
        r1 = t - t1.astype(F32)
        t2 = r1.astype(BF16)
        t3 = (r1 - t2.astype(F32)).astype(BF16)
        return _dot_tn(jnp.concatenate([t1, t2, t3], axis=0), spread_ref[...])

    cos = spread(cos_t) + off_rope_ref[...]
    sin = spread(sin_t)
    c_q = lat[:, :Q_LORA_RANK]
    c_kv = lat[:, Q_LORA_RANK:Q_LORA_RANK + KV_LORA_RANK]
    k_r = lat[:, Q_LORA_RANK + KV_LORA_RANK:Q_LORA_RANK + KV_LORA_RANK + MLA_PAD_DIM]
    k_r_rot = lat[:, Q_LORA_RANK + KV_LORA_RANK + MLA_PAD_DIM:]
    k_rope = k_r * cos + k_r_rot * sin

    ql = (c_q * _rms_scale(c_q, -1) * g_q_ref[...]).astype(BF16)
    kvl = (c_kv * _rms_scale(c_kv, -1) * g_kv_ref[...]).astype(BF16)

    q_lin = _dot(ql, w_uq_ref[...])
    q_rot = _dot(ql, w_uq_rot_ref[...])
    k_nope = _dot(kvl, w_uk_ref[...])
    v_mla = _dot(kvl, w_uv_ref[...])
    q_scale = LOG2_E / math.sqrt(MLA_QK_DIM)
    for h in range(MLA_HEADS):
        sl = slice(h * MLA_PAD_DIM, (h + 1) * MLA_PAD_DIM)
        q_h = (q_lin[:, sl] * cos + q_rot[:, sl] * sin) * q_scale
        q_mla_ref[0, h] = q_h.astype(BF16)
        k_mla_ref[0, h] = (k_nope[:, sl] + k_rope).astype(BF16)
    v_mla_t = v_mla.T
    pad_row = lax.broadcasted_iota(jnp.int32, (MLA_VT_ROWS - MLA_V_DIM, SEQ_BLOCK), 0)
    ones_then_zeros = jnp.where(pad_row == 0, 1.0, 0.0).astype(BF16)
    for h in range(MLA_HEADS):
        for t in range(PROJ_BLOCKS):
            vt_mla_ref[0, h, t, :MLA_V_DIM, :] = v_mla_t[h * MLA_V_DIM:(h + 1) * MLA_V_DIM,
                                                         t * SEQ_BLOCK:(t + 1) * SEQ_BLOCK].astype(BF16)
            vt_mla_ref[0, h, t, MLA_V_DIM:, :] = ones_then_zeros

    sb_scale = LOG2_E / math.sqrt(SB_HEAD_DIM)
    q_sb = sb[:, :SB_WIDTH] * sb_scale
    k_sb = sb[:, SB_WIDTH:2 * SB_WIDTH]
    v_sb_t = sb[:, 2 * SB_WIDTH:].T
    for h in range(SB_HEADS):
        sl = slice(h * SB_HEAD_DIM, (h + 1) * SB_HEAD_DIM)
        q_sb_ref[0, h] = q_sb[:, sl].astype(BF16)
        k_sb_ref[0, h] = k_sb[:, sl].astype(BF16)
        for t in range(PROJ_BLOCKS):
            vt_sb_ref[0, h, t] = v_sb_t[sl, t * SEQ_BLOCK:(t + 1) * SEQ_BLOCK].astype(BF16)


def _block_iotas():
    key_idx = lax.broadcasted_iota(jnp.int32, (SEQ_BLOCK, SEQ_BLOCK), 0)
    qry_idx = lax.broadcasted_iota(jnp.int32, (SEQ_BLOCK, SEQ_BLOCK), 1)
    return key_idx, qry_idx


def _attn_kernel(q_ref, k_ref, vt_ref, q_sb_ref, k_sb_ref, vt_sb_ref, tri_ref, o_ref, o_sb_ref,
                 z_scr, w_scr, *, n_super):
    blk, n, k_per_step = SEQ_BLOCK, N_CHAINS, MLA_K_PER_STEP
    key_idx, qry_idx = _block_iotas()
    causal = key_idx <= qry_idx
    unit = jnp.ones((1, blk), F32)
    sb_stages, sb_tail = _sb_super_block(q_sb_ref, k_sb_ref, vt_sb_ref, tri_ref, o_sb_ref)

    def rows(ref, j):
        return ref[0, 0, pl.ds(pl.multiple_of(j * blk, blk), blk), :]

    def softmax_blocks(zs, ms, diag_chain):
        new_ms, ws, alphas = [], [], []
        for i, (s, m) in enumerate(zip(zs, ms)):
            if i == diag_chain:
                s = jnp.where(causal, s, -jnp.inf)
            s_max = jnp.max(s, axis=0, keepdims=True)
            if m is None:
                m_new, alpha = s_max, None
            else:
                m_new = jnp.maximum(m, s_max)
                alpha = jnp.exp2(m - m_new)
            new_ms.append(m_new)
            ws.append(jnp.exp2(s - m_new).astype(BF16))
            alphas.append(alpha)
        return new_ms, ws, alphas

    def rescale_add(acc, pv, alpha):
        return pv if acc is None else alpha * acc + pv

    def super_body(sb, carry):
        base = sb * n

        def scores(j, chains):
            k = rows(k_ref, j)
            return [_dot_nt(k, rows(q_ref, base + a)) for a in chains]

        ms, accs = [None] * n, [None] * n
        side, sb_box = sb_stages(base)
        assert len(side) == n + 1
        zs_of = {n - 1: scores(base + n - 1, [n - 1])}
        side.pop(0)()
        for kb in reversed(range(n - 1)):
            zs_of[kb] = scores(base + kb, range(kb, n))
        for a, z in enumerate(scores(jnp.maximum(base - 1, 0), range(n))):
            z_scr[0, a] = z
        pending = None
        for kb in reversed(range(n)):
            chains = list(range(kb, n))
            zs = zs_of[kb]
            if pending is not None:
                vt = vt_ref[0, 0, base + kb + 1]
                for a, w, alpha in zip(*pending):
                    accs[a] = rescale_add(accs[a], _dot(vt, w), alpha)
            new_ms, ws, alphas = softmax_blocks(zs, [ms[a] for a in chains], 0)
            for a, m in zip(chains, new_ms):
                ms[a] = m
            side.pop(0)()
            if kb > 0:
                pending = (chains, ws, alphas)
            else:
                for a in range(n):
                    w_scr[a] = ws[a]
                accs[0] = jnp.zeros((vt_ref.shape[3], blk), F32)
                alphas[0] = unit

        def k_body(it, state):
            ms, accs, alphas = (list(t) for t in state)
            j = base - 1 - k_per_step * it
            for u in range(k_per_step):
                slot, j_cur, j_next = u % 2, j - u, jnp.maximum(j - u - 1, 0)
                k_next, vt_prev = rows(k_ref, j_next), vt_ref[0, 0, j_cur + 1]
                for a in range(n):
                    z_scr[1 - slot, a] = _dot_nt(k_next, rows(q_ref, base + a))
                    accs[a] = rescale_add(accs[a], _dot(vt_prev, w_scr[a]), alphas[a])
                ms, ws, alphas = softmax_blocks([z_scr[slot, a] for a in range(n)], ms, None)
                for a in range(n):
                    w_scr[a] = ws[a]
            return tuple(ms), tuple(accs), tuple(alphas)

        ms, accs, alphas = lax.fori_loop(
            0, sb * (n // k_per_step), k_body, (tuple(ms), tuple(accs), tuple(alphas)))
        vt = vt_ref[0, 0, 0]
        accs = [rescale_add(acc, _dot(vt, w_scr[a]), alpha)
                for a, (acc, alpha) in enumerate(zip(accs, alphas))]
        for a in range(n):
            o_ref[0, 0, base + a] = accs[a][:MLA_V_DIM] / accs[a][MLA_V_DIM:MLA_V_DIM + 1]
        sb_tail(base, sb_box)
        return carry

    lax.fori_loop(0, n_super, super_body, 0)


def _softplus2(z):
    return jnp.maximum(z, 0.0) + jnp.log2(1.0 + jnp.exp2(-jnp.abs(z)))


def _sb_super_block(q_ref, k_ref, vt_ref, tri_ref, o_ref):
    blk, n = SEQ_BLOCK, N_CHAINS
    key_idx, qry_idx = _block_iotas()
    strict = key_idx < qry_idx

    def rows(ref, j):
        return ref[0, 0, pl.ds(pl.multiple_of(j * blk, blk), blk), :]

    def logits(base, specs):
        return [_dot_nt(rows(k_ref, j), rows(q_ref, base + a)) for a, j, _ in specs]

    def suffix_sums(zs, specs):
        log_betas, sps = [], []
        for z, (_, _, diag) in zip(zs, specs):
            sp = _softplus2(z)
            log_betas.append(z - sp)
            if diag:
                sp = jnp.where(strict, sp, 0.0)
            sps.append(sp.astype(BF16))
        return log_betas, sps, [_dot(tri_ref[...], sp) for sp in sps]

    def products(log_betas, sps, sufs, specs):
        ws = []
        for log_beta, suf, (_, _, diag) in zip(log_betas, sufs, specs):
            w = jnp.exp2(log_beta + suf)
            if diag:
                w = jnp.where(strict, w, 0.0)
            ws.append(w.astype(BF16))
        return [(suf[0:1, :] - sp[0:1, :].astype(F32), _dot(vt_ref[0, 0, j], w))
                for w, sp, suf, (_, j, _) in zip(ws, sps, sufs, specs)]

    def tiles(base, groups):
        zs = [logits(base, g) for g in groups]
        out, prev = [], None
        for g, z in zip(groups, zs):
            cur = suffix_sums(z, g) + (g,)
            if prev is not None:
                out += products(*prev)
            prev = cur
        return out + products(*prev)

    def band(base, d):
        return [(a, jnp.maximum(base + a - d, 0), False) for a in range(n)]

    def factor(base, a, d, c):
        return jnp.where(base + a - d >= 0, jnp.exp2(c), 0.0)

    def live(cs):
        return (jnp.max(functools.reduce(jnp.maximum, cs)) > -SB_DEAD_BITS).astype(jnp.int32)

    def stages(base):
        groups = [[(a, base + a, True) for a in range(n)], band(base, 1)]
        box = {}

        def s0():
            box["z"] = [logits(base, g) for g in groups]

        def s1():
            box["c0"] = suffix_sums(box["z"][0], groups[0])

        def s2():
            box["c1"] = suffix_sums(box["z"][1], groups[1])
            box["r0"] = products(*box["c0"], groups[0])

        def s3():
            box["r1"] = products(*box["c1"], groups[1])

        def s4():
            cs = []
            for a in range(n):
                (sum0, pv0), (sum1, pv1) = box["r0"][a], box["r1"][a]
                o_ref[0, 0, base + a] = pv0 + pv1 * factor(base, a, 1, sum0)
                cs.append(sum0 + sum1)
            box["carry"] = cs

        return [s0, s1, s2, s3, s4], box

    def tail(base, box):
        def w_body(state):
            d, _, cs = state
            res = tiles(base, [band(base, d)])
            for a, (c, (_, pv)) in enumerate(zip(cs, res)):
                o_ref[0, 0, base + a] = o_ref[0, 0, base + a] + pv * factor(base, a, d, c)
            cs = tuple(c + block_sum for c, (block_sum, _) in zip(cs, res))
            return d + 1, live(cs), cs

        cs = box["carry"]
        lax.while_loop(lambda state: jnp.logical_and(state[0] < base + n, state[1] > 0), w_body,
                       (jnp.int32(2), live(cs), tuple(cs)))

    return stages, tail


def _out_kernel(x_ref, o_mla_ref, o_sb_ref, g_mla_ref, g_sb_ref, w_o_mla_ref, w_o_sb_ref,
                g_ffn_ref, w_gate_ref, w_up_ref, w_down_ref, g_final_ref, out_ref, *, d_ff):
    blk = SEQ_BLOCK

    def group(o_ref, g_ref, w_ref):
        parts = []
        for t in range(OUT_BLOCKS):
            o = o_ref[0, :, t].reshape(-1, blk)
            y = (o * _rms_scale(o, 0) * g_ref[...]).astype(BF16)
            parts.append(_dot_tn(y, w_ref[...]))
        return jnp.concatenate(parts, axis=0)

    h = x_ref[0] + group(o_mla_ref, g_mla_ref, w_o_mla_ref) + group(o_sb_ref, g_sb_ref, w_o_sb_ref)
    f = (h * _rms_scale(h, -1) * g_ffn_ref[...]).astype(BF16)
    ffn = jnp.zeros_like(h)
    for c in range(0, d_ff, FF_CHUNK):
        gate = _dot(f, w_gate_ref[:, c:c + FF_CHUNK])
        up = _dot(f, w_up_ref[:, c:c + FF_CHUNK])
        act = (gate * jax.nn.sigmoid(gate) * up).astype(BF16)
        ffn = ffn + _dot(act, w_down_ref[c:c + FF_CHUNK, :])
    h = h + ffn
    out_ref[0] = h * _rms_scale(h, -1) * g_final_ref[...]


def _rotate_half_cols(w):
    half = w.shape[-1] // 2
    return jnp.concatenate([-w[..., half:], w[..., :half]], axis=-1)


def _head_slots(nope, rope):
    ref = nope if nope is not None else rope
    r, h = ref.shape[0], ref.shape[1]
    nope = jnp.zeros((r, h, MLA_NOPE_DIM), ref.dtype) if nope is None else nope
    rope = jnp.zeros((r, h, MLA_ROPE_DIM), ref.dtype) if rope is None else rope
    pad = jnp.zeros((r, h, MLA_PAD_DIM - MLA_QK_DIM), ref.dtype)
    return jnp.concatenate([nope, rope, pad], axis=-1).reshape(r, h * MLA_PAD_DIM)


def _const_spec(shape):
    return pl.BlockSpec(shape, lambda *_: (0,) * len(shape))


def kernel(x, positions, norm_mix, w_in, q_latent_norm, w_uq, kv_latent_norm, w_ukv,
           out_norm_mla, out_norm_sb, w_o, norm_ffn, w_gate, w_up, w_down, norm_final):
    b, s, d = x.shape
    depth = w_in.shape[0]
    d_ff = w_gate.shape[-1]
    blk = SEQ_BLOCK
    nb = s // blk
    assert s % (blk * N_CHAINS) == 0 and nb % PROJ_BLOCKS == 0 and nb % OUT_BLOCKS == 0
    assert N_CHAINS % MLA_K_PER_STEP == 0 and MLA_K_PER_STEP % 2 == 0 and d_ff % FF_CHUNK == 0

    inv_freq = ROPE_THETA ** (-jnp.arange(0, MLA_ROPE_DIM, 2, dtype=F32) / MLA_ROPE_DIM)
    half = MLA_ROPE_DIM // 2
    lane = jnp.arange(MLA_PAD_DIM)[None, :]
    on_rope = (lane >= MLA_NOPE_DIM) & (lane < MLA_QK_DIM)
    spread = (on_rope & ((lane - MLA_NOPE_DIM) % half == jnp.arange(half)[:, None])).astype(BF16)
    spread = jnp.tile(spread, (3, 1))
    off_rope = (~on_rope).astype(F32)
    tri = jnp.where(jnp.arange(blk)[None, :] > jnp.arange(blk)[:, None], -1.0, 0.0).astype(BF16)
    pos = positions.reshape(b, 1, s)

    params = pltpu.CompilerParams(
        dimension_semantics=("arbitrary", "arbitrary"), vmem_limit_bytes=VMEM_LIMIT_BYTES)

    h = x
    for l in range(depth):
        o0 = Q_LORA_RANK
        o1 = o0 + KV_LORA_RANK
        o2 = o1 + MLA_ROPE_DIM
        w_in_t = jnp.swapaxes(w_in[l], 0, 1)
        w_cq, w_ckv, w_kr, w_sb = w_in_t[:o0], w_in_t[o0:o1], w_in_t[o1:o2], w_in_t[o2:]
        kr_slot = lambda w: jnp.pad(w, ((MLA_NOPE_DIM, MLA_PAD_DIM - MLA_QK_DIM), (0, 0)))
        w_kr_rot = jnp.concatenate([-w_kr[MLA_ROPE_DIM // 2:], w_kr[:MLA_ROPE_DIM // 2]], axis=0)
        w_lat = jnp.concatenate([w_cq, w_ckv, kr_slot(w_kr), kr_slot(w_kr_rot)],
                                axis=0).astype(BF16)
        uq = w_uq[l].reshape(Q_LORA_RANK, MLA_HEADS, MLA_QK_DIM)
        uq_nope, uq_rope = uq[..., :MLA_NOPE_DIM], uq[..., MLA_NOPE_DIM:]
        w_uq_lin = _head_slots(uq_nope, uq_rope).astype(BF16)
        w_uq_rot = _head_slots(None, _rotate_half_cols(uq_rope)).astype(BF16)
        ukv = w_ukv[l].reshape(KV_LORA_RANK, MLA_HEADS, MLA_NOPE_DIM + MLA_V_DIM)
        w_uk = _head_slots(ukv[..., :MLA_NOPE_DIM], None).astype(BF16)
        w_uv = ukv[..., MLA_NOPE_DIM:].reshape(KV_LORA_RANK, MLA_WIDTH).astype(BF16)

        head_major = lambda width: jax.ShapeDtypeStruct((b, MLA_HEADS, s, width), BF16)
        tile = PROJ_BLOCKS * blk
        head_spec = lambda width: pl.BlockSpec((1, MLA_HEADS, tile, width), lambda bi, ti: (bi, 0, ti, 0))
        vt_shape = lambda rows: jax.ShapeDtypeStruct((b, MLA_HEADS, nb, rows, blk), BF16)
        vt_spec = lambda rows: pl.BlockSpec((1, MLA_HEADS, PROJ_BLOCKS, rows, blk),
                                            lambda bi, ti: (bi, 0, ti, 0, 0))
        proj_steps = nb // PROJ_BLOCKS
        n_slabs = b * proj_steps
        assert d % (n_slabs * BF16_SUBLANE_TILE) == 0 and d_ff % (n_slabs * BF16_SUBLANE_TILE) == 0
        slab_in = lambda rows, cols: pl.BlockSpec(
            (1, rows // n_slabs, cols), lambda bi, ti: (l, bi * proj_steps + ti, 0))
        slab_out = lambda rows, cols: pl.BlockSpec(
            (rows // n_slabs, cols), lambda bi, ti: (bi * proj_steps + ti, 0))
        (q_mla, k_mla, vt_mla, q_sb, k_sb, vt_sb,
         w_gate_bf, w_up_bf, w_down_bf) = pl.pallas_call(
            _proj_kernel,
            grid=(b, proj_steps),
            in_specs=[
                pl.BlockSpec((1, tile, d), lambda bi, ti: (bi, ti, 0)),
                pl.BlockSpec((1, 1, tile), lambda bi, ti: (bi, 0, ti)),
                _const_spec((half, 1)),
                _const_spec((3 * half, MLA_PAD_DIM)),
                _const_spec((1, MLA_PAD_DIM)),
                _const_spec((1, d)),
                _const_spec((1, Q_LORA_RANK)),
                _const_spec((1, KV_LORA_RANK)),
                _const_spec(w_lat.shape),
                _const_spec((3 * SB_WIDTH, d)),
                _const_spec(w_uq_lin.shape),
                _const_spec(w_uq_rot.shape),
                _const_spec(w_uk.shape),
                _const_spec(w_uv.shape),
                slab_in(d, d_ff), slab_in(d, d_ff), slab_in(d_ff, d),
            ],
            out_specs=[head_spec(MLA_PAD_DIM), head_spec(MLA_PAD_DIM), vt_spec(MLA_VT_ROWS),
                       head_spec(SB_HEAD_DIM), head_spec(SB_HEAD_DIM), vt_spec(SB_HEAD_DIM),
                       slab_out(d, d_ff), slab_out(d, d_ff), slab_out(d_ff, d)],
            out_shape=[head_major(MLA_PAD_DIM), head_major(MLA_PAD_DIM), vt_shape(MLA_VT_ROWS),
                       head_major(SB_HEAD_DIM), head_major(SB_HEAD_DIM), vt_shape(SB_HEAD_DIM),
                       jax.ShapeDtypeStruct((d, d_ff), BF16), jax.ShapeDtypeStruct((d, d_ff), BF16),
                       jax.ShapeDtypeStruct((d_ff, d), BF16)],
            compiler_params=params,
            name="proj",
        )(h, pos, inv_freq[:, None], spread, off_rope,
          norm_mix[l][None, :], q_latent_norm[l][None, :], kv_latent_norm[l][None, :],
          w_lat, w_sb.astype(BF16), w_uq_lin, w_uq_rot, w_uk, w_uv, w_gate, w_up, w_down)

        seq_spec = lambda width: pl.BlockSpec((1, 1, s, width), lambda bi, hi: (bi, hi, 0, 0))
        blocked = lambda rows: pl.BlockSpec((1, 1, nb, rows, blk), lambda bi, hi: (bi, hi, 0, 0, 0))
        o_shape = jax.ShapeDtypeStruct((b, MLA_HEADS, nb, MLA_V_DIM, blk), F32)
        score_scratch = [pltpu.VMEM((2, N_CHAINS, blk, blk), F32),
                         pltpu.VMEM((N_CHAINS, blk, blk), BF16)]
        assert MLA_HEADS == SB_HEADS
        o_mla, o_sb = pl.pallas_call(
            functools.partial(_attn_kernel, n_super=nb // N_CHAINS),
            grid=(b, MLA_HEADS),
            in_specs=[seq_spec(MLA_PAD_DIM), seq_spec(MLA_PAD_DIM), blocked(MLA_VT_ROWS),
                      seq_spec(SB_HEAD_DIM), seq_spec(SB_HEAD_DIM), blocked(SB_HEAD_DIM),
                      _const_spec((blk, blk))],
            out_specs=[blocked(MLA_V_DIM), blocked(SB_HEAD_DIM)],
            out_shape=[o_shape, o_shape],
            scratch_shapes=score_scratch,
            compiler_params=params,
            name="attn",
        )(q_mla, k_mla, vt_mla, q_sb, k_sb, vt_sb, tri)

        assert depth == 1
        o_spec = pl.BlockSpec((1, MLA_HEADS, OUT_BLOCKS, MLA_V_DIM, blk), lambda bi, ti: (bi, 0, ti, 0, 0))
        tok_spec = pl.BlockSpec((1, OUT_BLOCKS * blk, d), lambda bi, ti: (bi, ti, 0))
        resident = lambda shape: pl.BlockSpec(shape, lambda *_: (0,) * len(shape),
                                              pipeline_mode=pl.Buffered(1))
        h = pl.pallas_call(
            functools.partial(_out_kernel, d_ff=d_ff),
            grid=(b, nb // OUT_BLOCKS),
            in_specs=[
                tok_spec, o_spec, o_spec,
                _const_spec((MLA_WIDTH, 1)), _const_spec((SB_WIDTH, 1)),
                resident((MLA_WIDTH, d)), resident((SB_WIDTH, d)),
                _const_spec((1, d)),
                resident((d, d_ff)), resident((d, d_ff)), resident((d_ff, d)),
                _const_spec((1, d)),
            ],
            out_specs=tok_spec,
            out_shape=jax.ShapeDtypeStruct((b, s, d), F32),
            compiler_params=params,
            name="out_ffn",
        )(h, o_mla, o_sb, out_norm_mla[l][:, None], out_norm_sb[l][:, None],
          w_o[l][:MLA_WIDTH].astype(BF16), w_o[l][MLA_WIDTH:].astype(BF16),
          norm_ffn[l][None, :], w_gate_bf, w_up_bf, w_down_bf, norm_final[None, :])
    return h
```

```python
import functools
import math

import jax
import jax.numpy as jnp
from jax import lax
from jax.experimental import pallas as pl
from jax.experimental.pallas import tpu as pltpu

EPS = 1e-6
ROPE_THETA = 10000.0
LOG2_E = 1.4426950408889634

LANES = 128
BF16_SUBLANE_TILE = 16
MXU_DIM_V7X = 256
VMEM_BYTES_V7X = 64 * 1024 * 1024

MLA_HEADS = 8
MLA_NOPE_DIM = 64
MLA_ROPE_DIM = 32
MLA_V_DIM = 64
MLA_QK_DIM = MLA_NOPE_DIM + MLA_ROPE_DIM
MLA_PAD_DIM = LANES
MLA_VT_ROWS = MLA_V_DIM + BF16_SUBLANE_TILE
Q_LORA_RANK = 256
KV_LORA_RANK = 128
SB_HEADS = 8
SB_HEAD_DIM = 64
MLA_WIDTH = MLA_HEADS * MLA_V_DIM
SB_WIDTH = SB_HEADS * SB_HEAD_DIM

SEQ_BLOCK = MXU_DIM_V7X
N_CHAINS = 4
MLA_K_PER_STEP = 4
SB_DEAD_BITS = 160.0
FF_CHUNK = MXU_DIM_V7X
OUT_BLOCKS = 2
PROJ_BLOCKS = 4
VMEM_LIMIT_BYTES = VMEM_BYTES_V7X * 7 // 8

F32 = jnp.float32
BF16 = jnp.bfloat16


def _rms_scale(v, axis):
    return lax.rsqrt(jnp.mean(v * v, axis=axis, keepdims=True) + EPS)


def _dot(a, b):
    return jnp.dot(a, b, preferred_element_type=F32)


def _dot_nt(a, b):
    return lax.dot_general(a, b, (((1,), (1,)), ((), ())), preferred_element_type=F32)


def _dot_tn(a, b):
    return lax.dot_general(a, b, (((0,), (0,)), ((), ())), preferred_element_type=F32)


def _proj_kernel(x_ref, pos_ref, freq_ref, spread_ref, off_rope_ref, g_mix_ref, g_q_ref, g_kv_ref,
                 w_lat_ref, w_sb_ref, w_uq_ref, w_uq_rot_ref, w_uk_ref, w_uv_ref,
                 w_gate_ref, w_up_ref, w_down_ref,
                 q_mla_ref, k_mla_ref, vt_mla_ref, q_sb_ref, k_sb_ref, vt_sb_ref,
                 w_gate_bf_ref, w_up_bf_ref, w_down_bf_ref):
    w_gate_bf_ref[...] = w_gate_ref[0].astype(BF16)
    w_up_bf_ref[...] = w_up_ref[0].astype(BF16)
    w_down_bf_ref[...] = w_down_ref[0].astype(BF16)

    x = x_ref[0]
    u = (x * _rms_scale(x, -1) * g_mix_ref[...]).astype(BF16)

    ang_t = freq_ref[...] * pos_ref[0].astype(F32)
    cos_t = jnp.cos(ang_t)
    sin_t = jnp.sin(ang_t)

    lat = _dot_nt(u, w_lat_ref[...])
    sb = _dot_nt(u, w_sb_ref[...])
    def spread(t):
        t1 = t.astype(BF16)
        r1 = t - t1.astype(F32)
        t2 = r1.astype(BF16)
        t3 = (r1 - t2.astype(F32)).astype(BF16)
        return _dot_tn(jnp.concatenate([t1, t2, t3], axis=0), spread_ref[...])

    cos = spread(cos_t) + off_rope_ref[...]
    sin = spread(sin_t)
    c_q = lat[:, :Q_LORA_RANK]
    c_kv = lat[:, Q_LORA_RANK:Q_LORA_RANK + KV_LORA_RANK]
    k_r = lat[:, Q_LORA_RANK + KV_LORA_RANK:Q_LORA_RANK + KV_LORA_RANK + MLA_PAD_DIM]
    k_r_rot = lat[:, Q_LORA_RANK + KV_LORA_RANK + MLA_PAD_DIM:]
    k_rope = k_r * cos + k_r_rot * sin

    ql = (c_q * _rms_scale(c_q, -1) * g_q_ref[...]).astype(BF16)
    kvl = (c_kv * _rms_scale(c_kv, -1) * g_kv_ref[...]).astype(BF16)

    q_lin = _dot(ql, w_uq_ref[...])
    q_rot = _dot(ql, w_uq_rot_ref[...])
    k_nope = _dot(kvl, w_uk_ref[...])
    v_mla = _dot(kvl, w_uv_ref[...])
    q_scale = LOG2_E / math.sqrt(MLA_QK_DIM)
    for h in range(MLA_HEADS):
        sl = slice(h * MLA_PAD_DIM, (h + 1) * MLA_PAD_DIM)
        q_h = (q_lin[:, sl] * cos + q_rot[:, sl] * sin) * q_scale
        q_mla_ref[0, h] = q_h.astype(BF16)
        k_mla_ref[0, h] = (k_nope[:, sl] + k_rope).astype(BF16)
    v_mla_t = v_mla.T
    pad_row = lax.broadcasted_iota(jnp.int32, (MLA_VT_ROWS - MLA_V_DIM, SEQ_BLOCK), 0)
    ones_then_zeros = jnp.where(pad_row == 0, 1.0, 0.0).astype(BF16)
    for h in range(MLA_HEADS):
        for t in range(PROJ_BLOCKS):
            vt_mla_ref[0, h, t, :MLA_V_DIM, :] = v_mla_t[h * MLA_V_DIM:(h + 1) * MLA_V_DIM,
                                                         t * SEQ_BLOCK:(t + 1) * SEQ_BLOCK].astype(BF16)
            vt_mla_ref[0, h, t, MLA_V_DIM:, :] = ones_then_zeros

    sb_scale = LOG2_E / math.sqrt(SB_HEAD_DIM)
    q_sb = sb[:, :SB_WIDTH] * sb_scale
    k_sb = sb[:, SB_WIDTH:2 * SB_WIDTH]
    v_sb_t = sb[:, 2 * SB_WIDTH:].T
    for h in range(SB_HEADS):
        sl = slice(h * SB_HEAD_DIM, (h + 1) * SB_HEAD_DIM)
        q_sb_ref[0, h] = q_sb[:, sl].astype(BF16)
        k_sb_ref[0, h] = k_sb[:, sl].astype(BF16)
        for t in range(PROJ_BLOCKS):
            vt_sb_ref[0, h, t] = v_sb_t[sl, t * SEQ_BLOCK:(t + 1) * SEQ_BLOCK].astype(BF16)


def _block_iotas():
    key_idx = lax.broadcasted_iota(jnp.int32, (SEQ_BLOCK, SEQ_BLOCK), 0)
    qry_idx = lax.broadcasted_iota(jnp.int32, (SEQ_BLOCK, SEQ_BLOCK), 1)
    return key_idx, qry_idx


def _attn_kernel(q_ref, k_ref, vt_ref, q_sb_ref, k_sb_ref, vt_sb_ref, tri_ref, o_ref, o_sb_ref,
                 z_scr, w_scr, *, n_super):
    blk, n, k_per_step = SEQ_BLOCK, N_CHAINS, MLA_K_PER_STEP
    key_idx, qry_idx = _block_iotas()
    causal = key_idx <= qry_idx
    unit = jnp.ones((1, blk), F32)
    sb_stages, sb_tail = _sb_super_block(q_sb_ref, k_sb_ref, vt_sb_ref, tri_ref, o_sb_ref)

    def rows(ref, j):
        return ref[0, 0, pl.ds(pl.multiple_of(j * blk, blk), blk), :]

    def softmax_blocks(zs, ms, diag_chain):
        new_ms, ws, alphas = [], [], []
        for i, (s, m) in enumerate(zip(zs, ms)):
            if i == diag_chain:
                s = jnp.where(causal, s, -jnp.inf)
            s_max = jnp.max(s, axis=0, keepdims=True)
            if m is None:
                m_new, alpha = s_max, None
            else:
                m_new = jnp.maximum(m, s_max)
                alpha = jnp.exp2(m - m_new)
            new_ms.append(m_new)
            ws.append(jnp.exp2(s - m_new).astype(BF16))
            alphas.append(alpha)
        return new_ms, ws, alphas

    def rescale_add(acc, pv, alpha):
        return pv if acc is None else alpha * acc + pv

    def super_body(sb, carry):
        base = sb * n

        def scores(j, chains):
            k = rows(k_ref, j)
            return [_dot_nt(k, rows(q_ref, base + a)) for a in chains]

        ms, accs = [None] * n, [None] * n
        side, sb_box = sb_stages(base)
        assert len(side) == n + 1
        zs_of = {n - 1: scores(base + n - 1, [n - 1])}
        side.pop(0)()
        for kb in reversed(range(n - 1)):
            zs_of[kb] = scores(base + kb, range(kb, n))
        for a, z in enumerate(scores(jnp.maximum(base - 1, 0), range(n))):
            z_scr[0, a] = z
        pending = None
        for kb in reversed(range(n)):
            chains = list(range(kb, n))
            zs = zs_of[kb]
            if pending is not None:
                vt = vt_ref[0, 0, base + kb + 1]
                for a, w, alpha in zip(*pending):
                    accs[a] = rescale_add(accs[a], _dot(vt, w), alpha)
            new_ms, ws, alphas = softmax_blocks(zs, [ms[a] for a in chains], 0)
            for a, m in zip(chains, new_ms):
                ms[a] = m
            side.pop(0)()
            if kb > 0:
                pending = (chains, ws, alphas)
            else:
                for a in range(n):
                    w_scr[a] = ws[a]
                accs[0] = jnp.zeros((vt_ref.shape[3], blk), F32)
                alphas[0] = unit

        def k_body(it, state):
            ms, accs, alphas = (list(t) for t in state)
            j = base - 1 - k_per_step * it
            for u in range(k_per_step):
                slot, j_cur, j_next = u, j - u, jnp.maximum(j - u - 1, 0)
                k_next, vt_prev = rows(k_ref, j_next), vt_ref[0, 0, j_cur + 1]
                for a in range(n):
                    z_scr[(slot + 1) % k_per_step, a] = _dot_nt(k_next, rows(q_ref, base + a))
                    accs[a] = rescale_add(accs[a], _dot(vt_prev, w_scr[a]), alphas[a])
                ms, ws, alphas = softmax_blocks([z_scr[slot, a] for a in range(n)], ms, None)
                for a in range(n):
                    w_scr[a] = ws[a]
            return tuple(ms), tuple(accs), tuple(alphas)

        ms, accs, alphas = lax.fori_loop(
            0, sb * (n // k_per_step), k_body, (tuple(ms), tuple(accs), tuple(alphas)))
        vt = vt_ref[0, 0, 0]
        accs = [rescale_add(acc, _dot(vt, w_scr[a]), alpha)
                for a, (acc, alpha) in enumerate(zip(accs, alphas))]
        for a in range(n):
            o_ref[0, 0, base + a] = accs[a][:MLA_V_DIM] / accs[a][MLA_V_DIM:MLA_V_DIM + 1]
        sb_tail(base, sb_box)
        return carry

    lax.fori_loop(0, n_super, super_body, 0)


def _softplus2(z):
    return jnp.maximum(z, 0.0) + jnp.log2(1.0 + jnp.exp2(-jnp.abs(z)))


def _sb_super_block(q_ref, k_ref, vt_ref, tri_ref, o_ref):
    blk, n = SEQ_BLOCK, N_CHAINS
    key_idx, qry_idx = _block_iotas()
    strict = key_idx < qry_idx

    def rows(ref, j):
        return ref[0, 0, pl.ds(pl.multiple_of(j * blk, blk), blk), :]

    def logits(base, specs):
        return [_dot_nt(rows(k_ref, j), rows(q_ref, base + a)) for a, j, _ in specs]

    def suffix_sums(zs, specs):
        log_betas, sps = [], []
        for z, (_, _, diag) in zip(zs, specs):
            sp = _softplus2(z)
            log_betas.append(z - sp)
            if diag:
                sp = jnp.where(strict, sp, 0.0)
            sps.append(sp.astype(BF16))
        return log_betas, sps, [_dot(tri_ref[...], sp) for sp in sps]

    def products(log_betas, sps, sufs, specs):
        ws = []
        for log_beta, suf, (_, _, diag) in zip(log_betas, sufs, specs):
            w = jnp.exp2(log_beta + suf)
            if diag:
                w = jnp.where(strict, w, 0.0)
            ws.append(w.astype(BF16))
        return [(suf[0:1, :] - sp[0:1, :].astype(F32), _dot(vt_ref[0, 0, j], w))
                for w, sp, suf, (_, j, _) in zip(ws, sps, sufs, specs)]

    def tiles(base, groups):
        zs = [logits(base, g) for g in groups]
        out, prev = [], None
        for g, z in zip(groups, zs):
            cur = suffix_sums(z, g) + (g,)
            if prev is not None:
                out += products(*prev)
            prev = cur
        return out + products(*prev)

    def band(base, d):
        return [(a, jnp.maximum(base + a - d, 0), False) for a in range(n)]

    def factor(base, a, d, c):
        return jnp.where(base + a - d >= 0, jnp.exp2(c), 0.0)

    def live(cs):
        return (jnp.max(functools.reduce(jnp.maximum, cs)) > -SB_DEAD_BITS).astype(jnp.int32)

    def stages(base):
        groups = [[(a, base + a, True) for a in range(n)], band(base, 1)]
        box = {}

        def s0():
            box["z"] = [logits(base, g) for g in groups]

        def s1():
            box["c0"] = suffix_sums(box["z"][0], groups[0])

        def s2():
            box["c1"] = suffix_sums(box["z"][1], groups[1])
            box["r0"] = products(*box["c0"], groups[0])

        def s3():
            box["r1"] = products(*box["c1"], groups[1])

        def s4():
            cs = []
            for a in range(n):
                (sum0, pv0), (sum1, pv1) = box["r0"][a], box["r1"][a]
                o_ref[0, 0, base + a] = pv0 + pv1 * factor(base, a, 1, sum0)
                cs.append(sum0 + sum1)
            box["carry"] = cs

        return [s0, s1, s2, s3, s4], box

    def tail(base, box):
        def w_body(state):
            d, _, cs = state
            res = tiles(base, [band(base, d)])
            for a, (c, (_, pv)) in enumerate(zip(cs, res)):
                o_ref[0, 0, base + a] = o_ref[0, 0, base + a] + pv * factor(base, a, d, c)
            cs = tuple(c + block_sum for c, (block_sum, _) in zip(cs, res))
            return d + 1, live(cs), cs

        cs = box["carry"]
        lax.while_loop(lambda state: jnp.logical_and(state[0] < base + n, state[1] > 0), w_body,
                       (jnp.int32(2), live(cs), tuple(cs)))

    return stages, tail


def _out_kernel(x_ref, o_mla_ref, o_sb_ref, g_mla_ref, g_sb_ref, w_o_mla_ref, w_o_sb_ref,
                g_ffn_ref, w_gate_ref, w_up_ref, w_down_ref, g_final_ref, out_ref, *, d_ff):
    blk = SEQ_BLOCK

    def group(o_ref, g_ref, w_ref):
        parts = []
        for t in range(OUT_BLOCKS):
            o = o_ref[0, :, t].reshape(-1, blk)
            y = (o * _rms_scale(o, 0) * g_ref[...]).astype(BF16)
            parts.append(_dot_tn(y, w_ref[...]))
        return jnp.concatenate(parts, axis=0)

    h = x_ref[0] + group(o_mla_ref, g_mla_ref, w_o_mla_ref) + group(o_sb_ref, g_sb_ref, w_o_sb_ref)
    f = (h * _rms_scale(h, -1) * g_ffn_ref[...]).astype(BF16)
    ffn = jnp.zeros_like(h)
    for c in range(0, d_ff, FF_CHUNK):
        gate = _dot(f, w_gate_ref[:, c:c + FF_CHUNK])
        up = _dot(f, w_up_ref[:, c:c + FF_CHUNK])
        act = (gate * jax.nn.sigmoid(gate) * up).astype(BF16)
        ffn = ffn + _dot(act, w_down_ref[c:c + FF_CHUNK, :])
    h = h + ffn
    out_ref[0] = h * _rms_scale(h, -1) * g_final_ref[...]


def _rotate_half_cols(w):
    half = w.shape[-1] // 2
    return jnp.concatenate([-w[..., half:], w[..., :half]], axis=-1)


def _head_slots(nope, rope):
    ref = nope if nope is not None else rope
    r, h = ref.shape[0], ref.shape[1]
    nope = jnp.zeros((r, h, MLA_NOPE_DIM), ref.dtype) if nope is None else nope
    rope = jnp.zeros((r, h, MLA_ROPE_DIM), ref.dtype) if rope is None else rope
    pad = jnp.zeros((r, h, MLA_PAD_DIM - MLA_QK_DIM), ref.dtype)
    return jnp.concatenate([nope, rope, pad], axis=-1).reshape(r, h * MLA_PAD_DIM)


def _const_spec(shape):
    return pl.BlockSpec(shape, lambda *_: (0,) * len(shape))


def kernel(x, positions, norm_mix, w_in, q_latent_norm, w_uq, kv_latent_norm, w_ukv,
           out_norm_mla, out_norm_sb, w_o, norm_ffn, w_gate, w_up, w_down, norm_final):
    b, s, d = x.shape
    depth = w_in.shape[0]
    d_ff = w_gate.shape[-1]
    blk = SEQ_BLOCK
    nb = s // blk
    assert s % (blk * N_CHAINS) == 0 and nb % PROJ_BLOCKS == 0 and nb % OUT_BLOCKS == 0
    assert N_CHAINS % MLA_K_PER_STEP == 0 and d_ff % FF_CHUNK == 0

    inv_freq = ROPE_THETA ** (-jnp.arange(0, MLA_ROPE_DIM, 2, dtype=F32) / MLA_ROPE_DIM)
    half = MLA_ROPE_DIM // 2
    lane = jnp.arange(MLA_PAD_DIM)[None, :]
    on_rope = (lane >= MLA_NOPE_DIM) & (lane < MLA_QK_DIM)
    spread = (on_rope & ((lane - MLA_NOPE_DIM) % half == jnp.arange(half)[:, None])).astype(BF16)
    spread = jnp.tile(spread, (3, 1))
    off_rope = (~on_rope).astype(F32)
    tri = jnp.where(jnp.arange(blk)[None, :] > jnp.arange(blk)[:, None], -1.0, 0.0).astype(BF16)
    pos = positions.reshape(b, 1, s)

    params = pltpu.CompilerParams(
        dimension_semantics=("arbitrary", "arbitrary"), vmem_limit_bytes=VMEM_LIMIT_BYTES)

    h = x
    for l in range(depth):
        o0 = Q_LORA_RANK
        o1 = o0 + KV_LORA_RANK
        o2 = o1 + MLA_ROPE_DIM
        w_in_t = jnp.swapaxes(w_in[l], 0, 1)
        w_cq, w_ckv, w_kr, w_sb = w_in_t[:o0], w_in_t[o0:o1], w_in_t[o1:o2], w_in_t[o2:]
        kr_slot = lambda w: jnp.pad(w, ((MLA_NOPE_DIM, MLA_PAD_DIM - MLA_QK_DIM), (0, 0)))
        w_kr_rot = jnp.concatenate([-w_kr[MLA_ROPE_DIM // 2:], w_kr[:MLA_ROPE_DIM // 2]], axis=0)
        w_lat = jnp.concatenate([w_cq, w_ckv, kr_slot(w_kr), kr_slot(w_kr_rot)],
                                axis=0).astype(BF16)
        uq = w_uq[l].reshape(Q_LORA_RANK, MLA_HEADS, MLA_QK_DIM)
        uq_nope, uq_rope = uq[..., :MLA_NOPE_DIM], uq[..., MLA_NOPE_DIM:]
        w_uq_lin = _head_slots(uq_nope, uq_rope).astype(BF16)
        w_uq_rot = _head_slots(None, _rotate_half_cols(uq_rope)).astype(BF16)
        ukv = w_ukv[l].reshape(KV_LORA_RANK, MLA_HEADS, MLA_NOPE_DIM + MLA_V_DIM)
        w_uk = _head_slots(ukv[..., :MLA_NOPE_DIM], None).astype(BF16)
        w_uv = ukv[..., MLA_NOPE_DIM:].reshape(KV_LORA_RANK, MLA_WIDTH).astype(BF16)

        head_major = lambda width: jax.ShapeDtypeStruct((b, MLA_HEADS, s, width), BF16)
        tile = PROJ_BLOCKS * blk
        head_spec = lambda width: pl.BlockSpec((1, MLA_HEADS, tile, width), lambda bi, ti: (bi, 0, ti, 0))
        vt_shape = lambda rows: jax.ShapeDtypeStruct((b, MLA_HEADS, nb, rows, blk), BF16)
        vt_spec = lambda rows: pl.BlockSpec((1, MLA_HEADS, PROJ_BLOCKS, rows, blk),
                                            lambda bi, ti: (bi, 0, ti, 0, 0))
        proj_steps = nb // PROJ_BLOCKS
        n_slabs = b * proj_steps
        assert d % (n_slabs * BF16_SUBLANE_TILE) == 0 and d_ff % (n_slabs * BF16_SUBLANE_TILE) == 0
        slab_in = lambda rows, cols: pl.BlockSpec(
            (1, rows // n_slabs, cols), lambda bi, ti: (l, bi * proj_steps + ti, 0))
        slab_out = lambda rows, cols: pl.BlockSpec(
            (rows // n_slabs, cols), lambda bi, ti: (bi * proj_steps + ti, 0))
        (q_mla, k_mla, vt_mla, q_sb, k_sb, vt_sb,
         w_gate_bf, w_up_bf, w_down_bf) = pl.pallas_call(
            _proj_kernel,
            grid=(b, proj_steps),
            in_specs=[
                pl.BlockSpec((1, tile, d), lambda bi, ti: (bi, ti, 0)),
                pl.BlockSpec((1, 1, tile), lambda bi, ti: (bi, 0, ti)),
                _const_spec((half, 1)),
                _const_spec((3 * half, MLA_PAD_DIM)),
                _const_spec((1, MLA_PAD_DIM)),
                _const_spec((1, d)),
                _const_spec((1, Q_LORA_RANK)),
                _const_spec((1, KV_LORA_RANK)),
                _const_spec(w_lat.shape),
                _const_spec((3 * SB_WIDTH, d)),
                _const_spec(w_uq_lin.shape),
                _const_spec(w_uq_rot.shape),
                _const_spec(w_uk.shape),
                _const_spec(w_uv.shape),
                slab_in(d, d_ff), slab_in(d, d_ff), slab_in(d_ff, d),
            ],
            out_specs=[head_spec(MLA_PAD_DIM), head_spec(MLA_PAD_DIM), vt_spec(MLA_VT_ROWS),
                       head_spec(SB_HEAD_DIM), head_spec(SB_HEAD_DIM), vt_spec(SB_HEAD_DIM),
                       slab_out(d, d_ff), slab_out(d, d_ff), slab_out(d_ff, d)],
            out_shape=[head_major(MLA_PAD_DIM), head_major(MLA_PAD_DIM), vt_shape(MLA_VT_ROWS),
                       head_major(SB_HEAD_DIM), head_major(SB_HEAD_DIM), vt_shape(SB_HEAD_DIM),
                       jax.ShapeDtypeStruct((d, d_ff), BF16), jax.ShapeDtypeStruct((d, d_ff), BF16),
                       jax.ShapeDtypeStruct((d_ff, d), BF16)],
            compiler_params=params,
            name="proj",
        )(h, pos, inv_freq[:, None], spread, off_rope,
          norm_mix[l][None, :], q_latent_norm[l][None, :], kv_latent_norm[l][None, :],
          w_lat, w_sb.astype(BF16), w_uq_lin, w_uq_rot, w_uk, w_uv, w_gate, w_up, w_down)

        seq_spec = lambda width: pl.BlockSpec((1, 1, s, width), lambda bi, hi: (bi, hi, 0, 0))
        blocked = lambda rows: pl.BlockSpec((1, 1, nb, rows, blk), lambda bi, hi: (bi, hi, 0, 0, 0))
        o_shape = jax.ShapeDtypeStruct((b, MLA_HEADS, nb, MLA_V_DIM, blk), F32)
        score_scratch = [pltpu.VMEM((MLA_K_PER_STEP, N_CHAINS, blk, blk), F32),
                         pltpu.VMEM((N_CHAINS, blk, blk), BF16)]
        assert MLA_HEADS == SB_HEADS
        o_mla, o_sb = pl.pallas_call(
            functools.partial(_attn_kernel, n_super=nb // N_CHAINS),
            grid=(b, MLA_HEADS),
            in_specs=[seq_spec(MLA_PAD_DIM), seq_spec(MLA_PAD_DIM), blocked(MLA_VT_ROWS),
                      seq_spec(SB_HEAD_DIM), seq_spec(SB_HEAD_DIM), blocked(SB_HEAD_DIM),
                      _const_spec((blk, blk))],
            out_specs=[blocked(MLA_V_DIM), blocked(SB_HEAD_DIM)],
            out_shape=[o_shape, o_shape],
            scratch_shapes=score_scratch,
            compiler_params=params,
            name="attn",
        )(q_mla, k_mla, vt_mla, q_sb, k_sb, vt_sb, tri)

        assert depth == 1
        o_spec = pl.BlockSpec((1, MLA_HEADS, OUT_BLOCKS, MLA_V_DIM, blk), lambda bi, ti: (bi, 0, ti, 0, 0))
        tok_spec = pl.BlockSpec((1, OUT_BLOCKS * blk, d), lambda bi, ti: (bi, ti, 0))
        resident = lambda shape: pl.BlockSpec(shape, lambda *_: (0,) * len(shape),
                                              pipeline_mode=pl.Buffered(1))
        h = pl.pallas_call(
            functools.partial(_out_kernel, d_ff=d_ff),
            grid=(b, nb // OUT_BLOCKS),
            in_specs=[
                tok_spec, o_spec, o_spec,
                _const_spec((MLA_WIDTH, 1)), _const_spec((SB_WIDTH, 1)),
                resident((MLA_WIDTH, d)), resident((SB_WIDTH, d)),
                _const_spec((1, d)),
                resident((d, d_ff)), resident((d, d_ff)), resident((d_ff, d)),
                _const_spec((1, d)),
            ],
            out_specs=tok_spec,
            out_shape=jax.ShapeDtypeStruct((b, s, d), F32),
            compiler_params=params,
            name="out_ffn",
        )(h, o_mla, o_sb, out_norm_mla[l][:, None], out_norm_sb[l][:, None],
          w_o[l][:MLA_WIDTH].astype(BF16), w_o[l][MLA_WIDTH:].astype(BF16),
          norm_ffn[l][None, :], w_gate_bf, w_up_bf, w_down_bf, norm_final[None, :])
    return h
```

```python
import functools
import math

import jax
import jax.numpy as jnp
from jax import lax
from jax.experimental import pallas as pl
from jax.experimental.pallas import tpu as pltpu

EPS = 1e-6
ROPE_THETA = 10000.0
LOG2_E = 1.4426950408889634

LANES = 128
BF16_SUBLANE_TILE = 16
MXU_DIM_V7X = 256
VMEM_BYTES_V7X = 64 * 1024 * 1024

MLA_HEADS = 8
MLA_NOPE_DIM = 64
MLA_ROPE_DIM = 32
MLA_V_DIM = 64
MLA_QK_DIM = MLA_NOPE_DIM + MLA_ROPE_DIM
MLA_PAD_DIM = LANES
MLA_VT_ROWS = MLA_V_DIM + BF16_SUBLANE_TILE
Q_LORA_RANK = 256
KV_LORA_RANK = 128
SB_HEADS = 8
SB_HEAD_DIM = 64
MLA_WIDTH = MLA_HEADS * MLA_V_DIM
SB_WIDTH = SB_HEADS * SB_HEAD_DIM

SEQ_BLOCK = MXU_DIM_V7X
N_CHAINS = 4
MLA_K_PER_STEP = 4
SB_DEAD_BITS = 160.0
FF_CHUNK = MXU_DIM_V7X
OUT_BLOCKS = 4
PROJ_BLOCKS = 4
VMEM_LIMIT_BYTES = VMEM_BYTES_V7X * 7 // 8

F32 = jnp.float32
BF16 = jnp.bfloat16


def _rms_scale(v, axis):
    return lax.rsqrt(jnp.mean(v * v, axis=axis, keepdims=True) + EPS)


def _dot(a, b):
    return jnp.dot(a, b, preferred_element_type=F32)


def _dot_nt(a, b):
    return lax.dot_general(a, b, (((1,), (1,)), ((), ())), preferred_element_type=F32)


def _dot_tn(a, b):
    return lax.dot_general(a, b, (((0,), (0,)), ((), ())), preferred_element_type=F32)


def _proj_kernel(x_ref, pos_ref, freq_ref, spread_ref, off_rope_ref, g_mix_ref, g_q_ref, g_kv_ref,
                 w_lat_ref, w_sb_ref, w_uq_ref, w_uq_rot_ref, w_uk_ref, w_uv_ref,
                 w_gate_ref, w_up_ref, w_down_ref,
                 q_mla_ref, k_mla_ref, vt_mla_ref, q_sb_ref, k_sb_ref, vt_sb_ref,
                 w_gate_bf_ref, w_up_bf_ref, w_down_bf_ref):
    w_gate_bf_ref[...] = w_gate_ref[0].astype(BF16)
    w_up_bf_ref[...] = w_up_ref[0].astype(BF16)
    w_down_bf_ref[...] = w_down_ref[0].astype(BF16)

    x = x_ref[0]
    u = (x * _rms_scale(x, -1) * g_mix_ref[...]).astype(BF16)

    ang_t = freq_ref[...] * pos_ref[0].astype(F32)
    cos_t = jnp.cos(ang_t)
    sin_t = jnp.sin(ang_t)

    lat = _dot_nt(u, w_lat_ref[...])
    sb = _dot_nt(u, w_sb_ref[...])
    def spread(t):
        t1 = t.astype(BF16)
        r1 = t - t1.astype(F32)
        t2 = r1.astype(BF16)
        t3 = (r1 - t2.astype(F32)).astype(BF16)
        return _dot_tn(jnp.concatenate([t1, t2, t3], axis=0), spread_ref[...])

    cos = spread(cos_t) + off_rope_ref[...]
    sin = spread(sin_t)
    c_q = lat[:, :Q_LORA_RANK]
    c_kv = lat[:, Q_LORA_RANK:Q_LORA_RANK + KV_LORA_RANK]
    k_r = lat[:, Q_LORA_RANK + KV_LORA_RANK:Q_LORA_RANK + KV_LORA_RANK + MLA_PAD_DIM]
    k_r_rot = lat[:, Q_LORA_RANK + KV_LORA_RANK + MLA_PAD_DIM:]
    k_rope = k_r * cos + k_r_rot * sin

    ql = (c_q * _rms_scale(c_q, -1) * g_q_ref[...]).astype(BF16)
    kvl = (c_kv * _rms_scale(c_kv, -1) * g_kv_ref[...]).astype(BF16)

    q_lin = _dot(ql, w_uq_ref[...])
    q_rot = _dot(ql, w_uq_rot_ref[...])
    k_nope = _dot(kvl, w_uk_ref[...])
    v_mla = _dot(kvl, w_uv_ref[...])
    q_scale = LOG2_E / math.sqrt(MLA_QK_DIM)
    for h in range(MLA_HEADS):
        sl = slice(h * MLA_PAD_DIM, (h + 1) * MLA_PAD_DIM)
        q_h = (q_lin[:, sl] * cos + q_rot[:, sl] * sin) * q_scale
        q_mla_ref[0, h] = q_h.astype(BF16)
        k_mla_ref[0, h] = (k_nope[:, sl] + k_rope).astype(BF16)
    v_mla_t = v_mla.T
    pad_row = lax.broadcasted_iota(jnp.int32, (MLA_VT_ROWS - MLA_V_DIM, SEQ_BLOCK), 0)
    ones_then_zeros = jnp.where(pad_row == 0, 1.0, 0.0).astype(BF16)
    for h in range(MLA_HEADS):
        for t in range(PROJ_BLOCKS):
            vt_mla_ref[0, h, t, :MLA_V_DIM, :] = v_mla_t[h * MLA_V_DIM:(h + 1) * MLA_V_DIM,
                                                         t * SEQ_BLOCK:(t + 1) * SEQ_BLOCK].astype(BF16)
            vt_mla_ref[0, h, t, MLA_V_DIM:, :] = ones_then_zeros

    sb_scale = LOG2_E / math.sqrt(SB_HEAD_DIM)
    q_sb = sb[:, :SB_WIDTH] * sb_scale
    k_sb = sb[:, SB_WIDTH:2 * SB_WIDTH]
    v_sb_t = sb[:, 2 * SB_WIDTH:].T
    for h in range(SB_HEADS):
        sl = slice(h * SB_HEAD_DIM, (h + 1) * SB_HEAD_DIM)
        q_sb_ref[0, h] = q_sb[:, sl].astype(BF16)
        k_sb_ref[0, h] = k_sb[:, sl].astype(BF16)
        for t in range(PROJ_BLOCKS):
            vt_sb_ref[0, h, t] = v_sb_t[sl, t * SEQ_BLOCK:(t + 1) * SEQ_BLOCK].astype(BF16)


def _block_iotas():
    key_idx = lax.broadcasted_iota(jnp.int32, (SEQ_BLOCK, SEQ_BLOCK), 0)
    qry_idx = lax.broadcasted_iota(jnp.int32, (SEQ_BLOCK, SEQ_BLOCK), 1)
    return key_idx, qry_idx


def _attn_kernel(q_ref, k_ref, vt_ref, q_sb_ref, k_sb_ref, vt_sb_ref, tri_ref, o_ref, o_sb_ref,
                 z_scr, w_scr, *, n_super):
    blk, n, k_per_step = SEQ_BLOCK, N_CHAINS, MLA_K_PER_STEP
    key_idx, qry_idx = _block_iotas()
    causal = key_idx <= qry_idx
    unit = jnp.ones((1, blk), F32)
    sb_stages, sb_tail = _sb_super_block(q_sb_ref, k_sb_ref, vt_sb_ref, tri_ref, o_sb_ref)

    def rows(ref, j):
        return ref[0, 0, pl.ds(pl.multiple_of(j * blk, blk), blk), :]

    def softmax_blocks(zs, ms, diag_chain):
        new_ms, ws, alphas = [], [], []
        for i, (s, m) in enumerate(zip(zs, ms)):
            if i == diag_chain:
                s = jnp.where(causal, s, -jnp.inf)
            s_max = jnp.max(s, axis=0, keepdims=True)
            if m is None:
                m_new, alpha = s_max, None
            else:
                m_new = jnp.maximum(m, s_max)
                alpha = jnp.exp2(m - m_new)
            new_ms.append(m_new)
            ws.append(jnp.exp2(s - m_new).astype(BF16))
            alphas.append(alpha)
        return new_ms, ws, alphas

    def rescale_add(acc, pv, alpha):
        return pv if acc is None else alpha * acc + pv

    def super_body(sb, carry):
        base = sb * n

        def scores(j, chains):
            k = rows(k_ref, j)
            return [_dot_nt(k, rows(q_ref, base + a)) for a in chains]

        ms, accs = [None] * n, [None] * n
        side, sb_box = sb_stages(base)
        assert len(side) == n + 1
        zs_of = {n - 1: scores(base + n - 1, [n - 1])}
        side.pop(0)()
        for kb in reversed(range(n - 1)):
            zs_of[kb] = scores(base + kb, range(kb, n))
        for a, z in enumerate(scores(jnp.maximum(base - 1, 0), range(n))):
            z_scr[0, a] = z
        pending = None
        for kb in reversed(range(n)):
            chains = list(range(kb, n))
            zs = zs_of[kb]
            if pending is not None:
                vt = vt_ref[0, 0, base + kb + 1]
                for a, w, alpha in zip(*pending):
                    accs[a] = rescale_add(accs[a], _dot(vt, w), alpha)
            new_ms, ws, alphas = softmax_blocks(zs, [ms[a] for a in chains], 0)
            for a, m in zip(chains, new_ms):
                ms[a] = m
            side.pop(0)()
            if kb > 0:
                pending = (chains, ws, alphas)
            else:
                for a in range(n):
                    w_scr[a] = ws[a]
                accs[0] = jnp.zeros((vt_ref.shape[3], blk), F32)
                alphas[0] = unit

        def k_body(it, state):
            ms, accs, alphas = (list(t) for t in state)
            j = base - 1 - k_per_step * it
            for u in range(k_per_step):
                slot, j_cur, j_next = u % 2, j - u, jnp.maximum(j - u - 1, 0)
                k_next, vt_prev = rows(k_ref, j_next), vt_ref[0, 0, j_cur + 1]
                for a in range(n):
                    z_scr[1 - slot, a] = _dot_nt(k_next, rows(q_ref, base + a))
                    accs[a] = rescale_add(accs[a], _dot(vt_prev, w_scr[a]), alphas[a])
                ms, ws, alphas = softmax_blocks([z_scr[slot, a] for a in range(n)], ms, None)
                for a in range(n):
                    w_scr[a] = ws[a]
            return tuple(ms), tuple(accs), tuple(alphas)

        ms, accs, alphas = lax.fori_loop(
            0, sb * (n // k_per_step), k_body, (tuple(ms), tuple(accs), tuple(alphas)))
        vt = vt_ref[0, 0, 0]
        accs = [rescale_add(acc, _dot(vt, w_scr[a]), alpha)
                for a, (acc, alpha) in enumerate(zip(accs, alphas))]
        for a in range(n):
            o_ref[0, 0, base + a] = accs[a][:MLA_V_DIM] / accs[a][MLA_V_DIM:MLA_V_DIM + 1]
        sb_tail(base, sb_box)
        return carry

    lax.fori_loop(0, n_super, super_body, 0)


def _softplus2(z):
    return jnp.maximum(z, 0.0) + jnp.log2(1.0 + jnp.exp2(-jnp.abs(z)))


def _sb_super_block(q_ref, k_ref, vt_ref, tri_ref, o_ref):
    blk, n = SEQ_BLOCK, N_CHAINS
    key_idx, qry_idx = _block_iotas()
    strict = key_idx < qry_idx

    def rows(ref, j):
        return ref[0, 0, pl.ds(pl.multiple_of(j * blk, blk), blk), :]

    def logits(base, specs):
        return [_dot_nt(rows(k_ref, j), rows(q_ref, base + a)) for a, j, _ in specs]

    def suffix_sums(zs, specs):
        log_betas, sps = [], []
        for z, (_, _, diag) in zip(zs, specs):
            sp = _softplus2(z)
            log_betas.append(z - sp)
            if diag:
                sp = jnp.where(strict, sp, 0.0)
            sps.append(sp.astype(BF16))
        return log_betas, sps, [_dot(tri_ref[...], sp) for sp in sps]

    def products(log_betas, sps, sufs, specs):
        ws = []
        for log_beta, suf, (_, _, diag) in zip(log_betas, sufs, specs):
            w = jnp.exp2(log_beta + suf)
            if diag:
                w = jnp.where(strict, w, 0.0)
            ws.append(w.astype(BF16))
        return [(suf[0:1, :] - sp[0:1, :].astype(F32), _dot(vt_ref[0, 0, j], w))
                for w, sp, suf, (_, j, _) in zip(ws, sps, sufs, specs)]

    def tiles(base, groups):
        zs = [logits(base, g) for g in groups]
        out, prev = [], None
        for g, z in zip(groups, zs):
            cur = suffix_sums(z, g) + (g,)
            if prev is not None:
                out += products(*prev)
            prev = cur
        return out + products(*prev)

    def band(base, d):
        return [(a, jnp.maximum(base + a - d, 0), False) for a in range(n)]

    def factor(base, a, d, c):
        return jnp.where(base + a - d >= 0, jnp.exp2(c), 0.0)

    def live(cs):
        return (jnp.max(functools.reduce(jnp.maximum, cs)) > -SB_DEAD_BITS).astype(jnp.int32)

    def stages(base):
        groups = [[(a, base + a, True) for a in range(n)], band(base, 1)]
        box = {}

        def s0():
            box["z"] = [logits(base, g) for g in groups]

        def s1():
            box["c0"] = suffix_sums(box["z"][0], groups[0])

        def s2():
            box["c1"] = suffix_sums(box["z"][1], groups[1])
            box["r0"] = products(*box["c0"], groups[0])

        def s3():
            box["r1"] = products(*box["c1"], groups[1])

        def s4():
            cs = []
            for a in range(n):
                (sum0, pv0), (sum1, pv1) = box["r0"][a], box["r1"][a]
                o_ref[0, 0, base + a] = pv0 + pv1 * factor(base, a, 1, sum0)
                cs.append(sum0 + sum1)
            box["carry"] = cs

        return [s0, s1, s2, s3, s4], box

    def tail(base, box):
        def w_body(state):
            d, _, cs = state
            res = tiles(base, [band(base, d)])
            for a, (c, (_, pv)) in enumerate(zip(cs, res)):
                o_ref[0, 0, base + a] = o_ref[0, 0, base + a] + pv * factor(base, a, d, c)
            cs = tuple(c + block_sum for c, (block_sum, _) in zip(cs, res))
            return d + 1, live(cs), cs

        cs = box["carry"]
        lax.while_loop(lambda state: jnp.logical_and(state[0] < base + n, state[1] > 0), w_body,
                       (jnp.int32(2), live(cs), tuple(cs)))

    return stages, tail


def _out_kernel(x_ref, o_mla_ref, o_sb_ref, g_mla_ref, g_sb_ref, w_o_mla_ref, w_o_sb_ref,
                g_ffn_ref, w_gate_ref, w_up_ref, w_down_ref, g_final_ref, out_ref, *, d_ff):
    blk = SEQ_BLOCK

    def group(o_ref, g_ref, w_ref):
        parts = []
        for t in range(OUT_BLOCKS):
            o = o_ref[0, :, t].reshape(-1, blk)
            y = (o * _rms_scale(o, 0) * g_ref[...]).astype(BF16)
            parts.append(_dot_tn(y, w_ref[...]))
        return jnp.concatenate(parts, axis=0)

    h = x_ref[0] + group(o_mla_ref, g_mla_ref, w_o_mla_ref) + group(o_sb_ref, g_sb_ref, w_o_sb_ref)
    f = (h * _rms_scale(h, -1) * g_ffn_ref[...]).astype(BF16)
    ffn = jnp.zeros_like(h)
    for c in range(0, d_ff, FF_CHUNK):
        gate = _dot(f, w_gate_ref[:, c:c + FF_CHUNK])
        up = _dot(f, w_up_ref[:, c:c + FF_CHUNK])
        act = (gate * jax.nn.sigmoid(gate) * up).astype(BF16)
        ffn = ffn + _dot(act, w_down_ref[c:c + FF_CHUNK, :])
    h = h + ffn
    out_ref[0] = h * _rms_scale(h, -1) * g_final_ref[...]


def _rotate_half_cols(w):
    half = w.shape[-1] // 2
    return jnp.concatenate([-w[..., half:], w[..., :half]], axis=-1)


def _head_slots(nope, rope):
    ref = nope if nope is not None else rope
    r, h = ref.shape[0], ref.shape[1]
    nope = jnp.zeros((r, h, MLA_NOPE_DIM), ref.dtype) if nope is None else nope
    rope = jnp.zeros((r, h, MLA_ROPE_DIM), ref.dtype) if rope is None else rope
    pad = jnp.zeros((r, h, MLA_PAD_DIM - MLA_QK_DIM), ref.dtype)
    return jnp.concatenate([nope, rope, pad], axis=-1).reshape(r, h * MLA_PAD_DIM)


def _const_spec(shape):
    return pl.BlockSpec(shape, lambda *_: (0,) * len(shape))


def kernel(x, positions, norm_mix, w_in, q_latent_norm, w_uq, kv_latent_norm, w_ukv,
           out_norm_mla, out_norm_sb, w_o, norm_ffn, w_gate, w_up, w_down, norm_final):
    b, s, d = x.shape
    depth = w_in.shape[0]
    d_ff = w_gate.shape[-1]
    blk = SEQ_BLOCK
    nb = s // blk
    assert s % (blk * N_CHAINS) == 0 and nb % PROJ_BLOCKS == 0 and nb % OUT_BLOCKS == 0
    assert N_CHAINS % MLA_K_PER_STEP == 0 and MLA_K_PER_STEP % 2 == 0 and d_ff % FF_CHUNK == 0

    inv_freq = ROPE_THETA ** (-jnp.arange(0, MLA_ROPE_DIM, 2, dtype=F32) / MLA_ROPE_DIM)
    half = MLA_ROPE_DIM // 2
    lane = jnp.arange(MLA_PAD_DIM)[None, :]
    on_rope = (lane >= MLA_NOPE_DIM) & (lane < MLA_QK_DIM)
    spread = (on_rope & ((lane - MLA_NOPE_DIM) % half == jnp.arange(half)[:, None])).astype(BF16)
    spread = jnp.tile(spread, (3, 1))
    off_rope = (~on_rope).astype(F32)
    tri = jnp.where(jnp.arange(blk)[None, :] > jnp.arange(blk)[:, None], -1.0, 0.0).astype(BF16)
    pos = positions.reshape(b, 1, s)

    params = pltpu.CompilerParams(
        dimension_semantics=("arbitrary", "arbitrary"), vmem_limit_bytes=VMEM_LIMIT_BYTES)

    h = x
    for l in range(depth):
        o0 = Q_LORA_RANK
        o1 = o0 + KV_LORA_RANK
        o2 = o1 + MLA_ROPE_DIM
        w_in_t = jnp.swapaxes(w_in[l], 0, 1)
        w_cq, w_ckv, w_kr, w_sb = w_in_t[:o0], w_in_t[o0:o1], w_in_t[o1:o2], w_in_t[o2:]
        kr_slot = lambda w: jnp.pad(w, ((MLA_NOPE_DIM, MLA_PAD_DIM - MLA_QK_DIM), (0, 0)))
        w_kr_rot = jnp.concatenate([-w_kr[MLA_ROPE_DIM // 2:], w_kr[:MLA_ROPE_DIM // 2]], axis=0)
        w_lat = jnp.concatenate([w_cq, w_ckv, kr_slot(w_kr), kr_slot(w_kr_rot)],
                                axis=0).astype(BF16)
        uq = w_uq[l].reshape(Q_LORA_RANK, MLA_HEADS, MLA_QK_DIM)
        uq_nope, uq_rope = uq[..., :MLA_NOPE_DIM], uq[..., MLA_NOPE_DIM:]
        w_uq_lin = _head_slots(uq_nope, uq_rope).astype(BF16)
        w_uq_rot = _head_slots(None, _rotate_half_cols(uq_rope)).astype(BF16)
        ukv = w_ukv[l].reshape(KV_LORA_RANK, MLA_HEADS, MLA_NOPE_DIM + MLA_V_DIM)
        w_uk = _head_slots(ukv[..., :MLA_NOPE_DIM], None).astype(BF16)
        w_uv = ukv[..., MLA_NOPE_DIM:].reshape(KV_LORA_RANK, MLA_WIDTH).astype(BF16)

        head_major = lambda width: jax.ShapeDtypeStruct((b, MLA_HEADS, s, width), BF16)
        tile = PROJ_BLOCKS * blk
        head_spec = lambda width: pl.BlockSpec((1, MLA_HEADS, tile, width), lambda bi, ti: (bi, 0, ti, 0))
        vt_shape = lambda rows: jax.ShapeDtypeStruct((b, MLA_HEADS, nb, rows, blk), BF16)
        vt_spec = lambda rows: pl.BlockSpec((1, MLA_HEADS, PROJ_BLOCKS, rows, blk),
                                            lambda bi, ti: (bi, 0, ti, 0, 0))
        proj_steps = nb // PROJ_BLOCKS
        n_slabs = b * proj_steps
        assert d % (n_slabs * BF16_SUBLANE_TILE) == 0 and d_ff % (n_slabs * BF16_SUBLANE_TILE) == 0
        slab_in = lambda rows, cols: pl.BlockSpec(
            (1, rows // n_slabs, cols), lambda bi, ti: (l, bi * proj_steps + ti, 0))
        slab_out = lambda rows, cols: pl.BlockSpec(
            (rows // n_slabs, cols), lambda bi, ti: (bi * proj_steps + ti, 0))
        (q_mla, k_mla, vt_mla, q_sb, k_sb, vt_sb,
         w_gate_bf, w_up_bf, w_down_bf) = pl.pallas_call(
            _proj_kernel,
            grid=(b, proj_steps),
            in_specs=[
                pl.BlockSpec((1, tile, d), lambda bi, ti: (bi, ti, 0)),
                pl.BlockSpec((1, 1, tile), lambda bi, ti: (bi, 0, ti)),
                _const_spec((half, 1)),
                _const_spec((3 * half, MLA_PAD_DIM)),
                _const_spec((1, MLA_PAD_DIM)),
                _const_spec((1, d)),
                _const_spec((1, Q_LORA_RANK)),
                _const_spec((1, KV_LORA_RANK)),
                _const_spec(w_lat.shape),
                _const_spec((3 * SB_WIDTH, d)),
                _const_spec(w_uq_lin.shape),
                _const_spec(w_uq_rot.shape),
                _const_spec(w_uk.shape),
                _const_spec(w_uv.shape),
                slab_in(d, d_ff), slab_in(d, d_ff), slab_in(d_ff, d),
            ],
            out_specs=[head_spec(MLA_PAD_DIM), head_spec(MLA_PAD_DIM), vt_spec(MLA_VT_ROWS),
                       head_spec(SB_HEAD_DIM), head_spec(SB_HEAD_DIM), vt_spec(SB_HEAD_DIM),
                       slab_out(d, d_ff), slab_out(d, d_ff), slab_out(d_ff, d)],
            out_shape=[head_major(MLA_PAD_DIM), head_major(MLA_PAD_DIM), vt_shape(MLA_VT_ROWS),
                       head_major(SB_HEAD_DIM), head_major(SB_HEAD_DIM), vt_shape(SB_HEAD_DIM),
                       jax.ShapeDtypeStruct((d, d_ff), BF16), jax.ShapeDtypeStruct((d, d_ff), BF16),
                       jax.ShapeDtypeStruct((d_ff, d), BF16)],
            compiler_params=params,
            name="proj",
        )(h, pos, inv_freq[:, None], spread, off_rope,
          norm_mix[l][None, :], q_latent_norm[l][None, :], kv_latent_norm[l][None, :],
          w_lat, w_sb.astype(BF16), w_uq_lin, w_uq_rot, w_uk, w_uv, w_gate, w_up, w_down)

        seq_spec = lambda width: pl.BlockSpec((1, 1, s, width), lambda bi, hi: (bi, hi, 0, 0))
        blocked = lambda rows: pl.BlockSpec((1, 1, nb, rows, blk), lambda bi, hi: (bi, hi, 0, 0, 0))
        o_shape = jax.ShapeDtypeStruct((b, MLA_HEADS, nb, MLA_V_DIM, blk), F32)
        score_scratch = [pltpu.VMEM((2, N_CHAINS, blk, blk), F32),
                         pltpu.VMEM((N_CHAINS, blk, blk), BF16)]
        assert MLA_HEADS == SB_HEADS
        o_mla, o_sb = pl.pallas_call(
            functools.partial(_attn_kernel, n_super=nb // N_CHAINS),
            grid=(b, MLA_HEADS),
            in_specs=[seq_spec(MLA_PAD_DIM), seq_spec(MLA_PAD_DIM), blocked(MLA_VT_ROWS),
                      seq_spec(SB_HEAD_DIM), seq_spec(SB_HEAD_DIM), blocked(SB_HEAD_DIM),
                      _const_spec((blk, blk))],
            out_specs=[blocked(MLA_V_DIM), blocked(SB_HEAD_DIM)],
            out_shape=[o_shape, o_shape],
            scratch_shapes=score_scratch,
            compiler_params=params,
            name="attn",
        )(q_mla, k_mla, vt_mla, q_sb, k_sb, vt_sb, tri)

        assert depth == 1
        o_spec = pl.BlockSpec((1, MLA_HEADS, OUT_BLOCKS, MLA_V_DIM, blk), lambda bi, ti: (bi, 0, ti, 0, 0))
        tok_spec = pl.BlockSpec((1, OUT_BLOCKS * blk, d), lambda bi, ti: (bi, ti, 0))
        resident = lambda shape: pl.BlockSpec(shape, lambda *_: (0,) * len(shape),
                                              pipeline_mode=pl.Buffered(1))
        h = pl.pallas_call(
            functools.partial(_out_kernel, d_ff=d_ff),
            grid=(b, nb // OUT_BLOCKS),
            in_specs=[
                tok_spec, o_spec, o_spec,
                _const_spec((MLA_WIDTH, 1)), _const_spec((SB_WIDTH, 1)),
                resident((MLA_WIDTH, d)), resident((SB_WIDTH, d)),
                _const_spec((1, d)),
                resident((d, d_ff)), resident((d, d_ff)), resident((d_ff, d)),
                _const_spec((1, d)),
            ],
            out_specs=tok_spec,
            out_shape=jax.ShapeDtypeStruct((b, s, d), F32),
            compiler_params=params,
            name="out_ffn",
        )(h, o_mla, o_sb, out_norm_mla[l][:, None], out_norm_sb[l][:, None],
          w_o[l][:MLA_WIDTH].astype(BF16), w_o[l][MLA_WIDTH:].astype(BF16),
          norm_ffn[l][None, :], w_gate_bf, w_up_bf, w_down_bf, norm_final[None, :])
    return h
```

```python
import functools
import math

import jax
import jax.numpy as jnp
from jax import lax
from jax.experimental import pallas as pl
from jax.experimental.pallas import tpu as pltpu

EPS = 1e-6
ROPE_THETA = 10000.0
LOG2_E = 1.4426950408889634

LANES = 128
BF16_SUBLANE_TILE = 16
MXU_DIM_V7X = 256
VMEM_BYTES_V7X = 64 * 1024 * 1024

MLA_HEADS = 8
MLA_NOPE_DIM = 64
MLA_ROPE_DIM = 32
MLA_V_DIM = 64
MLA_QK_DIM = MLA_NOPE_DIM + MLA_ROPE_DIM
MLA_PAD_DIM = LANES
MLA_VT_ROWS = MLA_V_DIM + BF16_SUBLANE_TILE
Q_LORA_RANK = 256
KV_LORA_RANK = 128
SB_HEADS = 8
SB_HEAD_DIM = 64
MLA_WIDTH = MLA_HEADS * MLA_V_DIM
SB_WIDTH = SB_HEADS * SB_HEAD_DIM

SEQ_BLOCK = MXU_DIM_V7X
N_CHAINS = 4
MLA_K_PER_STEP = 4
SB_DEAD_BITS = 160.0
FF_CHUNK = MXU_DIM_V7X
OUT_BLOCKS = 2
PROJ_BLOCKS = 4
VMEM_LIMIT_BYTES = VMEM_BYTES_V7X * 7 // 8

F32 = jnp.float32
BF16 = jnp.bfloat16


def _rms_scale(v, axis):
    return lax.rsqrt(jnp.mean(v * v, axis=axis, keepdims=True) + EPS)


def _dot(a, b):
    return jnp.dot(a, b, preferred_element_type=F32)


def _dot_nt(a, b):
    return lax.dot_general(a, b, (((1,), (1,)), ((), ())), preferred_element_type=F32)


def _dot_tn(a, b):
    return lax.dot_general(a, b, (((0,), (0,)), ((), ())), preferred_element_type=F32)


def _proj_kernel(x_ref, pos_ref, freq_ref, spread_ref, off_rope_ref, g_mix_ref, g_q_ref, g_kv_ref,
                 w_lat_ref, w_sb_ref, w_uq_ref, w_uq_rot_ref, w_uk_ref, w_uv_ref,
                 w_gate_ref, w_up_ref, w_down_ref,
                 q_mla_ref, k_mla_ref, vt_mla_ref, q_sb_ref, k_sb_ref, vt_sb_ref,
                 w_gate_bf_ref, w_up_bf_ref, w_down_bf_ref):
    w_gate_bf_ref[...] = w_gate_ref[0].astype(BF16)
    w_up_bf_ref[...] = w_up_ref[0].astype(BF16)
    w_down_bf_ref[...] = w_down_ref[0].astype(BF16)

    x = x_ref[0]
    u = (x * _rms_scale(x, -1) * g_mix_ref[...]).astype(BF16)

    ang_t = freq_ref[...] * pos_ref[0].astype(F32)
    cos_t = jnp.cos(ang_t)
    sin_t = jnp.sin(ang_t)

    lat = _dot_nt(u, w_lat_ref[...])
    sb = _dot_nt(u, w_sb_ref[...])
    def spread(t):
        t1 = t.astype(BF16)
        r1 = t - t1.astype(F32)
        t2 = r1.astype(BF16)
        t3 = (r1 - t2.astype(F32)).astype(BF16)
        return _dot_tn(jnp.concatenate([t1, t2, t3], axis=0), spread_ref[...])

    cos = spread(cos_t) + off_rope_ref[...]
    sin = spread(sin_t)
    c_q = lat[:, :Q_LORA_RANK]
    c_kv = lat[:, Q_LORA_RANK:Q_LORA_RANK + KV_LORA_RANK]
    k_r = lat[:, Q_LORA_RANK + KV_LORA_RANK:Q_LORA_RANK + KV_LORA_RANK + MLA_PAD_DIM]
    k_r_rot = lat[:, Q_LORA_RANK + KV_LORA_RANK + MLA_PAD_DIM:]
    k_rope = k_r * cos + k_r_rot * sin

    ql = (c_q * _rms_scale(c_q, -1) * g_q_ref[...]).astype(BF16)
    kvl = (c_kv * _rms_scale(c_kv, -1) * g_kv_ref[...]).astype(BF16)

    q_lin = _dot(ql, w_uq_ref[...])
    q_rot = _dot(ql, w_uq_rot_ref[...])
    k_nope = _dot(kvl, w_uk_ref[...])
    v_mla = _dot(kvl, w_uv_ref[...])
    q_scale = LOG2_E / math.sqrt(MLA_QK_DIM)
    for h in range(MLA_HEADS):
        sl = slice(h * MLA_PAD_DIM, (h + 1) * MLA_PAD_DIM)
        q_h = (q_lin[:, sl] * cos + q_rot[:, sl] * sin) * q_scale
        q_mla_ref[0, h] = q_h.astype(BF16)
        k_mla_ref[0, h] = (k_nope[:, sl] + k_rope).astype(BF16)
    v_mla_t = v_mla.T
    pad_row = lax.broadcasted_iota(jnp.int32, (MLA_VT_ROWS - MLA_V_DIM, SEQ_BLOCK), 0)
    ones_then_zeros = jnp.where(pad_row == 0, 1.0, 0.0).astype(BF16)
    for h in range(MLA_HEADS):
        for t in range(PROJ_BLOCKS):
            vt_mla_ref[0, h, t, :MLA_V_DIM, :] = v_mla_t[h * MLA_V_DIM:(h + 1) * MLA_V_DIM,
                                                         t * SEQ_BLOCK:(t + 1) * SEQ_BLOCK].astype(BF16)
            vt_mla_ref[0, h, t, MLA_V_DIM:, :] = ones_then_zeros

    sb_scale = LOG2_E / math.sqrt(SB_HEAD_DIM)
    q_sb = sb[:, :SB_WIDTH] * sb_scale
    k_sb = sb[:, SB_WIDTH:2 * SB_WIDTH]
    v_sb_t = sb[:, 2 * SB_WIDTH:].T
    for h in range(SB_HEADS):
        sl = slice(h * SB_HEAD_DIM, (h + 1) * SB_HEAD_DIM)
        q_sb_ref[0, h] = q_sb[:, sl].astype(BF16)
        k_sb_ref[0, h] = k_sb[:, sl].astype(BF16)
        for t in range(PROJ_BLOCKS):
            vt_sb_ref[0, h, t] = v_sb_t[sl, t * SEQ_BLOCK:(t + 1) * SEQ_BLOCK].astype(BF16)


def _block_iotas():
    key_idx = lax.broadcasted_iota(jnp.int32, (SEQ_BLOCK, SEQ_BLOCK), 0)
    qry_idx = lax.broadcasted_iota(jnp.int32, (SEQ_BLOCK, SEQ_BLOCK), 1)
    return key_idx, qry_idx


def _attn_kernel(q_ref, k_ref, vt_ref, q_sb_ref, k_sb_ref, vt_sb_ref, tri_ref, o_ref, o_sb_ref,
                 z_scr, w_scr, acc_scr, *, n_super):
    blk, n, k_per_step = SEQ_BLOCK, N_CHAINS, MLA_K_PER_STEP
    key_idx, qry_idx = _block_iotas()
    causal = key_idx <= qry_idx
    unit = jnp.ones((1, blk), F32)
    sb_stages, sb_tail = _sb_super_block(q_sb_ref, k_sb_ref, vt_sb_ref, tri_ref, o_sb_ref)

    def rows(ref, j):
        return ref[0, 0, pl.ds(pl.multiple_of(j * blk, blk), blk), :]

    def softmax_blocks(zs, ms, diag_chain):
        new_ms, ws, alphas = [], [], []
        for i, (s, m) in enumerate(zip(zs, ms)):
            if i == diag_chain:
                s = jnp.where(causal, s, -jnp.inf)
            s_max = jnp.max(s, axis=0, keepdims=True)
            if m is None:
                m_new, alpha = s_max, None
            else:
                m_new = jnp.maximum(m, s_max)
                alpha = jnp.exp2(m - m_new)
            new_ms.append(m_new)
            ws.append(jnp.exp2(s - m_new).astype(BF16))
            alphas.append(alpha)
        return new_ms, ws, alphas

    def rescale_add(acc, pv, alpha):
        return pv if acc is None else alpha * acc + pv

    def super_body(sb, carry):
        base = sb * n

        def scores(j, chains):
            k = rows(k_ref, j)
            return [_dot_nt(k, rows(q_ref, base + a)) for a in chains]

        ms, accs = [None] * n, [None] * n
        side, sb_box = sb_stages(base)
        assert len(side) == n + 1
        zs_of = {n - 1: scores(base + n - 1, [n - 1])}
        side.pop(0)()
        for kb in reversed(range(n - 1)):
            zs_of[kb] = scores(base + kb, range(kb, n))
        for a, z in enumerate(scores(jnp.maximum(base - 1, 0), range(n))):
            z_scr[0, a] = z
        pending = None
        for kb in reversed(range(n)):
            chains = list(range(kb, n))
            zs = zs_of[kb]
            if pending is not None:
                vt = vt_ref[0, 0, base + kb + 1]
                for a, w, alpha in zip(*pending):
                    accs[a] = rescale_add(accs[a], _dot(vt, w), alpha)
            new_ms, ws, alphas = softmax_blocks(zs, [ms[a] for a in chains], 0)
            for a, m in zip(chains, new_ms):
                ms[a] = m
            side.pop(0)()
            if kb > 0:
                pending = (chains, ws, alphas)
            else:
                for a in range(n):
                    w_scr[a] = ws[a]
                accs[0] = jnp.zeros((vt_ref.shape[3], blk), F32)
                alphas[0] = unit

        for a in range(n):
            acc_scr[a] = accs[a]

        def k_body(it, state):
            ms, alphas = (list(t) for t in state)
            j = base - 1 - k_per_step * it
            for u in range(k_per_step):
                slot, j_cur, j_next = u % 2, j - u, jnp.maximum(j - u - 1, 0)
                k_next, vt_prev = rows(k_ref, j_next), vt_ref[0, 0, j_cur + 1]
                for a in range(n):
                    z_scr[1 - slot, a] = _dot_nt(k_next, rows(q_ref, base + a))
                    acc_scr[a] = rescale_add(acc_scr[a], _dot(vt_prev, w_scr[a]), alphas[a])
                ms, ws, alphas = softmax_blocks([z_scr[slot, a] for a in range(n)], ms, None)
                for a in range(n):
                    w_scr[a] = ws[a]
            return tuple(ms), tuple(alphas)

        ms, alphas = lax.fori_loop(
            0, sb * (n // k_per_step), k_body, (tuple(ms), tuple(alphas)))
        accs = [acc_scr[a] for a in range(n)]
        vt = vt_ref[0, 0, 0]
        accs = [rescale_add(acc, _dot(vt, w_scr[a]), alpha)
                for a, (acc, alpha) in enumerate(zip(accs, alphas))]
        for a in range(n):
            o_ref[0, 0, base + a] = accs[a][:MLA_V_DIM] / accs[a][MLA_V_DIM:MLA_V_DIM + 1]
        sb_tail(base, sb_box)
        return carry

    lax.fori_loop(0, n_super, super_body, 0)


def _softplus2(z):
    return jnp.maximum(z, 0.0) + jnp.log2(1.0 + jnp.exp2(-jnp.abs(z)))


def _sb_super_block(q_ref, k_ref, vt_ref, tri_ref, o_ref):
    blk, n = SEQ_BLOCK, N_CHAINS
    key_idx, qry_idx = _block_iotas()
    strict = key_idx < qry_idx

    def rows(ref, j):
        return ref[0, 0, pl.ds(pl.multiple_of(j * blk, blk), blk), :]

    def logits(base, specs):
        return [_dot_nt(rows(k_ref, j), rows(q_ref, base + a)) for a, j, _ in specs]

    def suffix_sums(zs, specs):
        log_betas, sps = [], []
        for z, (_, _, diag) in zip(zs, specs):
            sp = _softplus2(z)
            log_betas.append(z - sp)
            if diag:
                sp = jnp.where(strict, sp, 0.0)
            sps.append(sp.astype(BF16))
        return log_betas, sps, [_dot(tri_ref[...], sp) for sp in sps]

    def products(log_betas, sps, sufs, specs):
        ws = []
        for log_beta, suf, (_, _, diag) in zip(log_betas, sufs, specs):
            w = jnp.exp2(log_beta + suf)
            if diag:
                w = jnp.where(strict, w, 0.0)
            ws.append(w.astype(BF16))
        return [(suf[0:1, :] - sp[0:1, :].astype(F32), _dot(vt_ref[0, 0, j], w))
                for w, sp, suf, (_, j, _) in zip(ws, sps, sufs, specs)]

    def tiles(base, groups):
        zs = [logits(base, g) for g in groups]
        out, prev = [], None
        for g, z in zip(groups, zs):
            cur = suffix_sums(z, g) + (g,)
            if prev is not None:
                out += products(*prev)
            prev = cur
        return out + products(*prev)

    def band(base, d):
        return [(a, jnp.maximum(base + a - d, 0), False) for a in range(n)]

    def factor(base, a, d, c):
        return jnp.where(base + a - d >= 0, jnp.exp2(c), 0.0)

    def live(cs):
        return (jnp.max(functools.reduce(jnp.maximum, cs)) > -SB_DEAD_BITS).astype(jnp.int32)

    def stages(base):
        groups = [[(a, base + a, True) for a in range(n)], band(base, 1)]
        box = {}

        def s0():
            box["z"] = [logits(base, g) for g in groups]

        def s1():
            box["c0"] = suffix_sums(box["z"][0], groups[0])

        def s2():
            box["c1"] = suffix_sums(box["z"][1], groups[1])
            box["r0"] = products(*box["c0"], groups[0])

        def s3():
            box["r1"] = products(*box["c1"], groups[1])

        def s4():
            cs = []
            for a in range(n):
                (sum0, pv0), (sum1, pv1) = box["r0"][a], box["r1"][a]
                o_ref[0, 0, base + a] = pv0 + pv1 * factor(base, a, 1, sum0)
                cs.append(sum0 + sum1)
            box["carry"] = cs

        return [s0, s1, s2, s3, s4], box

    def tail(base, box):
        def w_body(state):
            d, _, cs = state
            res = tiles(base, [band(base, d)])
            for a, (c, (_, pv)) in enumerate(zip(cs, res)):
                o_ref[0, 0, base + a] = o_ref[0, 0, base + a] + pv * factor(base, a, d, c)
            cs = tuple(c + block_sum for c, (block_sum, _) in zip(cs, res))
            return d + 1, live(cs), cs

        cs = box["carry"]
        lax.while_loop(lambda state: jnp.logical_and(state[0] < base + n, state[1] > 0), w_body,
                       (jnp.int32(2), live(cs), tuple(cs)))

    return stages, tail


def _out_kernel(x_ref, o_mla_ref, o_sb_ref, g_mla_ref, g_sb_ref, w_o_mla_ref, w_o_sb_ref,
                g_ffn_ref, w_gate_ref, w_up_ref, w_down_ref, g_final_ref, out_ref, *, d_ff):
    blk = SEQ_BLOCK

    def group(o_ref, g_ref, w_ref):
        parts = []
        for t in range(OUT_BLOCKS):
            o = o_ref[0, :, t].reshape(-1, blk)
            y = (o * _rms_scale(o, 0) * g_ref[...]).astype(BF16)
            parts.append(_dot_tn(y, w_ref[...]))
        return jnp.concatenate(parts, axis=0)

    h = x_ref[0] + group(o_mla_ref, g_mla_ref, w_o_mla_ref) + group(o_sb_ref, g_sb_ref, w_o_sb_ref)
    f = (h * _rms_scale(h, -1) * g_ffn_ref[...]).astype(BF16)
    ffn = jnp.zeros_like(h)
    for c in range(0, d_ff, FF_CHUNK):
        gate = _dot(f, w_gate_ref[:, c:c + FF_CHUNK])
        up = _dot(f, w_up_ref[:, c:c + FF_CHUNK])
        act = (gate * jax.nn.sigmoid(gate) * up).astype(BF16)
        ffn = ffn + _dot(act, w_down_ref[c:c + FF_CHUNK, :])
    h = h + ffn
    out_ref[0] = h * _rms_scale(h, -1) * g_final_ref[...]


def _rotate_half_cols(w):
    half = w.shape[-1] // 2
    return jnp.concatenate([-w[..., half:], w[..., :half]], axis=-1)


def _head_slots(nope, rope):
    ref = nope if nope is not None else rope
    r, h = ref.shape[0], ref.shape[1]
    nope = jnp.zeros((r, h, MLA_NOPE_DIM), ref.dtype) if nope is None else nope
    rope = jnp.zeros((r, h, MLA_ROPE_DIM), ref.dtype) if rope is None else rope
    pad = jnp.zeros((r, h, MLA_PAD_DIM - MLA_QK_DIM), ref.dtype)
    return jnp.concatenate([nope, rope, pad], axis=-1).reshape(r, h * MLA_PAD_DIM)


def _const_spec(shape):
    return pl.BlockSpec(shape, lambda *_: (0,) * len(shape))


def kernel(x, positions, norm_mix, w_in, q_latent_norm, w_uq, kv_latent_norm, w_ukv,
           out_norm_mla, out_norm_sb, w_o, norm_ffn, w_gate, w_up, w_down, norm_final):
    b, s, d = x.shape
    depth = w_in.shape[0]
    d_ff = w_gate.shape[-1]
    blk = SEQ_BLOCK
    nb = s // blk
    assert s % (blk * N_CHAINS) == 0 and nb % PROJ_BLOCKS == 0 and nb % OUT_BLOCKS == 0
    assert N_CHAINS % MLA_K_PER_STEP == 0 and MLA_K_PER_STEP % 2 == 0 and d_ff % FF_CHUNK == 0

    inv_freq = ROPE_THETA ** (-jnp.arange(0, MLA_ROPE_DIM, 2, dtype=F32) / MLA_ROPE_DIM)
    half = MLA_ROPE_DIM // 2
    lane = jnp.arange(MLA_PAD_DIM)[None, :]
    on_rope = (lane >= MLA_NOPE_DIM) & (lane < MLA_QK_DIM)
    spread = (on_rope & ((lane - MLA_NOPE_DIM) % half == jnp.arange(half)[:, None])).astype(BF16)
    spread = jnp.tile(spread, (3, 1))
    off_rope = (~on_rope).astype(F32)
    tri = jnp.where(jnp.arange(blk)[None, :] > jnp.arange(blk)[:, None], -1.0, 0.0).astype(BF16)
    pos = positions.reshape(b, 1, s)

    params = pltpu.CompilerParams(
        dimension_semantics=("arbitrary", "arbitrary"), vmem_limit_bytes=VMEM_LIMIT_BYTES)

    h = x
    for l in range(depth):
        o0 = Q_LORA_RANK
        o1 = o0 + KV_LORA_RANK
        o2 = o1 + MLA_ROPE_DIM
        w_in_t = jnp.swapaxes(w_in[l], 0, 1)
        w_cq, w_ckv, w_kr, w_sb = w_in_t[:o0], w_in_t[o0:o1], w_in_t[o1:o2], w_in_t[o2:]
        kr_slot = lambda w: jnp.pad(w, ((MLA_NOPE_DIM, MLA_PAD_DIM - MLA_QK_DIM), (0, 0)))
        w_kr_rot = jnp.concatenate([-w_kr[MLA_ROPE_DIM // 2:], w_kr[:MLA_ROPE_DIM // 2]], axis=0)
        w_lat = jnp.concatenate([w_cq, w_ckv, kr_slot(w_kr), kr_slot(w_kr_rot)],
                                axis=0).astype(BF16)
        uq = w_uq[l].reshape(Q_LORA_RANK, MLA_HEADS, MLA_QK_DIM)
        uq_nope, uq_rope = uq[..., :MLA_NOPE_DIM], uq[..., MLA_NOPE_DIM:]
        w_uq_lin = _head_slots(uq_nope, uq_rope).astype(BF16)
        w_uq_rot = _head_slots(None, _rotate_half_cols(uq_rope)).astype(BF16)
        ukv = w_ukv[l].reshape(KV_LORA_RANK, MLA_HEADS, MLA_NOPE_DIM + MLA_V_DIM)
        w_uk = _head_slots(ukv[..., :MLA_NOPE_DIM], None).astype(BF16)
        w_uv = ukv[..., MLA_NOPE_DIM:].reshape(KV_LORA_RANK, MLA_WIDTH).astype(BF16)

        head_major = lambda width: jax.ShapeDtypeStruct((b, MLA_HEADS, s, width), BF16)
        tile = PROJ_BLOCKS * blk
        head_spec = lambda width: pl.BlockSpec((1, MLA_HEADS, tile, width), lambda bi, ti: (bi, 0, ti, 0))
        vt_shape = lambda rows: jax.ShapeDtypeStruct((b, MLA_HEADS, nb, rows, blk), BF16)
        vt_spec = lambda rows: pl.BlockSpec((1, MLA_HEADS, PROJ_BLOCKS, rows, blk),
                                            lambda bi, ti: (bi, 0, ti, 0, 0))
        proj_steps = nb // PROJ_BLOCKS
        n_slabs = b * proj_steps
        assert d % (n_slabs * BF16_SUBLANE_TILE) == 0 and d_ff % (n_slabs * BF16_SUBLANE_TILE) == 0
        slab_in = lambda rows, cols: pl.BlockSpec(
            (1, rows // n_slabs, cols), lambda bi, ti: (l, bi * proj_steps + ti, 0))
        slab_out = lambda rows, cols: pl.BlockSpec(
            (rows // n_slabs, cols), lambda bi, ti: (bi * proj_steps + ti, 0))
        (q_mla, k_mla, vt_mla, q_sb, k_sb, vt_sb,
         w_gate_bf, w_up_bf, w_down_bf) = pl.pallas_call(
            _proj_kernel,
            grid=(b, proj_steps),
            in_specs=[
                pl.BlockSpec((1, tile, d), lambda bi, ti: (bi, ti, 0)),
                pl.BlockSpec((1, 1, tile), lambda bi, ti: (bi, 0, ti)),
                _const_spec((half, 1)),
                _const_spec((3 * half, MLA_PAD_DIM)),
                _const_spec((1, MLA_PAD_DIM)),
                _const_spec((1, d)),
                _const_spec((1, Q_LORA_RANK)),
                _const_spec((1, KV_LORA_RANK)),
                _const_spec(w_lat.shape),
                _const_spec((3 * SB_WIDTH, d)),
                _const_spec(w_uq_lin.shape),
                _const_spec(w_uq_rot.shape),
                _const_spec(w_uk.shape),
                _const_spec(w_uv.shape),
                slab_in(d, d_ff), slab_in(d, d_ff), slab_in(d_ff, d),
            ],
            out_specs=[head_spec(MLA_PAD_DIM), head_spec(MLA_PAD_DIM), vt_spec(MLA_VT_ROWS),
                       head_spec(SB_HEAD_DIM), head_spec(SB_HEAD_DIM), vt_spec(SB_HEAD_DIM),
                       slab_out(d, d_ff), slab_out(d, d_ff), slab_out(d_ff, d)],
            out_shape=[head_major(MLA_PAD_DIM), head_major(MLA_PAD_DIM), vt_shape(MLA_VT_ROWS),
                       head_major(SB_HEAD_DIM), head_major(SB_HEAD_DIM), vt_shape(SB_HEAD_DIM),
                       jax.ShapeDtypeStruct((d, d_ff), BF16), jax.ShapeDtypeStruct((d, d_ff), BF16),
                       jax.ShapeDtypeStruct((d_ff, d), BF16)],
            compiler_params=params,
            name="proj",
        )(h, pos, inv_freq[:, None], spread, off_rope,
          norm_mix[l][None, :], q_latent_norm[l][None, :], kv_latent_norm[l][None, :],
          w_lat, w_sb.astype(BF16), w_uq_lin, w_uq_rot, w_uk, w_uv, w_gate, w_up, w_down)

        seq_spec = lambda width: pl.BlockSpec((1, 1, s, width), lambda bi, hi: (bi, hi, 0, 0))
        blocked = lambda rows: pl.BlockSpec((1, 1, nb, rows, blk), lambda bi, hi: (bi, hi, 0, 0, 0))
        o_shape = jax.ShapeDtypeStruct((b, MLA_HEADS, nb, MLA_V_DIM, blk), F32)
        score_scratch = [pltpu.VMEM((2, N_CHAINS, blk, blk), F32),
                         pltpu.VMEM((N_CHAINS, blk, blk), BF16),
                         pltpu.VMEM((N_CHAINS, MLA_VT_ROWS, blk), F32)]
        assert MLA_HEADS == SB_HEADS
        o_mla, o_sb = pl.pallas_call(
            functools.partial(_attn_kernel, n_super=nb // N_CHAINS),
            grid=(b, MLA_HEADS),
            in_specs=[seq_spec(MLA_PAD_DIM), seq_spec(MLA_PAD_DIM), blocked(MLA_VT_ROWS),
                      seq_spec(SB_HEAD_DIM), seq_spec(SB_HEAD_DIM), blocked(SB_HEAD_DIM),
                      _const_spec((blk, blk))],
            out_specs=[blocked(MLA_V_DIM), blocked(SB_HEAD_DIM)],
            out_shape=[o_shape, o_shape],
            scratch_shapes=score_scratch,
            compiler_params=params,
            name="attn",
        )(q_mla, k_mla, vt_mla, q_sb, k_sb, vt_sb, tri)

        assert depth == 1
        o_spec = pl.BlockSpec((1, MLA_HEADS, OUT_BLOCKS, MLA_V_DIM, blk), lambda bi, ti: (bi, 0, ti, 0, 0))
        tok_spec = pl.BlockSpec((1, OUT_BLOCKS * blk, d), lambda bi, ti: (bi, ti, 0))
        resident = lambda shape: pl.BlockSpec(shape, lambda *_: (0,) * len(shape),
                                              pipeline_mode=pl.Buffered(1))
        h = pl.pallas_call(
            functools.partial(_out_kernel, d_ff=d_ff),
            grid=(b, nb // OUT_BLOCKS),
            in_specs=[
                tok_spec, o_spec, o_spec,
                _const_spec((MLA_WIDTH, 1)), _const_spec((SB_WIDTH, 1)),
                resident((MLA_WIDTH, d)), resident((SB_WIDTH, d)),
                _const_spec((1, d)),
                resident((d, d_ff)), resident((d, d_ff)), resident((d_ff, d)),
                _const_spec((1, d)),
            ],
            out_specs=tok_spec,
            out_shape=jax.ShapeDtypeStruct((b, s, d), F32),
            compiler_params=params,
            name="out_ffn",
        )(h, o_mla, o_sb, out_norm_mla[l][:, None], out_norm_sb[l][:, None],
          w_o[l][:MLA_WIDTH].astype(BF16), w_o[l][MLA_WIDTH:].astype(BF16),
          norm_ffn[l][None, :], w_gate_bf, w_up_bf, w_down_bf, norm_final[None, :])
    return h
```

```python
import functools
import math

import jax
import jax.numpy as jnp
from jax import lax
from jax.experimental import pallas as pl
from jax.experimental.pallas import tpu as pltpu

EPS = 1e-6
ROPE_THETA = 10000.0
LOG2_E = 1.4426950408889634

LANES = 128
BF16_SUBLANE_TILE = 16
MXU_DIM_V7X = 256
VMEM_BYTES_V7X = 64 * 1024 * 1024

MLA_HEADS = 8
MLA_NOPE_DIM = 64
MLA_ROPE_DIM = 32
MLA_V_DIM = 64
MLA_QK_DIM = MLA_NOPE_DIM + MLA_ROPE_DIM
MLA_PAD_DIM = LANES
MLA_VT_ROWS = LANES
Q_LORA_RANK = 256
KV_LORA_RANK = 128
SB_HEADS = 8
SB_HEAD_DIM = 64
MLA_WIDTH = MLA_HEADS * MLA_V_DIM
SB_WIDTH = SB_HEADS * SB_HEAD_DIM

SEQ_BLOCK = MXU_DIM_V7X
N_CHAINS = 4
MLA_K_PER_STEP = 4
SB_DEAD_BITS = 160.0
FF_CHUNK = MXU_DIM_V7X
OUT_BLOCKS = 2
PROJ_BLOCKS = 4
VMEM_LIMIT_BYTES = VMEM_BYTES_V7X * 7 // 8

F32 = jnp.float32
BF16 = jnp.bfloat16


def _rms_scale(v, axis):
    return lax.rsqrt(jnp.mean(v * v, axis=axis, keepdims=True) + EPS)


def _dot(a, b):
    return jnp.dot(a, b, preferred_element_type=F32)


def _dot_nt(a, b):
    return lax.dot_general(a, b, (((1,), (1,)), ((), ())), preferred_element_type=F32)


def _dot_tn(a, b):
    return lax.dot_general(a, b, (((0,), (0,)), ((), ())), preferred_element_type=F32)


def _proj_kernel(x_ref, pos_ref, freq_ref, spread_ref, off_rope_ref, g_mix_ref, g_q_ref, g_kv_ref,
                 w_lat_ref, w_sb_ref, w_uq_ref, w_uq_rot_ref, w_uk_ref, w_uv_ref,
                 w_gate_ref, w_up_ref, w_down_ref,
                 q_mla_ref, k_mla_ref, vt_mla_ref, q_sb_ref, k_sb_ref, vt_sb_ref,
                 w_gate_bf_ref, w_up_bf_ref, w_down_bf_ref):
    w_gate_bf_ref[...] = w_gate_ref[0].astype(BF16)
    w_up_bf_ref[...] = w_up_ref[0].astype(BF16)
    w_down_bf_ref[...] = w_down_ref[0].astype(BF16)

    x = x_ref[0]
    u = (x * _rms_scale(x, -1) * g_mix_ref[...]).astype(BF16)

    ang_t = freq_ref[...] * pos_ref[0].astype(F32)
    cos_t = jnp.cos(ang_t)
    sin_t = jnp.sin(ang_t)

    lat = _dot_nt(u, w_lat_ref[...])
    sb = _dot_nt(u, w_sb_ref[...])
    def spread(t):
        t1 = t.astype(BF16)
        r1 = t - t1.astype(F32)
        t2 = r1.astype(BF16)
        t3 = (r1 - t2.astype(F32)).astype(BF16)
        return _dot_tn(jnp.concatenate([t1, t2, t3], axis=0), spread_ref[...])

    cos = spread(cos_t) + off_rope_ref[...]
    sin = spread(sin_t)
    c_q = lat[:, :Q_LORA_RANK]
    c_kv = lat[:, Q_LORA_RANK:Q_LORA_RANK + KV_LORA_RANK]
    k_r = lat[:, Q_LORA_RANK + KV_LORA_RANK:Q_LORA_RANK + KV_LORA_RANK + MLA_PAD_DIM]
    k_r_rot = lat[:, Q_LORA_RANK + KV_LORA_RANK + MLA_PAD_DIM:]
    k_rope = k_r * cos + k_r_rot * sin

    ql = (c_q * _rms_scale(c_q, -1) * g_q_ref[...]).astype(BF16)
    kvl = (c_kv * _rms_scale(c_kv, -1) * g_kv_ref[...]).astype(BF16)

    q_lin = _dot(ql, w_uq_ref[...])
    q_rot = _dot(ql, w_uq_rot_ref[...])
    k_nope = _dot(kvl, w_uk_ref[...])
    v_mla = _dot(kvl, w_uv_ref[...])
    q_scale = LOG2_E / math.sqrt(MLA_QK_DIM)
    for h in range(MLA_HEADS):
        sl = slice(h * MLA_PAD_DIM, (h + 1) * MLA_PAD_DIM)
        q_h = (q_lin[:, sl] * cos + q_rot[:, sl] * sin) * q_scale
        q_mla_ref[0, h] = q_h.astype(BF16)
        k_mla_ref[0, h] = (k_nope[:, sl] + k_rope).astype(BF16)
    v_mla_t = v_mla.T
    pad_row = lax.broadcasted_iota(jnp.int32, (MLA_VT_ROWS - MLA_V_DIM, SEQ_BLOCK), 0)
    ones_then_zeros = jnp.where(pad_row == 0, 1.0, 0.0).astype(BF16)
    for h in range(MLA_HEADS):
        for t in range(PROJ_BLOCKS):
            vt_mla_ref[0, h, t, :MLA_V_DIM, :] = v_mla_t[h * MLA_V_DIM:(h + 1) * MLA_V_DIM,
                                                         t * SEQ_BLOCK:(t + 1) * SEQ_BLOCK].astype(BF16)
            vt_mla_ref[0, h, t, MLA_V_DIM:, :] = ones_then_zeros

    sb_scale = LOG2_E / math.sqrt(SB_HEAD_DIM)
    q_sb = sb[:, :SB_WIDTH] * sb_scale
    k_sb = sb[:, SB_WIDTH:2 * SB_WIDTH]
    v_sb_t = sb[:, 2 * SB_WIDTH:].T
    for h in range(SB_HEADS):
        sl = slice(h * SB_HEAD_DIM, (h + 1) * SB_HEAD_DIM)
        q_sb_ref[0, h] = q_sb[:, sl].astype(BF16)
        k_sb_ref[0, h] = k_sb[:, sl].astype(BF16)
        for t in range(PROJ_BLOCKS):
            vt_sb_ref[0, h, t] = v_sb_t[sl, t * SEQ_BLOCK:(t + 1) * SEQ_BLOCK].astype(BF16)


def _block_iotas():
    key_idx = lax.broadcasted_iota(jnp.int32, (SEQ_BLOCK, SEQ_BLOCK), 0)
    qry_idx = lax.broadcasted_iota(jnp.int32, (SEQ_BLOCK, SEQ_BLOCK), 1)
    return key_idx, qry_idx


def _attn_kernel(q_ref, k_ref, vt_ref, q_sb_ref, k_sb_ref, vt_sb_ref, tri_ref, o_ref, o_sb_ref,
                 z_scr, w_scr, acc_scr, m_scr, alpha_scr, *, n_super):
    blk, n = SEQ_BLOCK, N_CHAINS
    span = blk * n
    sb_stages, sb_tail = _sb_super_block(q_sb_ref, k_sb_ref, vt_sb_ref, tri_ref, o_sb_ref)

    def span_rows(ref, jb):
        return ref[0, 0, pl.ds(pl.multiple_of(jb * span, span), span), :]

    def vt_span(jb):
        return jnp.concatenate([vt_ref[0, 0, jb * n + i] for i in range(n)], axis=1)

    def softmax_span(s, first):
        s_max = jnp.max(s, axis=1, keepdims=True)
        if first:
            m_new = s_max
            alpha_scr[...] = jnp.ones_like(s_max)
        else:
            m_old = m_scr[...]
            m_new = jnp.maximum(m_old, s_max)
            alpha_scr[...] = jnp.exp2(m_old - m_new)
        m_scr[...] = m_new
        return jnp.exp2(s - m_new).astype(BF16)

    def super_body(sb, carry):
        base = sb * n
        side, sb_box = sb_stages(base)
        assert len(side) == n + 1
        side.pop(0)()
        q = span_rows(q_ref, sb)
        z_scr[1] = _dot_nt(q, span_rows(k_ref, sb))
        z_scr[0] = _dot_nt(q, span_rows(k_ref, jnp.maximum(sb - 1, 0)))
        side.pop(0)()
        qry_idx = lax.broadcasted_iota(jnp.int32, (span, span), 0)
        key_idx = lax.broadcasted_iota(jnp.int32, (span, span), 1)
        w_scr[1] = softmax_span(jnp.where(key_idx <= qry_idx, z_scr[1], -jnp.inf), True)
        while side:
            side.pop(0)()
        acc_scr[...] = jnp.zeros(acc_scr.shape, F32)

        def step(jb, slot):
            z_scr[1 - slot] = _dot_nt(span_rows(q_ref, sb), span_rows(k_ref, jnp.maximum(jb - 1, 0)))
            pv = _dot_nt(w_scr[1 - slot], vt_span(jb + 1))
            acc_scr[...] = alpha_scr[...] * acc_scr[...] + pv
            w_scr[slot] = softmax_span(z_scr[slot], False)

        def pair_body(it, c):
            jb = sb - 1 - 2 * it
            step(jb, 0)
            step(jb - 1, 1)
            return c

        lax.fori_loop(0, sb // 2, pair_body, 0)

        @pl.when(sb % 2 == 1)
        def _():
            step(0, 0)

        acc = alpha_scr[...] * acc_scr[...] + _dot_nt(w_scr[(sb + 1) % 2], vt_span(0))
        out = acc / acc[:, MLA_V_DIM:MLA_V_DIM + 1]
        for a in range(n):
            o_ref[0, 0, base + a] = out[a * blk:(a + 1) * blk].T[:MLA_V_DIM]
        sb_tail(base, sb_box)
        return carry

    lax.fori_loop(0, n_super, super_body, 0)


def _softplus2(z):
    return jnp.maximum(z, 0.0) + jnp.log2(1.0 + jnp.exp2(-jnp.abs(z)))


def _sb_super_block(q_ref, k_ref, vt_ref, tri_ref, o_ref):
    blk, n = SEQ_BLOCK, N_CHAINS
    key_idx, qry_idx = _block_iotas()
    strict = key_idx < qry_idx

    def rows(ref, j):
        return ref[0, 0, pl.ds(pl.multiple_of(j * blk, blk), blk), :]

    def logits(base, specs):
        return [_dot_nt(rows(k_ref, j), rows(q_ref, base + a)) for a, j, _ in specs]

    def suffix_sums(zs, specs):
        log_betas, sps = [], []
        for z, (_, _, diag) in zip(zs, specs):
            sp = _softplus2(z)
            log_betas.append(z - sp)
            if diag:
                sp = jnp.where(strict, sp, 0.0)
            sps.append(sp.astype(BF16))
        return log_betas, sps, [_dot(tri_ref[...], sp) for sp in sps]

    def products(log_betas, sps, sufs, specs):
        ws = []
        for log_beta, suf, (_, _, diag) in zip(log_betas, sufs, specs):
            w = jnp.exp2(log_beta + suf)
            if diag:
                w = jnp.where(strict, w, 0.0)
            ws.append(w.astype(BF16))
        return [(suf[0:1, :] - sp[0:1, :].astype(F32), _dot(vt_ref[0, 0, j], w))
                for w, sp, suf, (_, j, _) in zip(ws, sps, sufs, specs)]

    def tiles(base, groups):
        zs = [logits(base, g) for g in groups]
        out, prev = [], None
        for g, z in zip(groups, zs):
            cur = suffix_sums(z, g) + (g,)
            if prev is not None:
                out += products(*prev)
            prev = cur
        return out + products(*prev)

    def band(base, d):
        return [(a, jnp.maximum(base + a - d, 0), False) for a in range(n)]

    def factor(base, a, d, c):
        return jnp.where(base + a - d >= 0, jnp.exp2(c), 0.0)

    def live(cs):
        return (jnp.max(functools.reduce(jnp.maximum, cs)) > -SB_DEAD_BITS).astype(jnp.int32)

    def stages(base):
        groups = [[(a, base + a, True) for a in range(n)], band(base, 1)]
        box = {}

        def s0():
            box["z"] = [logits(base, g) for g in groups]

        def s1():
            box["c0"] = suffix_sums(box["z"][0], groups[0])

        def s2():
            box["c1"] = suffix_sums(box["z"][1], groups[1])
            box["r0"] = products(*box["c0"], groups[0])

        def s3():
            box["r1"] = products(*box["c1"], groups[1])

        def s4():
            cs = []
            for a in range(n):
                (sum0, pv0), (sum1, pv1) = box["r0"][a], box["r1"][a]
                o_ref[0, 0, base + a] = pv0 + pv1 * factor(base, a, 1, sum0)
                cs.append(sum0 + sum1)
            box["carry"] = cs

        return [s0, s1, s2, s3, s4], box

    def tail(base, box):
        def w_body(state):
            d, _, cs = state
            res = tiles(base, [band(base, d)])
            for a, (c, (_, pv)) in enumerate(zip(cs, res)):
                o_ref[0, 0, base + a] = o_ref[0, 0, base + a] + pv * factor(base, a, d, c)
            cs = tuple(c + block_sum for c, (block_sum, _) in zip(cs, res))
            return d + 1, live(cs), cs

        cs = box["carry"]
        lax.while_loop(lambda state: jnp.logical_and(state[0] < base + n, state[1] > 0), w_body,
                       (jnp.int32(2), live(cs), tuple(cs)))

    return stages, tail


def _out_kernel(x_ref, o_mla_ref, o_sb_ref, g_mla_ref, g_sb_ref, w_o_mla_ref, w_o_sb_ref,
                g_ffn_ref, w_gate_ref, w_up_ref, w_down_ref, g_final_ref, out_ref, *, d_ff):
    blk = SEQ_BLOCK

    def group(o_ref, g_ref, w_ref):
        parts = []
        for t in range(OUT_BLOCKS):
            o = o_ref[0, :, t].reshape(-1, blk)
            y = (o * _rms_scale(o, 0) * g_ref[...]).astype(BF16)
            parts.append(_dot_tn(y, w_ref[...]))
        return jnp.concatenate(parts, axis=0)

    h = x_ref[0] + group(o_mla_ref, g_mla_ref, w_o_mla_ref) + group(o_sb_ref, g_sb_ref, w_o_sb_ref)
    f = (h * _rms_scale(h, -1) * g_ffn_ref[...]).astype(BF16)
    ffn = jnp.zeros_like(h)
    for c in range(0, d_ff, FF_CHUNK):
        gate = _dot(f, w_gate_ref[:, c:c + FF_CHUNK])
        up = _dot(f, w_up_ref[:, c:c + FF_CHUNK])
        act = (gate * jax.nn.sigmoid(gate) * up).astype(BF16)
        ffn = ffn + _dot(act, w_down_ref[c:c + FF_CHUNK, :])
    h = h + ffn
    out_ref[0] = h * _rms_scale(h, -1) * g_final_ref[...]


def _rotate_half_cols(w):
    half = w.shape[-1] // 2
    return jnp.concatenate([-w[..., half:], w[..., :half]], axis=-1)


def _head_slots(nope, rope):
    ref = nope if nope is not None else rope
    r, h = ref.shape[0], ref.shape[1]
    nope = jnp.zeros((r, h, MLA_NOPE_DIM), ref.dtype) if nope is None else nope
    rope = jnp.zeros((r, h, MLA_ROPE_DIM), ref.dtype) if rope is None else rope
    pad = jnp.zeros((r, h, MLA_PAD_DIM - MLA_QK_DIM), ref.dtype)
    return jnp.concatenate([nope, rope, pad], axis=-1).reshape(r, h * MLA_PAD_DIM)


def _const_spec(shape):
    return pl.BlockSpec(shape, lambda *_: (0,) * len(shape))


def kernel(x, positions, norm_mix, w_in, q_latent_norm, w_uq, kv_latent_norm, w_ukv,
           out_norm_mla, out_norm_sb, w_o, norm_ffn, w_gate, w_up, w_down, norm_final):
    b, s, d = x.shape
    depth = w_in.shape[0]
    d_ff = w_gate.shape[-1]
    blk = SEQ_BLOCK
    nb = s // blk
    assert s % (blk * N_CHAINS) == 0 and nb % PROJ_BLOCKS == 0 and nb % OUT_BLOCKS == 0
    assert N_CHAINS % MLA_K_PER_STEP == 0 and MLA_K_PER_STEP % 2 == 0 and d_ff % FF_CHUNK == 0

    inv_freq = ROPE_THETA ** (-jnp.arange(0, MLA_ROPE_DIM, 2, dtype=F32) / MLA_ROPE_DIM)
    half = MLA_ROPE_DIM // 2
    lane = jnp.arange(MLA_PAD_DIM)[None, :]
    on_rope = (lane >= MLA_NOPE_DIM) & (lane < MLA_QK_DIM)
    spread = (on_rope & ((lane - MLA_NOPE_DIM) % half == jnp.arange(half)[:, None])).astype(BF16)
    spread = jnp.tile(spread, (3, 1))
    off_rope = (~on_rope).astype(F32)
    tri = jnp.where(jnp.arange(blk)[None, :] > jnp.arange(blk)[:, None], -1.0, 0.0).astype(BF16)
    pos = positions.reshape(b, 1, s)

    params = pltpu.CompilerParams(
        dimension_semantics=("arbitrary", "arbitrary"), vmem_limit_bytes=VMEM_LIMIT_BYTES)

    h = x
    for l in range(depth):
        o0 = Q_LORA_RANK
        o1 = o0 + KV_LORA_RANK
        o2 = o1 + MLA_ROPE_DIM
        w_in_t = jnp.swapaxes(w_in[l], 0, 1)
        w_cq, w_ckv, w_kr, w_sb = w_in_t[:o0], w_in_t[o0:o1], w_in_t[o1:o2], w_in_t[o2:]
        kr_slot = lambda w: jnp.pad(w, ((MLA_NOPE_DIM, MLA_PAD_DIM - MLA_QK_DIM), (0, 0)))
        w_kr_rot = jnp.concatenate([-w_kr[MLA_ROPE_DIM // 2:], w_kr[:MLA_ROPE_DIM // 2]], axis=0)
        w_lat = jnp.concatenate([w_cq, w_ckv, kr_slot(w_kr), kr_slot(w_kr_rot)],
                                axis=0).astype(BF16)
        uq = w_uq[l].reshape(Q_LORA_RANK, MLA_HEADS, MLA_QK_DIM)
        uq_nope, uq_rope = uq[..., :MLA_NOPE_DIM], uq[..., MLA_NOPE_DIM:]
        w_uq_lin = _head_slots(uq_nope, uq_rope).astype(BF16)
        w_uq_rot = _head_slots(None, _rotate_half_cols(uq_rope)).astype(BF16)
        ukv = w_ukv[l].reshape(KV_LORA_RANK, MLA_HEADS, MLA_NOPE_DIM + MLA_V_DIM)
        w_uk = _head_slots(ukv[..., :MLA_NOPE_DIM], None).astype(BF16)
        w_uv = ukv[..., MLA_NOPE_DIM:].reshape(KV_LORA_RANK, MLA_WIDTH).astype(BF16)

        head_major = lambda width: jax.ShapeDtypeStruct((b, MLA_HEADS, s, width), BF16)
        tile = PROJ_BLOCKS * blk
        head_spec = lambda width: pl.BlockSpec((1, MLA_HEADS, tile, width), lambda bi, ti: (bi, 0, ti, 0))
        vt_shape = lambda rows: jax.ShapeDtypeStruct((b, MLA_HEADS, nb, rows, blk), BF16)
        vt_spec = lambda rows: pl.BlockSpec((1, MLA_HEADS, PROJ_BLOCKS, rows, blk),
                                            lambda bi, ti: (bi, 0, ti, 0, 0))
        proj_steps = nb // PROJ_BLOCKS
        n_slabs = b * proj_steps
        assert d % (n_slabs * BF16_SUBLANE_TILE) == 0 and d_ff % (n_slabs * BF16_SUBLANE_TILE) == 0
        slab_in = lambda rows, cols: pl.BlockSpec(
            (1, rows // n_slabs, cols), lambda bi, ti: (l, bi * proj_steps + ti, 0))
        slab_out = lambda rows, cols: pl.BlockSpec(
            (rows // n_slabs, cols), lambda bi, ti: (bi * proj_steps + ti, 0))
        (q_mla, k_mla, vt_mla, q_sb, k_sb, vt_sb,
         w_gate_bf, w_up_bf, w_down_bf) = pl.pallas_call(
            _proj_kernel,
            grid=(b, proj_steps),
            in_specs=[
                pl.BlockSpec((1, tile, d), lambda bi, ti: (bi, ti, 0)),
                pl.BlockSpec((1, 1, tile), lambda bi, ti: (bi, 0, ti)),
                _const_spec((half, 1)),
                _const_spec((3 * half, MLA_PAD_DIM)),
                _const_spec((1, MLA_PAD_DIM)),
                _const_spec((1, d)),
                _const_spec((1, Q_LORA_RANK)),
                _const_spec((1, KV_LORA_RANK)),
                _const_spec(w_lat.shape),
                _const_spec((3 * SB_WIDTH, d)),
                _const_spec(w_uq_lin.shape),
                _const_spec(w_uq_rot.shape),
                _const_spec(w_uk.shape),
                _const_spec(w_uv.shape),
                slab_in(d, d_ff), slab_in(d, d_ff), slab_in(d_ff, d),
            ],
            out_specs=[head_spec(MLA_PAD_DIM), head_spec(MLA_PAD_DIM), vt_spec(MLA_VT_ROWS),
                       head_spec(SB_HEAD_DIM), head_spec(SB_HEAD_DIM), vt_spec(SB_HEAD_DIM),
                       slab_out(d, d_ff), slab_out(d, d_ff), slab_out(d_ff, d)],
            out_shape=[head_major(MLA_PAD_DIM), head_major(MLA_PAD_DIM), vt_shape(MLA_VT_ROWS),
                       head_major(SB_HEAD_DIM), head_major(SB_HEAD_DIM), vt_shape(SB_HEAD_DIM),
                       jax.ShapeDtypeStruct((d, d_ff), BF16), jax.ShapeDtypeStruct((d, d_ff), BF16),
                       jax.ShapeDtypeStruct((d_ff, d), BF16)],
            compiler_params=params,
            name="proj",
        )(h, pos, inv_freq[:, None], spread, off_rope,
          norm_mix[l][None, :], q_latent_norm[l][None, :], kv_latent_norm[l][None, :],
          w_lat, w_sb.astype(BF16), w_uq_lin, w_uq_rot, w_uk, w_uv, w_gate, w_up, w_down)

        seq_spec = lambda width: pl.BlockSpec((1, 1, s, width), lambda bi, hi: (bi, hi, 0, 0))
        blocked = lambda rows: pl.BlockSpec((1, 1, nb, rows, blk), lambda bi, hi: (bi, hi, 0, 0, 0))
        o_shape = jax.ShapeDtypeStruct((b, MLA_HEADS, nb, MLA_V_DIM, blk), F32)
        span = N_CHAINS * blk
        score_scratch = [pltpu.VMEM((2, span, span), F32),
                         pltpu.VMEM((2, span, span), BF16),
                         pltpu.VMEM((span, MLA_VT_ROWS), F32),
                         pltpu.VMEM((span, 1), F32),
                         pltpu.VMEM((span, 1), F32)]
        assert MLA_HEADS == SB_HEADS
        o_mla, o_sb = pl.pallas_call(
            functools.partial(_attn_kernel, n_super=nb // N_CHAINS),
            grid=(b, MLA_HEADS),
            in_specs=[seq_spec(MLA_PAD_DIM), seq_spec(MLA_PAD_DIM), blocked(MLA_VT_ROWS),
                      seq_spec(SB_HEAD_DIM), seq_spec(SB_HEAD_DIM), blocked(SB_HEAD_DIM),
                      _const_spec((blk, blk))],
            out_specs=[blocked(MLA_V_DIM), blocked(SB_HEAD_DIM)],
            out_shape=[o_shape, o_shape],
            scratch_shapes=score_scratch,
            compiler_params=params,
            name="attn",
        )(q_mla, k_mla, vt_mla, q_sb, k_sb, vt_sb, tri)

        assert depth == 1
        o_spec = pl.BlockSpec((1, MLA_HEADS, OUT_BLOCKS, MLA_V_DIM, blk), lambda bi, ti: (bi, 0, ti, 0, 0))
        tok_spec = pl.BlockSpec((1, OUT_BLOCKS * blk, d), lambda bi, ti: (bi, ti, 0))
        resident = lambda shape: pl.BlockSpec(shape, lambda *_: (0,) * len(shape),
                                              pipeline_mode=pl.Buffered(1))
        h = pl.pallas_call(
            functools.partial(_out_kernel, d_ff=d_ff),
            grid=(b, nb // OUT_BLOCKS),
            in_specs=[
                tok_spec, o_spec, o_spec,
                _const_spec((MLA_WIDTH, 1)), _const_spec((SB_WIDTH, 1)),
                resident((MLA_WIDTH, d)), resident((SB_WIDTH, d)),
                _const_spec((1, d)),
                resident((d, d_ff)), resident((d, d_ff)), resident((d_ff, d)),
                _const_spec((1, d)),
            ],
            out_specs=tok_spec,
            out_shape=jax.ShapeDtypeStruct((b, s, d), F32),
            compiler_params=params,
            name="out_ffn",
        )(h, o_mla, o_sb, out_norm_mla[l][:, None], out_norm_sb[l][:, None],
          w_o[l][:MLA_WIDTH].astype(BF16), w_o[l][MLA_WIDTH:].astype(BF16),
          norm_ffn[l][None, :], w_gate_bf, w_up_bf, w_down_bf, norm_final[None, :])
    return h
```

```python
import functools
import math

import jax
import jax.numpy as jnp
from jax import lax
from jax.experimental import pallas as pl
from jax.experimental.pallas import tpu as pltpu

EPS = 1e-6
ROPE_THETA = 10000.0
LOG2_E = 1.4426950408889634

LANES = 128
BF16_SUBLANE_TILE = 16
MXU_DIM_V7X = 256
VMEM_BYTES_V7X = 64 * 1024 * 1024

MLA_HEADS = 8
MLA_NOPE_DIM = 64
MLA_ROPE_DIM = 32
MLA_V_DIM = 64
MLA_QK_DIM = MLA_NOPE_DIM + MLA_ROPE_DIM
MLA_PAD_DIM = LANES
MLA_VT_ROWS = MLA_V_DIM + BF16_SUBLANE_TILE
Q_LORA_RANK = 256
KV_LORA_RANK = 128
SB_HEADS = 8
SB_HEAD_DIM = 64
MLA_WIDTH = MLA_HEADS * MLA_V_DIM
SB_WIDTH = SB_HEADS * SB_HEAD_DIM

SEQ_BLOCK = MXU_DIM_V7X
N_CHAINS = 4
MLA_K_PER_STEP = 4
SB_DEAD_BITS = 160.0
FF_CHUNK = MXU_DIM_V7X
OUT_BLOCKS = 2
PROJ_BLOCKS = 4
VMEM_LIMIT_BYTES = VMEM_BYTES_V7X * 7 // 8

F32 = jnp.float32
BF16 = jnp.bfloat16


def _rms_scale(v, axis):
    return lax.rsqrt(jnp.mean(v * v, axis=axis, keepdims=True) + EPS)


def _dot(a, b):
    return jnp.dot(a, b, preferred_element_type=F32)


def _dot_nt(a, b):
    return lax.dot_general(a, b, (((1,), (1,)), ((), ())), preferred_element_type=F32)


def _dot_tn(a, b):
    return lax.dot_general(a, b, (((0,), (0,)), ((), ())), preferred_element_type=F32)


def _proj_kernel(x_ref, pos_ref, freq_ref, spread_ref, off_rope_ref, g_mix_ref, g_q_ref, g_kv_ref,
                 w_lat_ref, w_sb_ref, w_uq_ref, w_uq_rot_ref, w_uk_ref, w_uv_ref,
                 w_gate_ref, w_up_ref, w_down_ref,
                 q_mla_ref, k_mla_ref, vt_mla_ref, q_sb_ref, k_sb_ref, vt_sb_ref,
                 w_gate_bf_ref, w_up_bf_ref, w_down_bf_ref):
    w_gate_bf_ref[...] = w_gate_ref[0].astype(BF16)
    w_up_bf_ref[...] = w_up_ref[0].astype(BF16)
    w_down_bf_ref[...] = w_down_ref[0].astype(BF16)

    x = x_ref[0]
    u = (x * _rms_scale(x, -1) * g_mix_ref[...]).astype(BF16)

    ang_t = freq_ref[...] * pos_ref[0].astype(F32)
    cos_t = jnp.cos(ang_t)
    sin_t = jnp.sin(ang_t)

    lat = _dot_nt(u, w_lat_ref[...])
    sb = _dot_nt(u, w_sb_ref[...])
    def spread(t):
        t1 = t.astype(BF16)
        r1 = t - t1.astype(F32)
        t2 = r1.astype(BF16)
        t3 = (r1 - t2.astype(F32)).astype(BF16)
        return _dot_tn(jnp.concatenate([t1, t2, t3], axis=0), spread_ref[...])

    cos = spread(cos_t) + off_rope_ref[...]
    sin = spread(sin_t)
    c_q = lat[:, :Q_LORA_RANK]
    c_kv = lat[:, Q_LORA_RANK:Q_LORA_RANK + KV_LORA_RANK]
    k_r = lat[:, Q_LORA_RANK + KV_LORA_RANK:Q_LORA_RANK + KV_LORA_RANK + MLA_PAD_DIM]
    k_r_rot = lat[:, Q_LORA_RANK + KV_LORA_RANK + MLA_PAD_DIM:]
    k_rope = k_r * cos + k_r_rot * sin

    ql = (c_q * _rms_scale(c_q, -1) * g_q_ref[...]).astype(BF16)
    kvl = (c_kv * _rms_scale(c_kv, -1) * g_kv_ref[...]).astype(BF16)

    q_lin = _dot(ql, w_uq_ref[...])
    q_rot = _dot(ql, w_uq_rot_ref[...])
    k_nope = _dot(kvl, w_uk_ref[...])
    v_mla = _dot(kvl, w_uv_ref[...])
    q_scale = LOG2_E / math.sqrt(MLA_QK_DIM)
    for h in range(MLA_HEADS):
        sl = slice(h * MLA_PAD_DIM, (h + 1) * MLA_PAD_DIM)
        q_h = (q_lin[:, sl] * cos + q_rot[:, sl] * sin) * q_scale
        q_mla_ref[0, h] = q_h.astype(BF16)
        k_mla_ref[0, h] = (k_nope[:, sl] + k_rope).astype(BF16)
    v_mla_t = v_mla.T
    pad_row = lax.broadcasted_iota(jnp.int32, (MLA_VT_ROWS - MLA_V_DIM, SEQ_BLOCK), 0)
    ones_then_zeros = jnp.where(pad_row == 0, 1.0, 0.0).astype(BF16)
    for h in range(MLA_HEADS):
        for t in range(PROJ_BLOCKS):
            vt_mla_ref[0, h, t, :MLA_V_DIM, :] = v_mla_t[h * MLA_V_DIM:(h + 1) * MLA_V_DIM,
                                                         t * SEQ_BLOCK:(t + 1) * SEQ_BLOCK].astype(BF16)
            vt_mla_ref[0, h, t, MLA_V_DIM:, :] = ones_then_zeros

    sb_scale = LOG2_E / math.sqrt(SB_HEAD_DIM)
    q_sb = sb[:, :SB_WIDTH] * sb_scale
    k_sb = sb[:, SB_WIDTH:2 * SB_WIDTH]
    v_sb_t = sb[:, 2 * SB_WIDTH:].T
    for h in range(SB_HEADS):
        sl = slice(h * SB_HEAD_DIM, (h + 1) * SB_HEAD_DIM)
        q_sb_ref[0, h] = q_sb[:, sl].astype(BF16)
        k_sb_ref[0, h] = k_sb[:, sl].astype(BF16)
        for t in range(PROJ_BLOCKS):
            vt_sb_ref[0, h, t] = v_sb_t[sl, t * SEQ_BLOCK:(t + 1) * SEQ_BLOCK].astype(BF16)


def _block_iotas():
    key_idx = lax.broadcasted_iota(jnp.int32, (SEQ_BLOCK, SEQ_BLOCK), 0)
    qry_idx = lax.broadcasted_iota(jnp.int32, (SEQ_BLOCK, SEQ_BLOCK), 1)
    return key_idx, qry_idx


def _attn_kernel(q_ref, k_ref, vt_ref, q_sb_ref, k_sb_ref, vt_sb_ref, tri_ref, o_ref, o_sb_ref,
                 z_scr, w_scr, *, n_super):
    blk, n, k_per_step = SEQ_BLOCK, N_CHAINS, MLA_K_PER_STEP
    key_idx, qry_idx = _block_iotas()
    causal = key_idx <= qry_idx
    unit = jnp.ones((1, blk), F32)
    sb_stages, sb_tail = _sb_super_block(q_sb_ref, k_sb_ref, vt_sb_ref, tri_ref, o_sb_ref)

    def rows(ref, j):
        return ref[0, 0, pl.ds(pl.multiple_of(j * blk, blk), blk), :]

    def softmax_blocks(zs, ms, diag_chain):
        new_ms, ws, alphas = [], [], []
        for i, (s, m) in enumerate(zip(zs, ms)):
            if i == diag_chain:
                s = jnp.where(causal, s, -jnp.inf)
            s_max = jnp.max(s, axis=0, keepdims=True)
            if m is None:
                m_new, alpha = s_max, None
            else:
                m_new = jnp.maximum(m, s_max)
                alpha = jnp.exp2(m - m_new)
            new_ms.append(m_new)
            ws.append(jnp.exp2(s - m_new).astype(BF16))
            alphas.append(alpha)
        return new_ms, ws, alphas

    def rescale_add(acc, pv, alpha):
        return pv if acc is None else alpha * acc + pv

    def super_body(sb, carry):
        base = sb * n

        def scores(j, chains):
            k = rows(k_ref, j)
            return [_dot_nt(k, rows(q_ref, base + a)) for a in chains]

        ms, accs = [None] * n, [None] * n
        side, sb_box = sb_stages(base)
        assert len(side) == n + 1
        zs_of = {n - 1: scores(base + n - 1, [n - 1])}
        side.pop(0)()
        for kb in reversed(range(n - 1)):
            zs_of[kb] = scores(base + kb, range(kb, n))
        for a, z in enumerate(scores(jnp.maximum(base - 1, 0), range(n))):
            z_scr[0, a] = z
        pending = None
        for kb in reversed(range(n)):
            chains = list(range(kb, n))
            zs = zs_of[kb]
            if pending is not None:
                vt = vt_ref[0, 0, base + kb + 1]
                for a, w, alpha in zip(*pending):
                    accs[a] = rescale_add(accs[a], _dot(vt, w), alpha)
            new_ms, ws, alphas = softmax_blocks(zs, [ms[a] for a in chains], 0)
            for a, m in zip(chains, new_ms):
                ms[a] = m
            side.pop(0)()
            if kb > 0:
                pending = (chains, ws, alphas)
            else:
                for a in range(n):
                    w_scr[a] = ws[a]
                accs[0] = jnp.zeros((vt_ref.shape[3], blk), F32)
                alphas[0] = unit

        def k_body(blocks, first):
            def body(it, state):
                ms, accs, alphas = (list(t) for t in state)
                j = first - blocks * it
                for u in range(blocks):
                    slot, j_cur, j_next = u % 2, j - u, jnp.maximum(j - u - 1, 0)
                    k_next, vt_prev = rows(k_ref, j_next), vt_ref[0, 0, j_cur + 1]
                    for a in range(n):
                        z_scr[1 - slot, a] = _dot_nt(k_next, rows(q_ref, base + a))
                        accs[a] = rescale_add(accs[a], _dot(vt_prev, w_scr[a]), alphas[a])
                    ms, ws, alphas = softmax_blocks([z_scr[slot, a] for a in range(n)], ms, None)
                    for a in range(n):
                        w_scr[a] = ws[a]
                return tuple(ms), tuple(accs), tuple(alphas)
            return body

        n_short = base // k_per_step
        n_long = n_short // 2
        state = lax.fori_loop(0, n_long, k_body(2 * k_per_step, base - 1),
                              (tuple(ms), tuple(accs), tuple(alphas)))
        ms, accs, alphas = lax.fori_loop(
            0, n_short - 2 * n_long, k_body(k_per_step, base - 1 - 2 * k_per_step * n_long), state)
        vt = vt_ref[0, 0, 0]
        accs = [rescale_add(acc, _dot(vt, w_scr[a]), alpha)
                for a, (acc, alpha) in enumerate(zip(accs, alphas))]
        for a in range(n):
            o_ref[0, 0, base + a] = accs[a][:MLA_V_DIM] / accs[a][MLA_V_DIM:MLA_V_DIM + 1]
        sb_tail(base, sb_box)
        return carry

    lax.fori_loop(0, n_super, super_body, 0)


def _softplus2(z):
    return jnp.maximum(z, 0.0) + jnp.log2(1.0 + jnp.exp2(-jnp.abs(z)))


def _sb_super_block(q_ref, k_ref, vt_ref, tri_ref, o_ref):
    blk, n = SEQ_BLOCK, N_CHAINS
    key_idx, qry_idx = _block_iotas()
    strict = key_idx < qry_idx

    def rows(ref, j):
        return ref[0, 0, pl.ds(pl.multiple_of(j * blk, blk), blk), :]

    def logits(base, specs):
        return [_dot_nt(rows(k_ref, j), rows(q_ref, base + a)) for a, j, _ in specs]

    def suffix_sums(zs, specs):
        log_betas, sps = [], []
        for z, (_, _, diag) in zip(zs, specs):
            sp = _softplus2(z)
            log_betas.append(z - sp)
            if diag:
                sp = jnp.where(strict, sp, 0.0)
            sps.append(sp.astype(BF16))
        return log_betas, sps, [_dot(tri_ref[...], sp) for sp in sps]

    def products(log_betas, sps, sufs, specs):
        ws = []
        for log_beta, suf, (_, _, diag) in zip(log_betas, sufs, specs):
            w = jnp.exp2(log_beta + suf)
            if diag:
                w = jnp.where(strict, w, 0.0)
            ws.append(w.astype(BF16))
        return [(suf[0:1, :] - sp[0:1, :].astype(F32), _dot(vt_ref[0, 0, j], w))
                for w, sp, suf, (_, j, _) in zip(ws, sps, sufs, specs)]

    def tiles(base, groups):
        zs = [logits(base, g) for g in groups]
        out, prev = [], None
        for g, z in zip(groups, zs):
            cur = suffix_sums(z, g) + (g,)
            if prev is not None:
                out += products(*prev)
            prev = cur
        return out + products(*prev)

    def band(base, d):
        return [(a, jnp.maximum(base + a - d, 0), False) for a in range(n)]

    def factor(base, a, d, c):
        return jnp.where(base + a - d >= 0, jnp.exp2(c), 0.0)

    def live(cs):
        return (jnp.max(functools.reduce(jnp.maximum, cs)) > -SB_DEAD_BITS).astype(jnp.int32)

    def stages(base):
        groups = [[(a, base + a, True) for a in range(n)], band(base, 1)]
        box = {}

        def s0():
            box["z"] = [logits(base, g) for g in groups]

        def s1():
            box["c0"] = suffix_sums(box["z"][0], groups[0])

        def s2():
            box["c1"] = suffix_sums(box["z"][1], groups[1])
            box["r0"] = products(*box["c0"], groups[0])

        def s3():
            box["r1"] = products(*box["c1"], groups[1])

        def s4():
            cs = []
            for a in range(n):
                (sum0, pv0), (sum1, pv1) = box["r0"][a], box["r1"][a]
                o_ref[0, 0, base + a] = pv0 + pv1 * factor(base, a, 1, sum0)
                cs.append(sum0 + sum1)
            box["carry"] = cs

        return [s0, s1, s2, s3, s4], box

    def tail(base, box):
        def w_body(state):
            d, _, cs = state
            res = tiles(base, [band(base, d)])
            for a, (c, (_, pv)) in enumerate(zip(cs, res)):
                o_ref[0, 0, base + a] = o_ref[0, 0, base + a] + pv * factor(base, a, d, c)
            cs = tuple(c + block_sum for c, (block_sum, _) in zip(cs, res))
            return d + 1, live(cs), cs

        cs = box["carry"]
        lax.while_loop(lambda state: jnp.logical_and(state[0] < base + n, state[1] > 0), w_body,
                       (jnp.int32(2), live(cs), tuple(cs)))

    return stages, tail


def _out_kernel(x_ref, o_mla_ref, o_sb_ref, g_mla_ref, g_sb_ref, w_o_mla_ref, w_o_sb_ref,
                g_ffn_ref, w_gate_ref, w_up_ref, w_down_ref, g_final_ref, out_ref, *, d_ff):
    blk = SEQ_BLOCK

    def group(o_ref, g_ref, w_ref):
        parts = []
        for t in range(OUT_BLOCKS):
            o = o_ref[0, :, t].reshape(-1, blk)
            y = (o * _rms_scale(o, 0) * g_ref[...]).astype(BF16)
            parts.append(_dot_tn(y, w_ref[...]))
        return jnp.concatenate(parts, axis=0)

    h = x_ref[0] + group(o_mla_ref, g_mla_ref, w_o_mla_ref) + group(o_sb_ref, g_sb_ref, w_o_sb_ref)
    f = (h * _rms_scale(h, -1) * g_ffn_ref[...]).astype(BF16)
    ffn = jnp.zeros_like(h)
    for c in range(0, d_ff, FF_CHUNK):
        gate = _dot(f, w_gate_ref[:, c:c + FF_CHUNK])
        up = _dot(f, w_up_ref[:, c:c + FF_CHUNK])
        act = (gate * jax.nn.sigmoid(gate) * up).astype(BF16)
        ffn = ffn + _dot(act, w_down_ref[c:c + FF_CHUNK, :])
    h = h + ffn
    out_ref[0] = h * _rms_scale(h, -1) * g_final_ref[...]


def _rotate_half_cols(w):
    half = w.shape[-1] // 2
    return jnp.concatenate([-w[..., half:], w[..., :half]], axis=-1)


def _head_slots(nope, rope):
    ref = nope if nope is not None else rope
    r, h = ref.shape[0], ref.shape[1]
    nope = jnp.zeros((r, h, MLA_NOPE_DIM), ref.dtype) if nope is None else nope
    rope = jnp.zeros((r, h, MLA_ROPE_DIM), ref.dtype) if rope is None else rope
    pad = jnp.zeros((r, h, MLA_PAD_DIM - MLA_QK_DIM), ref.dtype)
    return jnp.concatenate([nope, rope, pad], axis=-1).reshape(r, h * MLA_PAD_DIM)


def _const_spec(shape):
    return pl.BlockSpec(shape, lambda *_: (0,) * len(shape))


def kernel(x, positions, norm_mix, w_in, q_latent_norm, w_uq, kv_latent_norm, w_ukv,
           out_norm_mla, out_norm_sb, w_o, norm_ffn, w_gate, w_up, w_down, norm_final):
    b, s, d = x.shape
    depth = w_in.shape[0]
    d_ff = w_gate.shape[-1]
    blk = SEQ_BLOCK
    nb = s // blk
    assert s % (blk * N_CHAINS) == 0 and nb % PROJ_BLOCKS == 0 and nb % OUT_BLOCKS == 0
    assert N_CHAINS % MLA_K_PER_STEP == 0 and MLA_K_PER_STEP % 2 == 0 and d_ff % FF_CHUNK == 0

    inv_freq = ROPE_THETA ** (-jnp.arange(0, MLA_ROPE_DIM, 2, dtype=F32) / MLA_ROPE_DIM)
    half = MLA_ROPE_DIM // 2
    lane = jnp.arange(MLA_PAD_DIM)[None, :]
    on_rope = (lane >= MLA_NOPE_DIM) & (lane < MLA_QK_DIM)
    spread = (on_rope & ((lane - MLA_NOPE_DIM) % half == jnp.arange(half)[:, None])).astype(BF16)
    spread = jnp.tile(spread, (3, 1))
    off_rope = (~on_rope).astype(F32)
    tri = jnp.where(jnp.arange(blk)[None, :] > jnp.arange(blk)[:, None], -1.0, 0.0).astype(BF16)
    pos = positions.reshape(b, 1, s)

    params = pltpu.CompilerParams(
        dimension_semantics=("arbitrary", "arbitrary"), vmem_limit_bytes=VMEM_LIMIT_BYTES)

    h = x
    for l in range(depth):
        o0 = Q_LORA_RANK
        o1 = o0 + KV_LORA_RANK
        o2 = o1 + MLA_ROPE_DIM
        w_in_t = jnp.swapaxes(w_in[l], 0, 1)
        w_cq, w_ckv, w_kr, w_sb = w_in_t[:o0], w_in_t[o0:o1], w_in_t[o1:o2], w_in_t[o2:]
        kr_slot = lambda w: jnp.pad(w, ((MLA_NOPE_DIM, MLA_PAD_DIM - MLA_QK_DIM), (0, 0)))
        w_kr_rot = jnp.concatenate([-w_kr[MLA_ROPE_DIM // 2:], w_kr[:MLA_ROPE_DIM // 2]], axis=0)
        w_lat = jnp.concatenate([w_cq, w_ckv, kr_slot(w_kr), kr_slot(w_kr_rot)],
                                axis=0).astype(BF16)
        uq = w_uq[l].reshape(Q_LORA_RANK, MLA_HEADS, MLA_QK_DIM)
        uq_nope, uq_rope = uq[..., :MLA_NOPE_DIM], uq[..., MLA_NOPE_DIM:]
        w_uq_lin = _head_slots(uq_nope, uq_rope).astype(BF16)
        w_uq_rot = _head_slots(None, _rotate_half_cols(uq_rope)).astype(BF16)
        ukv = w_ukv[l].reshape(KV_LORA_RANK, MLA_HEADS, MLA_NOPE_DIM + MLA_V_DIM)
        w_uk = _head_slots(ukv[..., :MLA_NOPE_DIM], None).astype(BF16)
        w_uv = ukv[..., MLA_NOPE_DIM:].reshape(KV_LORA_RANK, MLA_WIDTH).astype(BF16)

        head_major = lambda width: jax.ShapeDtypeStruct((b, MLA_HEADS, s, width), BF16)
        tile = PROJ_BLOCKS * blk
        head_spec = lambda width: pl.BlockSpec((1, MLA_HEADS, tile, width), lambda bi, ti: (bi, 0, ti, 0))
        vt_shape = lambda rows: jax.ShapeDtypeStruct((b, MLA_HEADS, nb, rows, blk), BF16)
        vt_spec = lambda rows: pl.BlockSpec((1, MLA_HEADS, PROJ_BLOCKS, rows, blk),
                                            lambda bi, ti: (bi, 0, ti, 0, 0))
        proj_steps = nb // PROJ_BLOCKS
        n_slabs = b * proj_steps
        assert d % (n_slabs * BF16_SUBLANE_TILE) == 0 and d_ff % (n_slabs * BF16_SUBLANE_TILE) == 0
        slab_in = lambda rows, cols: pl.BlockSpec(
            (1, rows // n_slabs, cols), lambda bi, ti: (l, bi * proj_steps + ti, 0))
        slab_out = lambda rows, cols: pl.BlockSpec(
            (rows // n_slabs, cols), lambda bi, ti: (bi * proj_steps + ti, 0))
        (q_mla, k_mla, vt_mla, q_sb, k_sb, vt_sb,
         w_gate_bf, w_up_bf, w_down_bf) = pl.pallas_call(
            _proj_kernel,
            grid=(b, proj_steps),
            in_specs=[
                pl.BlockSpec((1, tile, d), lambda bi, ti: (bi, ti, 0)),
                pl.BlockSpec((1, 1, tile), lambda bi, ti: (bi, 0, ti)),
                _const_spec((half, 1)),
                _const_spec((3 * half, MLA_PAD_DIM)),
                _const_spec((1, MLA_PAD_DIM)),
                _const_spec((1, d)),
                _const_spec((1, Q_LORA_RANK)),
                _const_spec((1, KV_LORA_RANK)),
                _const_spec(w_lat.shape),
                _const_spec((3 * SB_WIDTH, d)),
                _const_spec(w_uq_lin.shape),
                _const_spec(w_uq_rot.shape),
                _const_spec(w_uk.shape),
                _const_spec(w_uv.shape),
                slab_in(d, d_ff), slab_in(d, d_ff), slab_in(d_ff, d),
            ],
            out_specs=[head_spec(MLA_PAD_DIM), head_spec(MLA_PAD_DIM), vt_spec(MLA_VT_ROWS),
                       head_spec(SB_HEAD_DIM), head_spec(SB_HEAD_DIM), vt_spec(SB_HEAD_DIM),
                       slab_out(d, d_ff), slab_out(d, d_ff), slab_out(d_ff, d)],
            out_shape=[head_major(MLA_PAD_DIM), head_major(MLA_PAD_DIM), vt_shape(MLA_VT_ROWS),
                       head_major(SB_HEAD_DIM), head_major(SB_HEAD_DIM), vt_shape(SB_HEAD_DIM),
                       jax.ShapeDtypeStruct((d, d_ff), BF16), jax.ShapeDtypeStruct((d, d_ff), BF16),
                       jax.ShapeDtypeStruct((d_ff, d), BF16)],
            compiler_params=params,
            name="proj",
        )(h, pos, inv_freq[:, None], spread, off_rope,
          norm_mix[l][None, :], q_latent_norm[l][None, :], kv_latent_norm[l][None, :],
          w_lat, w_sb.astype(BF16), w_uq_lin, w_uq_rot, w_uk, w_uv, w_gate, w_up, w_down)

        seq_spec = lambda width: pl.BlockSpec((1, 1, s, width), lambda bi, hi: (bi, hi, 0, 0))
        blocked = lambda rows: pl.BlockSpec((1, 1, nb, rows, blk), lambda bi, hi: (bi, hi, 0, 0, 0))
        o_shape = jax.ShapeDtypeStruct((b, MLA_HEADS, nb, MLA_V_DIM, blk), F32)
        score_scratch = [pltpu.VMEM((2, N_CHAINS, blk, blk), F32),
                         pltpu.VMEM((N_CHAINS, blk, blk), BF16)]
        assert MLA_HEADS == SB_HEADS
        o_mla, o_sb = pl.pallas_call(
            functools.partial(_attn_kernel, n_super=nb // N_CHAINS),
            grid=(b, MLA_HEADS),
            in_specs=[seq_spec(MLA_PAD_DIM), seq_spec(MLA_PAD_DIM), blocked(MLA_VT_ROWS),
                      seq_spec(SB_HEAD_DIM), seq_spec(SB_HEAD_DIM), blocked(SB_HEAD_DIM),
                      _const_spec((blk, blk))],
            out_specs=[blocked(MLA_V_DIM), blocked(SB_HEAD_DIM)],
            out_shape=[o_shape, o_shape],
            scratch_shapes=score_scratch,
            compiler_params=params,
            name="attn",
        )(q_mla, k_mla, vt_mla, q_sb, k_sb, vt_sb, tri)

        assert depth == 1
        o_spec = pl.BlockSpec((1, MLA_HEADS, OUT_BLOCKS, MLA_V_DIM, blk), lambda bi, ti: (bi, 0, ti, 0, 0))
        tok_spec = pl.BlockSpec((1, OUT_BLOCKS * blk, d), lambda bi, ti: (bi, ti, 0))
        resident = lambda shape: pl.BlockSpec(shape, lambda *_: (0,) * len(shape),
                                              pipeline_mode=pl.Buffered(1))
        h = pl.pallas_call(
            functools.partial(_out_kernel, d_ff=d_ff),
            grid=(b, nb // OUT_BLOCKS),
            in_specs=[
                tok_spec, o_spec, o_spec,
                _const_spec((MLA_WIDTH, 1)), _const_spec((SB_WIDTH, 1)),
                resident((MLA_WIDTH, d)), resident((SB_WIDTH, d)),
                _const_spec((1, d)),
                resident((d, d_ff)), resident((d, d_ff)), resident((d_ff, d)),
                _const_spec((1, d)),
            ],
            out_specs=tok_spec,
            out_shape=jax.ShapeDtypeStruct((b, s, d), F32),
            compiler_params=params,
            name="out_ffn",
        )(h, o_mla, o_sb, out_norm_mla[l][:, None], out_norm_sb[l][:, None],
          w_o[l][:MLA_WIDTH].astype(BF16), w_o[l][MLA_WIDTH:].astype(BF16),
          norm_ffn[l][None, :], w_gate_bf, w_up_bf, w_down_bf, norm_final[None, :])
    return h
```

```python
import functools
import math

import jax
import jax.numpy as jnp
from jax import lax
from jax.experimental import pallas as pl
from jax.experimental.pallas import tpu as pltpu

EPS = 1e-6
ROPE_THETA = 10000.0
LOG2_E = 1.4426950408889634

LANES = 128
BF16_SUBLANE_TILE = 16
MXU_DIM_V7X = 256
VMEM_BYTES_V7X = 64 * 1024 * 1024

MLA_HEADS = 8
MLA_NOPE_DIM = 64
MLA_ROPE_DIM = 32
MLA_V_DIM = 64
MLA_QK_DIM = MLA_NOPE_DIM + MLA_ROPE_DIM
MLA_PAD_DIM = LANES
MLA_VT_ROWS = MLA_V_DIM + BF16_SUBLANE_TILE
Q_LORA_RANK = 256
KV_LORA_RANK = 128
SB_HEADS = 8
SB_HEAD_DIM = 64
MLA_WIDTH = MLA_HEADS * MLA_V_DIM
SB_WIDTH = SB_HEADS * SB_HEAD_DIM

SEQ_BLOCK = MXU_DIM_V7X
N_CHAINS = 4
MLA_K_PER_STEP = 4
MLA_STEP_SIZES = (4, 2, 1)
SB_DEAD_BITS = 160.0
FF_CHUNK = MXU_DIM_V7X
OUT_BLOCKS = 2
PROJ_BLOCKS = 4
VMEM_LIMIT_BYTES = VMEM_BYTES_V7X * 7 // 8

F32 = jnp.float32
BF16 = jnp.bfloat16


def _rms_scale(v, axis):
    return lax.rsqrt(jnp.mean(v * v, axis=axis, keepdims=True) + EPS)


def _dot(a, b):
    return jnp.dot(a, b, preferred_element_type=F32)


def _dot_nt(a, b):
    return lax.dot_general(a, b, (((1,), (1,)), ((), ())), preferred_element_type=F32)


def _dot_tn(a, b):
    return lax.dot_general(a, b, (((0,), (0,)), ((), ())), preferred_element_type=F32)


def _proj_kernel(x_ref, pos_ref, freq_ref, spread_ref, off_rope_ref, g_mix_ref, g_q_ref, g_kv_ref,
                 w_lat_ref, w_sb_ref, w_uq_ref, w_uq_rot_ref, w_uk_ref, w_uv_ref,
                 w_gate_ref, w_up_ref, w_down_ref,
                 q_mla_ref, k_mla_ref, vt_mla_ref, q_sb_ref, k_sb_ref, vt_sb_ref,
                 w_gate_bf_ref, w_up_bf_ref, w_down_bf_ref):
    w_gate_bf_ref[...] = w_gate_ref[0].astype(BF16)
    w_up_bf_ref[...] = w_up_ref[0].astype(BF16)
    w_down_bf_ref[...] = w_down_ref[0].astype(BF16)

    x = x_ref[0]
    u = (x * _rms_scale(x, -1) * g_mix_ref[...]).astype(BF16)

    ang_t = freq_ref[...] * pos_ref[0].astype(F32)
    cos_t = jnp.cos(ang_t)
    sin_t = jnp.sin(ang_t)

    lat = _dot_nt(u, w_lat_ref[...])
    sb = _dot_nt(u, w_sb_ref[...])
    def spread(t):
        t1 = t.astype(BF16)
        r1 = t - t1.astype(F32)
        t2 = r1.astype(BF16)
        t3 = (r1 - t2.astype(F32)).astype(BF16)
        return _dot_tn(jnp.concatenate([t1, t2, t3], axis=0), spread_ref[...])

    cos = spread(cos_t) + off_rope_ref[...]
    sin = spread(sin_t)
    c_q = lat[:, :Q_LORA_RANK]
    c_kv = lat[:, Q_LORA_RANK:Q_LORA_RANK + KV_LORA_RANK]
    k_r = lat[:, Q_LORA_RANK + KV_LORA_RANK:Q_LORA_RANK + KV_LORA_RANK + MLA_PAD_DIM]
    k_r_rot = lat[:, Q_LORA_RANK + KV_LORA_RANK + MLA_PAD_DIM:]
    k_rope = k_r * cos + k_r_rot * sin

    ql = (c_q * _rms_scale(c_q, -1) * g_q_ref[...]).astype(BF16)
    kvl = (c_kv * _rms_scale(c_kv, -1) * g_kv_ref[...]).astype(BF16)

    q_lin = _dot(ql, w_uq_ref[...])
    q_rot = _dot(ql, w_uq_rot_ref[...])
    k_nope = _dot(kvl, w_uk_ref[...])
    v_mla = _dot(kvl, w_uv_ref[...])
    q_scale = LOG2_E / math.sqrt(MLA_QK_DIM)
    for h in range(MLA_HEADS):
        sl = slice(h * MLA_PAD_DIM, (h + 1) * MLA_PAD_DIM)
        q_h = (q_lin[:, sl] * cos + q_rot[:, sl] * sin) * q_scale
        q_mla_ref[0, h] = q_h.astype(BF16)
        k_mla_ref[0, h] = (k_nope[:, sl] + k_rope).astype(BF16)
    v_mla_t = v_mla.T
    pad_row = lax.broadcasted_iota(jnp.int32, (MLA_VT_ROWS - MLA_V_DIM, SEQ_BLOCK), 0)
    ones_then_zeros = jnp.where(pad_row == 0, 1.0, 0.0).astype(BF16)
    for h in range(MLA_HEADS):
        for t in range(PROJ_BLOCKS):
            vt_mla_ref[0, h, t, :MLA_V_DIM, :] = v_mla_t[h * MLA_V_DIM:(h + 1) * MLA_V_DIM,
                                                         t * SEQ_BLOCK:(t + 1) * SEQ_BLOCK].astype(BF16)
            vt_mla_ref[0, h, t, MLA_V_DIM:, :] = ones_then_zeros

    sb_scale = LOG2_E / math.sqrt(SB_HEAD_DIM)
    q_sb = sb[:, :SB_WIDTH] * sb_scale
    k_sb = sb[:, SB_WIDTH:2 * SB_WIDTH]
    v_sb_t = sb[:, 2 * SB_WIDTH:].T
    for h in range(SB_HEADS):
        sl = slice(h * SB_HEAD_DIM, (h + 1) * SB_HEAD_DIM)
        q_sb_ref[0, h] = q_sb[:, sl].astype(BF16)
        k_sb_ref[0, h] = k_sb[:, sl].astype(BF16)
        for t in range(PROJ_BLOCKS):
            vt_sb_ref[0, h, t] = v_sb_t[sl, t * SEQ_BLOCK:(t + 1) * SEQ_BLOCK].astype(BF16)


def _block_iotas():
    key_idx = lax.broadcasted_iota(jnp.int32, (SEQ_BLOCK, SEQ_BLOCK), 0)
    qry_idx = lax.broadcasted_iota(jnp.int32, (SEQ_BLOCK, SEQ_BLOCK), 1)
    return key_idx, qry_idx


def _attn_kernel(q_ref, k_ref, vt_ref, q_sb_ref, k_sb_ref, vt_sb_ref, tri_ref, o_ref, o_sb_ref,
                 z_scr, w_scr, *, n_super):
    blk, n, k_per_step = SEQ_BLOCK, N_CHAINS, MLA_K_PER_STEP
    key_idx, qry_idx = _block_iotas()
    causal = key_idx <= qry_idx
    unit = jnp.ones((1, blk), F32)
    sb_stages, sb_tail = _sb_super_block(q_sb_ref, k_sb_ref, vt_sb_ref, tri_ref, o_sb_ref)

    def rows(ref, j):
        return ref[0, 0, pl.ds(pl.multiple_of(j * blk, blk), blk), :]

    def softmax_blocks(zs, ms, diag_chain):
        new_ms, ws, alphas = [], [], []
        for i, (s, m) in enumerate(zip(zs, ms)):
            if i == diag_chain:
                s = jnp.where(causal, s, -jnp.inf)
            s_max = jnp.max(s, axis=0, keepdims=True)
            if m is None:
                m_new, alpha = s_max, None
            else:
                m_new = jnp.maximum(m, s_max)
                alpha = jnp.exp2(m - m_new)
            new_ms.append(m_new)
            ws.append(jnp.exp2(s - m_new).astype(BF16))
            alphas.append(alpha)
        return new_ms, ws, alphas

    def rescale_add(acc, pv, alpha):
        return pv if acc is None else alpha * acc + pv

    def super_body(sb, carry):
        base = sb * n

        def scores(j, chains):
            k = rows(k_ref, j)
            return [_dot_nt(k, rows(q_ref, base + a)) for a in chains]

        ms, accs = [None] * n, [None] * n
        side, sb_box = sb_stages(base)
        assert len(side) == n + 1
        zs_of = {n - 1: scores(base + n - 1, [n - 1])}
        side.pop(0)()
        for kb in reversed(range(n - 1)):
            zs_of[kb] = scores(base + kb, range(kb, n))
        for a, z in enumerate(scores(jnp.maximum(base - 1, 0), range(n))):
            z_scr[0, a] = z
        pending = None
        for kb in reversed(range(n)):
            chains = list(range(kb, n))
            zs = zs_of[kb]
            if pending is not None:
                vt = vt_ref[0, 0, base + kb + 1]
                for a, w, alpha in zip(*pending):
                    accs[a] = rescale_add(accs[a], _dot(vt, w), alpha)
            new_ms, ws, alphas = softmax_blocks(zs, [ms[a] for a in chains], 0)
            for a, m in zip(chains, new_ms):
                ms[a] = m
            side.pop(0)()
            if kb > 0:
                pending = (chains, ws, alphas)
            else:
                for a in range(n):
                    w_scr[a] = ws[a]
                accs[0] = jnp.zeros((vt_ref.shape[3], blk), F32)
                alphas[0] = unit

        def k_body(blocks, first):
            def body(it, state):
                ms, accs, alphas = (list(t) for t in state)
                j = first - blocks * it
                for u in range(blocks):
                    slot, j_cur, j_next = u % 2, j - u, jnp.maximum(j - u - 1, 0)
                    k_next, vt_prev = rows(k_ref, j_next), vt_ref[0, 0, j_cur + 1]
                    for a in range(n):
                        z_scr[1 - slot, a] = _dot_nt(k_next, rows(q_ref, base + a))
                        accs[a] = rescale_add(accs[a], _dot(vt_prev, w_scr[a]), alphas[a])
                    ms, ws, alphas = softmax_blocks([z_scr[slot, a] for a in range(n)], ms, None)
                    for a in range(n):
                        w_scr[a] = ws[a]
                return tuple(ms), tuple(accs), tuple(alphas)
            return body

        todo, first = base // k_per_step, base - 1
        state = (tuple(ms), tuple(accs), tuple(alphas))
        for size in MLA_STEP_SIZES:
            steps = todo // size
            state = lax.fori_loop(0, steps, k_body(size * k_per_step, first), state)
            todo, first = todo - steps * size, first - steps * size * k_per_step
        ms, accs, alphas = state
        vt = vt_ref[0, 0, 0]
        accs = [rescale_add(acc, _dot(vt, w_scr[a]), alpha)
                for a, (acc, alpha) in enumerate(zip(accs, alphas))]
        for a in range(n):
            o_ref[0, 0, base + a] = accs[a][:MLA_V_DIM] / accs[a][MLA_V_DIM:MLA_V_DIM + 1]
        sb_tail(base, sb_box)
        return carry

    lax.fori_loop(0, n_super, super_body, 0)


def _softplus2(z):
    return jnp.maximum(z, 0.0) + jnp.log2(1.0 + jnp.exp2(-jnp.abs(z)))


def _sb_super_block(q_ref, k_ref, vt_ref, tri_ref, o_ref):
    blk, n = SEQ_BLOCK, N_CHAINS
    key_idx, qry_idx = _block_iotas()
    strict = key_idx < qry_idx

    def rows(ref, j):
        return ref[0, 0, pl.ds(pl.multiple_of(j * blk, blk), blk), :]

    def logits(base, specs):
        return [_dot_nt(rows(k_ref, j), rows(q_ref, base + a)) for a, j, _ in specs]

    def suffix_sums(zs, specs):
        log_betas, sps = [], []
        for z, (_, _, diag) in zip(zs, specs):
            sp = _softplus2(z)
            log_betas.append(z - sp)
            if diag:
                sp = jnp.where(strict, sp, 0.0)
            sps.append(sp.astype(BF16))
        return log_betas, sps, [_dot(tri_ref[...], sp) for sp in sps]

    def products(log_betas, sps, sufs, specs):
        ws = []
        for log_beta, suf, (_, _, diag) in zip(log_betas, sufs, specs):
            w = jnp.exp2(log_beta + suf)
            if diag:
                w = jnp.where(strict, w, 0.0)
            ws.append(w.astype(BF16))
        return [(suf[0:1, :] - sp[0:1, :].astype(F32), _dot(vt_ref[0, 0, j], w))
                for w, sp, suf, (_, j, _) in zip(ws, sps, sufs, specs)]

    def tiles(base, groups):
        zs = [logits(base, g) for g in groups]
        out, prev = [], None
        for g, z in zip(groups, zs):
            cur = suffix_sums(z, g) + (g,)
            if prev is not None:
                out += products(*prev)
            prev = cur
        return out + products(*prev)

    def band(base, d):
        return [(a, jnp.maximum(base + a - d, 0), False) for a in range(n)]

    def factor(base, a, d, c):
        return jnp.where(base + a - d >= 0, jnp.exp2(c), 0.0)

    def live(cs):
        return (jnp.max(functools.reduce(jnp.maximum, cs)) > -SB_DEAD_BITS).astype(jnp.int32)

    def stages(base):
        groups = [[(a, base + a, True) for a in range(n)], band(base, 1)]
        box = {}

        def s0():
            box["z"] = [logits(base, g) for g in groups]

        def s1():
            box["c0"] = suffix_sums(box["z"][0], groups[0])

        def s2():
            box["c1"] = suffix_sums(box["z"][1], groups[1])
            box["r0"] = products(*box["c0"], groups[0])

        def s3():
            box["r1"] = products(*box["c1"], groups[1])

        def s4():
            cs = []
            for a in range(n):
                (sum0, pv0), (sum1, pv1) = box["r0"][a], box["r1"][a]
                o_ref[0, 0, base + a] = pv0 + pv1 * factor(base, a, 1, sum0)
                cs.append(sum0 + sum1)
            box["carry"] = cs

        return [s0, s1, s2, s3, s4], box

    def tail(base, box):
        def w_body(state):
            d, _, cs = state
            res = tiles(base, [band(base, d)])
            for a, (c, (_, pv)) in enumerate(zip(cs, res)):
                o_ref[0, 0, base + a] = o_ref[0, 0, base + a] + pv * factor(base, a, d, c)
            cs = tuple(c + block_sum for c, (block_sum, _) in zip(cs, res))
            return d + 1, live(cs), cs

        cs = box["carry"]
        lax.while_loop(lambda state: jnp.logical_and(state[0] < base + n, state[1] > 0), w_body,
                       (jnp.int32(2), live(cs), tuple(cs)))

    return stages, tail


def _out_kernel(x_ref, o_mla_ref, o_sb_ref, g_mla_ref, g_sb_ref, w_o_mla_ref, w_o_sb_ref,
                g_ffn_ref, w_gate_ref, w_up_ref, w_down_ref, g_final_ref, out_ref, *, d_ff):
    blk = SEQ_BLOCK

    def group(o_ref, g_ref, w_ref):
        parts = []
        for t in range(OUT_BLOCKS):
            o = o_ref[0, :, t].reshape(-1, blk)
            y = (o * _rms_scale(o, 0) * g_ref[...]).astype(BF16)
            parts.append(_dot_tn(y, w_ref[...]))
        return jnp.concatenate(parts, axis=0)

    h = x_ref[0] + group(o_mla_ref, g_mla_ref, w_o_mla_ref) + group(o_sb_ref, g_sb_ref, w_o_sb_ref)
    f = (h * _rms_scale(h, -1) * g_ffn_ref[...]).astype(BF16)
    ffn = jnp.zeros_like(h)
    for c in range(0, d_ff, FF_CHUNK):
        gate = _dot(f, w_gate_ref[:, c:c + FF_CHUNK])
        up = _dot(f, w_up_ref[:, c:c + FF_CHUNK])
        act = (gate * jax.nn.sigmoid(gate) * up).astype(BF16)
        ffn = ffn + _dot(act, w_down_ref[c:c + FF_CHUNK, :])
    h = h + ffn
    out_ref[0] = h * _rms_scale(h, -1) * g_final_ref[...]


def _rotate_half_cols(w):
    half = w.shape[-1] // 2
    return jnp.concatenate([-w[..., half:], w[..., :half]], axis=-1)


def _head_slots(nope, rope):
    ref = nope if nope is not None else rope
    r, h = ref.shape[0], ref.shape[1]
    nope = jnp.zeros((r, h, MLA_NOPE_DIM), ref.dtype) if nope is None else nope
    rope = jnp.zeros((r, h, MLA_ROPE_DIM), ref.dtype) if rope is None else rope
    pad = jnp.zeros((r, h, MLA_PAD_DIM - MLA_QK_DIM), ref.dtype)
    return jnp.concatenate([nope, rope, pad], axis=-1).reshape(r, h * MLA_PAD_DIM)


def _const_spec(shape):
    return pl.BlockSpec(shape, lambda *_: (0,) * len(shape))


def kernel(x, positions, norm_mix, w_in, q_latent_norm, w_uq, kv_latent_norm, w_ukv,
           out_norm_mla, out_norm_sb, w_o, norm_ffn, w_gate, w_up, w_down, norm_final):
    b, s, d = x.shape
    depth = w_in.shape[0]
    d_ff = w_gate.shape[-1]
    blk = SEQ_BLOCK
    nb = s // blk
    assert s % (blk * N_CHAINS) == 0 and nb % PROJ_BLOCKS == 0 and nb % OUT_BLOCKS == 0
    assert N_CHAINS % MLA_K_PER_STEP == 0 and MLA_K_PER_STEP % 2 == 0 and d_ff % FF_CHUNK == 0

    inv_freq = ROPE_THETA ** (-jnp.arange(0, MLA_ROPE_DIM, 2, dtype=F32) / MLA_ROPE_DIM)
    half = MLA_ROPE_DIM // 2
    lane = jnp.arange(MLA_PAD_DIM)[None, :]
    on_rope = (lane >= MLA_NOPE_DIM) & (lane < MLA_QK_DIM)
    spread = (on_rope & ((lane - MLA_NOPE_DIM) % half == jnp.arange(half)[:, None])).astype(BF16)
    spread = jnp.tile(spread, (3, 1))
    off_rope = (~on_rope).astype(F32)
    tri = jnp.where(jnp.arange(blk)[None, :] > jnp.arange(blk)[:, None], -1.0, 0.0).astype(BF16)
    pos = positions.reshape(b, 1, s)

    params = pltpu.CompilerParams(
        dimension_semantics=("arbitrary", "arbitrary"), vmem_limit_bytes=VMEM_LIMIT_BYTES)

    h = x
    for l in range(depth):
        o0 = Q_LORA_RANK
        o1 = o0 + KV_LORA_RANK
        o2 = o1 + MLA_ROPE_DIM
        w_in_t = jnp.swapaxes(w_in[l], 0, 1)
        w_cq, w_ckv, w_kr, w_sb = w_in_t[:o0], w_in_t[o0:o1], w_in_t[o1:o2], w_in_t[o2:]
        kr_slot = lambda w: jnp.pad(w, ((MLA_NOPE_DIM, MLA_PAD_DIM - MLA_QK_DIM), (0, 0)))
        w_kr_rot = jnp.concatenate([-w_kr[MLA_ROPE_DIM // 2:], w_kr[:MLA_ROPE_DIM // 2]], axis=0)
        w_lat = jnp.concatenate([w_cq, w_ckv, kr_slot(w_kr), kr_slot(w_kr_rot)],
                                axis=0).astype(BF16)
        uq = w_uq[l].reshape(Q_LORA_RANK, MLA_HEADS, MLA_QK_DIM)
        uq_nope, uq_rope = uq[..., :MLA_NOPE_DIM], uq[..., MLA_NOPE_DIM:]
        w_uq_lin = _head_slots(uq_nope, uq_rope).astype(BF16)
        w_uq_rot = _head_slots(None, _rotate_half_cols(uq_rope)).astype(BF16)
        ukv = w_ukv[l].reshape(KV_LORA_RANK, MLA_HEADS, MLA_NOPE_DIM + MLA_V_DIM)
        w_uk = _head_slots(ukv[..., :MLA_NOPE_DIM], None).astype(BF16)
        w_uv = ukv[..., MLA_NOPE_DIM:].reshape(KV_LORA_RANK, MLA_WIDTH).astype(BF16)

        head_major = lambda width: jax.ShapeDtypeStruct((b, MLA_HEADS, s, width), BF16)
        tile = PROJ_BLOCKS * blk
        head_spec = lambda width: pl.BlockSpec((1, MLA_HEADS, tile, width), lambda bi, ti: (bi, 0, ti, 0))
        vt_shape = lambda rows: jax.ShapeDtypeStruct((b, MLA_HEADS, nb, rows, blk), BF16)
        vt_spec = lambda rows: pl.BlockSpec((1, MLA_HEADS, PROJ_BLOCKS, rows, blk),
                                            lambda bi, ti: (bi, 0, ti, 0, 0))
        proj_steps = nb // PROJ_BLOCKS
        n_slabs = b * proj_steps
        assert d % (n_slabs * BF16_SUBLANE_TILE) == 0 and d_ff % (n_slabs * BF16_SUBLANE_TILE) == 0
        slab_in = lambda rows, cols: pl.BlockSpec(
            (1, rows // n_slabs, cols), lambda bi, ti: (l, bi * proj_steps + ti, 0))
        slab_out = lambda rows, cols: pl.BlockSpec(
            (rows // n_slabs, cols), lambda bi, ti: (bi * proj_steps + ti, 0))
        (q_mla, k_mla, vt_mla, q_sb, k_sb, vt_sb,
         w_gate_bf, w_up_bf, w_down_bf) = pl.pallas_call(
            _proj_kernel,
            grid=(b, proj_steps),
            in_specs=[
                pl.BlockSpec((1, tile, d), lambda bi, ti: (bi, ti, 0)),
                pl.BlockSpec((1, 1, tile), lambda bi, ti: (bi, 0, ti)),
                _const_spec((half, 1)),
                _const_spec((3 * half, MLA_PAD_DIM)),
                _const_spec((1, MLA_PAD_DIM)),
                _const_spec((1, d)),
                _const_spec((1, Q_LORA_RANK)),
                _const_spec((1, KV_LORA_RANK)),
                _const_spec(w_lat.shape),
                _const_spec((3 * SB_WIDTH, d)),
                _const_spec(w_uq_lin.shape),
                _const_spec(w_uq_rot.shape),
                _const_spec(w_uk.shape),
                _const_spec(w_uv.shape),
                slab_in(d, d_ff), slab_in(d, d_ff), slab_in(d_ff, d),
            ],
            out_specs=[head_spec(MLA_PAD_DIM), head_spec(MLA_PAD_DIM), vt_spec(MLA_VT_ROWS),
                       head_spec(SB_HEAD_DIM), head_spec(SB_HEAD_DIM), vt_spec(SB_HEAD_DIM),
                       slab_out(d, d_ff), slab_out(d, d_ff), slab_out(d_ff, d)],
            out_shape=[head_major(MLA_PAD_DIM), head_major(MLA_PAD_DIM), vt_shape(MLA_VT_ROWS),
                       head_major(SB_HEAD_DIM), head_major(SB_HEAD_DIM), vt_shape(SB_HEAD_DIM),
                       jax.ShapeDtypeStruct((d, d_ff), BF16), jax.ShapeDtypeStruct((d, d_ff), BF16),
                       jax.ShapeDtypeStruct((d_ff, d), BF16)],
            compiler_params=params,
            name="proj",
        )(h, pos, inv_freq[:, None], spread, off_rope,
          norm_mix[l][None, :], q_latent_norm[l][None, :], kv_latent_norm[l][None, :],
          w_lat, w_sb.astype(BF16), w_uq_lin, w_uq_rot, w_uk, w_uv, w_gate, w_up, w_down)

        seq_spec = lambda width: pl.BlockSpec((1, 1, s, width), lambda bi, hi: (bi, hi, 0, 0))
        blocked = lambda rows: pl.BlockSpec((1, 1, nb, rows, blk), lambda bi, hi: (bi, hi, 0, 0, 0))
        o_shape = jax.ShapeDtypeStruct((b, MLA_HEADS, nb, MLA_V_DIM, blk), F32)
        score_scratch = [pltpu.VMEM((2, N_CHAINS, blk, blk), F32),
                         pltpu.VMEM((N_CHAINS, blk, blk), BF16)]
        assert MLA_HEADS == SB_HEADS
        o_mla, o_sb = pl.pallas_call(
            functools.partial(_attn_kernel, n_super=nb // N_CHAINS),
            grid=(b, MLA_HEADS),
            in_specs=[seq_spec(MLA_PAD_DIM), seq_spec(MLA_PAD_DIM), blocked(MLA_VT_ROWS),
                      seq_spec(SB_HEAD_DIM), seq_spec(SB_HEAD_DIM), blocked(SB_HEAD_DIM),
                      _const_spec((blk, blk))],
            out_specs=[blocked(MLA_V_DIM), blocked(SB_HEAD_DIM)],
            out_shape=[o_shape, o_shape],
            scratch_shapes=score_scratch,
            compiler_params=params,
            name="attn",
        )(q_mla, k_mla, vt_mla, q_sb, k_sb, vt_sb, tri)

        assert depth == 1
        o_spec = pl.BlockSpec((1, MLA_HEADS, OUT_BLOCKS, MLA_V_DIM, blk), lambda bi, ti: (bi, 0, ti, 0, 0))
        tok_spec = pl.BlockSpec((1, OUT_BLOCKS * blk, d), lambda bi, ti: (bi, ti, 0))
        resident = lambda shape: pl.BlockSpec(shape, lambda *_: (0,) * len(shape),
                                              pipeline_mode=pl.Buffered(1))
        h = pl.pallas_call(
            functools.partial(_out_kernel, d_ff=d_ff),
            grid=(b, nb // OUT_BLOCKS),
            in_specs=[
                tok_spec, o_spec, o_spec,
                _const_spec((MLA_WIDTH, 1)), _const_spec((SB_WIDTH, 1)),
                resident((MLA_WIDTH, d)), resident((SB_WIDTH, d)),
                _const_spec((1, d)),
                resident((d, d_ff)), resident((d, d_ff)), resident((d_ff, d)),
                _const_spec((1, d)),
            ],
            out_specs=tok_spec,
            out_shape=jax.ShapeDtypeStruct((b, s, d), F32),
            compiler_params=params,
            name="out_ffn",
        )(h, o_mla, o_sb, out_norm_mla[l][:, None], out_norm_sb[l][:, None],
          w_o[l][:MLA_WIDTH].astype(BF16), w_o[l][MLA_WIDTH:].astype(BF16),
          norm_ffn[l][None, :], w_gate_bf, w_up_bf, w_down_bf, norm_final[None, :])
    return h
```

```python
import functools
import math

import jax
import jax.numpy as jnp
from jax import lax
from jax.experimental import pallas as pl
from jax.experimental.pallas import tpu as pltpu

EPS = 1e-6
ROPE_THETA = 10000.0
LOG2_E = 1.4426950408889634

LANES = 128
BF16_SUBLANE_TILE = 16
MXU_DIM_V7X = 256
VMEM_BYTES_V7X = 64 * 1024 * 1024

MLA_HEADS = 8
MLA_NOPE_DIM = 64
MLA_ROPE_DIM = 32
MLA_V_DIM = 64
MLA_QK_DIM = MLA_NOPE_DIM + MLA_ROPE_DIM
MLA_PAD_DIM = LANES
MLA_VT_ROWS = MLA_V_DIM + BF16_SUBLANE_TILE
Q_LORA_RANK = 256
KV_LORA_RANK = 128
SB_HEADS = 8
SB_HEAD_DIM = 64
MLA_WIDTH = MLA_HEADS * MLA_V_DIM
SB_WIDTH = SB_HEADS * SB_HEAD_DIM

SEQ_BLOCK = MXU_DIM_V7X
N_CHAINS = 4
MLA_K_PER_STEP = 4
MLA_STEP_SIZES = (4, 2, 1)
SB_DEAD_BITS = 160.0
FF_CHUNK = MXU_DIM_V7X
OUT_BLOCKS = 2
PROJ_BLOCKS = 4
VMEM_LIMIT_BYTES = VMEM_BYTES_V7X * 7 // 8

F32 = jnp.float32
BF16 = jnp.bfloat16


def _rms_scale(v, axis):
    return lax.rsqrt(jnp.mean(v * v, axis=axis, keepdims=True) + EPS)


def _dot(a, b):
    return jnp.dot(a, b, preferred_element_type=F32)


def _dot_nt(a, b):
    return lax.dot_general(a, b, (((1,), (1,)), ((), ())), preferred_element_type=F32)


def _dot_tn(a, b):
    return lax.dot_general(a, b, (((0,), (0,)), ((), ())), preferred_element_type=F32)


def _proj_kernel(x_ref, pos_ref, freq_ref, spread_ref, off_rope_ref, g_mix_ref, g_q_ref, g_kv_ref,
                 w_lat_ref, w_sb_ref, w_uq_ref, w_uq_rot_ref, w_uk_ref, w_uv_ref,
                 w_gate_ref, w_up_ref, w_down_ref,
                 q_mla_ref, k_mla_ref, vt_mla_ref, q_sb_ref, k_sb_ref, vt_sb_ref,
                 w_gate_bf_ref, w_up_bf_ref, w_down_bf_ref):
    w_gate_bf_ref[...] = w_gate_ref[0].astype(BF16)
    w_up_bf_ref[...] = w_up_ref[0].astype(BF16)
    w_down_bf_ref[...] = w_down_ref[0].astype(BF16)

    x = x_ref[0]
    u = (x * _rms_scale(x, -1) * g_mix_ref[...]).astype(BF16)

    ang_t = freq_ref[...] * pos_ref[0].astype(F32)
    cos_t = jnp.cos(ang_t)
    sin_t = jnp.sin(ang_t)

    lat = _dot_nt(u, w_lat_ref[...])
    sb = _dot_nt(u, w_sb_ref[...])
    def spread(t):
        t1 = t.astype(BF16)
        r1 = t - t1.astype(F32)
        t2 = r1.astype(BF16)
        t3 = (r1 - t2.astype(F32)).astype(BF16)
        return _dot_tn(jnp.concatenate([t1, t2, t3], axis=0), spread_ref[...])

    cos = spread(cos_t) + off_rope_ref[...]
    sin = spread(sin_t)
    c_q = lat[:, :Q_LORA_RANK]
    c_kv = lat[:, Q_LORA_RANK:Q_LORA_RANK + KV_LORA_RANK]
    k_r = lat[:, Q_LORA_RANK + KV_LORA_RANK:Q_LORA_RANK + KV_LORA_RANK + MLA_PAD_DIM]
    k_r_rot = lat[:, Q_LORA_RANK + KV_LORA_RANK + MLA_PAD_DIM:]
    k_rope = k_r * cos + k_r_rot * sin

    ql = (c_q * _rms_scale(c_q, -1) * g_q_ref[...]).astype(BF16)
    kvl = (c_kv * _rms_scale(c_kv, -1) * g_kv_ref[...]).astype(BF16)

    q_lin = _dot(ql, w_uq_ref[...])
    q_rot = _dot(ql, w_uq_rot_ref[...])
    k_nope = _dot(kvl, w_uk_ref[...])
    v_mla = _dot(kvl, w_uv_ref[...])
    q_scale = LOG2_E / math.sqrt(MLA_QK_DIM)
    for h in range(MLA_HEADS):
        sl = slice(h * MLA_PAD_DIM, (h + 1) * MLA_PAD_DIM)
        q_h = (q_lin[:, sl] * cos + q_rot[:, sl] * sin) * q_scale
        q_mla_ref[0, h] = q_h.astype(BF16)
        k_mla_ref[0, h] = (k_nope[:, sl] + k_rope).astype(BF16)
    v_mla_t = v_mla.T
    pad_row = lax.broadcasted_iota(jnp.int32, (MLA_VT_ROWS - MLA_V_DIM, SEQ_BLOCK), 0)
    ones_then_zeros = jnp.where(pad_row == 0, 1.0, 0.0).astype(BF16)
    for h in range(MLA_HEADS):
        for t in range(PROJ_BLOCKS):
            vt_mla_ref[0, h, t, :MLA_V_DIM, :] = v_mla_t[h * MLA_V_DIM:(h + 1) * MLA_V_DIM,
                                                         t * SEQ_BLOCK:(t + 1) * SEQ_BLOCK].astype(BF16)
            vt_mla_ref[0, h, t, MLA_V_DIM:, :] = ones_then_zeros

    sb_scale = LOG2_E / math.sqrt(SB_HEAD_DIM)
    q_sb = sb[:, :SB_WIDTH] * sb_scale
    k_sb = sb[:, SB_WIDTH:2 * SB_WIDTH]
    v_sb_t = sb[:, 2 * SB_WIDTH:].T
    for h in range(SB_HEADS):
        sl = slice(h * SB_HEAD_DIM, (h + 1) * SB_HEAD_DIM)
        q_sb_ref[0, h] = q_sb[:, sl].astype(BF16)
        k_sb_ref[0, h] = k_sb[:, sl].astype(BF16)
        for t in range(PROJ_BLOCKS):
            vt_sb_ref[0, h, t] = v_sb_t[sl, t * SEQ_BLOCK:(t + 1) * SEQ_BLOCK].astype(BF16)


def _block_iotas():
    key_idx = lax.broadcasted_iota(jnp.int32, (SEQ_BLOCK, SEQ_BLOCK), 0)
    qry_idx = lax.broadcasted_iota(jnp.int32, (SEQ_BLOCK, SEQ_BLOCK), 1)
    return key_idx, qry_idx


def _attn_kernel(q_ref, k_ref, vt_ref, q_sb_ref, k_sb_ref, vt_sb_ref, tri_ref, o_ref, o_sb_ref,
                 z_scr, w_scr, *, n_super):
    blk, n, k_per_step = SEQ_BLOCK, N_CHAINS, MLA_K_PER_STEP
    key_idx, qry_idx = _block_iotas()
    causal = key_idx <= qry_idx
    unit = jnp.ones((1, blk), F32)
    sb_stages, sb_tail = _sb_super_block(q_sb_ref, k_sb_ref, vt_sb_ref, tri_ref, o_sb_ref)

    def rows(ref, j):
        return ref[0, 0, pl.ds(pl.multiple_of(j * blk, blk), blk), :]

    def softmax_blocks(zs, ms, diag_chain):
        new_ms, ws, alphas = [], [], []
        for i, (s, m) in enumerate(zip(zs, ms)):
            if i == diag_chain:
                s = jnp.where(causal, s, -jnp.inf)
            s_max = jnp.max(s, axis=0, keepdims=True)
            if m is None:
                m_new, alpha = s_max, None
            else:
                m_new = jnp.maximum(m, s_max)
                alpha = jnp.exp2(m - m_new)
            new_ms.append(m_new)
            ws.append(jnp.exp2(s - m_new).astype(BF16))
            alphas.append(alpha)
        return new_ms, ws, alphas

    def rescale_add(acc, pv, alpha):
        return pv if acc is None else alpha * acc + pv

    def super_body(sb, carry):
        base = sb * n

        def scores(j, chains):
            k = rows(k_ref, j)
            return [_dot_nt(k, rows(q_ref, base + a)) for a in chains]

        ms, accs = [None] * n, [None] * n
        side, sb_box = sb_stages(base)
        assert len(side) == n + 1
        zs_of = {n - 1: scores(base + n - 1, [n - 1])}
        side.pop(0)()
        for kb in reversed(range(n - 1)):
            zs_of[kb] = scores(base + kb, range(kb, n))
        for a, z in enumerate(scores(jnp.maximum(base - 1, 0), range(n))):
            z_scr[0, a] = z
        pending = None
        for kb in reversed(range(n)):
            chains = list(range(kb, n))
            zs = zs_of[kb]
            if pending is not None:
                vt = vt_ref[0, 0, base + kb + 1]
                for a, w, alpha in zip(*pending):
                    accs[a] = rescale_add(accs[a], _dot(vt, w), alpha)
            new_ms, ws, alphas = softmax_blocks(zs, [ms[a] for a in chains], 0)
            for a, m in zip(chains, new_ms):
                ms[a] = m
            side.pop(0)()
            if kb > 0:
                pending = (chains, ws, alphas)
            else:
                for a in range(n):
                    w_scr[a] = ws[a]
                accs[0] = jnp.zeros((vt_ref.shape[3], blk), F32)
                alphas[0] = unit

        def k_body(blocks, first):
            def body(it, state):
                ms, accs, alphas = (list(t) for t in state)
                j = first - blocks * it
                for u in range(blocks):
                    slot, j_cur, j_next = u % 2, j - u, jnp.maximum(j - u - 1, 0)
                    k_next, vt_prev = rows(k_ref, j_next), vt_ref[0, 0, j_cur + 1]
                    for a in range(n):
                        z_scr[1 - slot, a] = _dot_nt(k_next, rows(q_ref, base + a))
                        accs[a] = rescale_add(accs[a], _dot(vt_prev, w_scr[a]), alphas[a])
                    ms, ws, alphas = softmax_blocks([z_scr[slot, a] for a in range(n)], ms, None)
                    for a in range(n):
                        w_scr[a] = ws[a]
                return tuple(ms), tuple(accs), tuple(alphas)
            return body

        todo, first = base // k_per_step, base - 1
        state = (tuple(ms), tuple(accs), tuple(alphas))
        for size in MLA_STEP_SIZES:
            steps = todo // size
            state = lax.fori_loop(0, steps, k_body(size * k_per_step, first), state)
            todo, first = todo - steps * size, first - steps * size * k_per_step
        ms, accs, alphas = state
        vt = vt_ref[0, 0, 0]
        accs = [rescale_add(acc, _dot(vt, w_scr[a]), alpha)
                for a, (acc, alpha) in enumerate(zip(accs, alphas))]
        for a in range(n):
            o_ref[0, 0, base + a] = accs[a][:MLA_V_DIM] / accs[a][MLA_V_DIM:MLA_V_DIM + 1]
        sb_tail(base, sb_box)
        return carry

    lax.fori_loop(0, n_super, super_body, 0)


def _softplus2(z):
    return jnp.maximum(z, 0.0) + jnp.log2(1.0 + jnp.exp2(-jnp.abs(z)))


def _sb_super_block(q_ref, k_ref, vt_ref, tri_ref, o_ref):
    blk, n = SEQ_BLOCK, N_CHAINS
    key_idx, qry_idx = _block_iotas()
    strict = key_idx < qry_idx

    def rows(ref, j):
        return ref[0, 0, pl.ds(pl.multiple_of(j * blk, blk), blk), :]

    def logits(base, specs):
        return [_dot_nt(rows(k_ref, j), rows(q_ref, base + a)) for a, j, _ in specs]

    def suffix_sums(zs, specs):
        log_betas, sps = [], []
        for z, (_, _, diag) in zip(zs, specs):
            sp = _softplus2(z)
            log_betas.append(z - sp)
            if diag:
                sp = jnp.where(strict, sp, 0.0)
            sps.append(sp.astype(BF16))
        return log_betas, sps, [_dot(tri_ref[...], sp) for sp in sps]

    def products(log_betas, sps, sufs, specs):
        ws = []
        for log_beta, suf, (_, _, diag) in zip(log_betas, sufs, specs):
            w = jnp.exp2(log_beta + suf)
            if diag:
                w = jnp.where(strict, w, 0.0)
            ws.append(w.astype(BF16))
        return [(suf[0:1, :] - sp[0:1, :].astype(F32), _dot(vt_ref[0, 0, j], w))
                for w, sp, suf, (_, j, _) in zip(ws, sps, sufs, specs)]

    def tiles(base, groups):
        zs = [logits(base, g) for g in groups]
        out, prev = [], None
        for g, z in zip(groups, zs):
            cur = suffix_sums(z, g) + (g,)
            if prev is not None:
                out += products(*prev)
            prev = cur
        return out + products(*prev)

    def band(base, d):
        return [(a, jnp.maximum(base + a - d, 0), False) for a in range(n)]

    def factor(base, a, d, c):
        return jnp.where(base + a - d >= 0, jnp.exp2(c), 0.0)

    def live(cs):
        return (jnp.max(functools.reduce(jnp.maximum, cs)) > -SB_DEAD_BITS).astype(jnp.int32)

    def stages(base):
        groups = [[(a, base + a, True) for a in range(n)], band(base, 1)]
        box = {}

        def s0():
            box["z"] = [logits(base, g) for g in groups]

        def s1():
            box["c0"] = suffix_sums(box["z"][0], groups[0])

        def s2():
            box["c1"] = suffix_sums(box["z"][1], groups[1])
            box["r0"] = products(*box["c0"], groups[0])

        def s3():
            box["r1"] = products(*box["c1"], groups[1])

        def s4():
            cs = []
            for a in range(n):
                (sum0, pv0), (sum1, pv1) = box["r0"][a], box["r1"][a]
                o_ref[0, 0, base + a] = pv0 + pv1 * factor(base, a, 1, sum0)
                cs.append(sum0 + sum1)
            box["carry"] = cs
            box["live"] = live(cs)

        return [s0, s1, s2, s3, s4], box

    def tail(base, box):
        def w_body(state):
            d, _, cs = state
            res = tiles(base, [band(base, d)])
            for a, (c, (_, pv)) in enumerate(zip(cs, res)):
                o_ref[0, 0, base + a] = o_ref[0, 0, base + a] + pv * factor(base, a, d, c)
            cs = tuple(c + block_sum for c, (block_sum, _) in zip(cs, res))
            return d + 1, live(cs), cs

        cs = box["carry"]
        lax.while_loop(lambda state: jnp.logical_and(state[0] < base + n, state[1] > 0), w_body,
                       (jnp.int32(2), box["live"], tuple(cs)))

    return stages, tail


def _out_kernel(x_ref, o_mla_ref, o_sb_ref, g_mla_ref, g_sb_ref, w_o_mla_ref, w_o_sb_ref,
                g_ffn_ref, w_gate_ref, w_up_ref, w_down_ref, g_final_ref, out_ref, *, d_ff):
    blk = SEQ_BLOCK

    def group(o_ref, g_ref, w_ref):
        parts = []
        for t in range(OUT_BLOCKS):
            o = o_ref[0, :, t].reshape(-1, blk)
            y = (o * _rms_scale(o, 0) * g_ref[...]).astype(BF16)
            parts.append(_dot_tn(y, w_ref[...]))
        return jnp.concatenate(parts, axis=0)

    h = x_ref[0] + group(o_mla_ref, g_mla_ref, w_o_mla_ref) + group(o_sb_ref, g_sb_ref, w_o_sb_ref)
    f = (h * _rms_scale(h, -1) * g_ffn_ref[...]).astype(BF16)
    ffn = jnp.zeros_like(h)
    for c in range(0, d_ff, FF_CHUNK):
        gate = _dot(f, w_gate_ref[:, c:c + FF_CHUNK])
        up = _dot(f, w_up_ref[:, c:c + FF_CHUNK])
        act = (gate * jax.nn.sigmoid(gate) * up).astype(BF16)
        ffn = ffn + _dot(act, w_down_ref[c:c + FF_CHUNK, :])
    h = h + ffn
    out_ref[0] = h * _rms_scale(h, -1) * g_final_ref[...]


def _rotate_half_cols(w):
    half = w.shape[-1] // 2
    return jnp.concatenate([-w[..., half:], w[..., :half]], axis=-1)


def _head_slots(nope, rope):
    ref = nope if nope is not None else rope
    r, h = ref.shape[0], ref.shape[1]
    nope = jnp.zeros((r, h, MLA_NOPE_DIM), ref.dtype) if nope is None else nope
    rope = jnp.zeros((r, h, MLA_ROPE_DIM), ref.dtype) if rope is None else rope
    pad = jnp.zeros((r, h, MLA_PAD_DIM - MLA_QK_DIM), ref.dtype)
    return jnp.concatenate([nope, rope, pad], axis=-1).reshape(r, h * MLA_PAD_DIM)


def _const_spec(shape):
    return pl.BlockSpec(shape, lambda *_: (0,) * len(shape))


def kernel(x, positions, norm_mix, w_in, q_latent_norm, w_uq, kv_latent_norm, w_ukv,
           out_norm_mla, out_norm_sb, w_o, norm_ffn, w_gate, w_up, w_down, norm_final):
    b, s, d = x.shape
    depth = w_in.shape[0]
    d_ff = w_gate.shape[-1]
    blk = SEQ_BLOCK
    nb = s // blk
    assert s % (blk * N_CHAINS) == 0 and nb % PROJ_BLOCKS == 0 and nb % OUT_BLOCKS == 0
    assert N_CHAINS % MLA_K_PER_STEP == 0 and MLA_K_PER_STEP % 2 == 0 and d_ff % FF_CHUNK == 0

    inv_freq = ROPE_THETA ** (-jnp.arange(0, MLA_ROPE_DIM, 2, dtype=F32) / MLA_ROPE_DIM)
    half = MLA_ROPE_DIM // 2
    lane = jnp.arange(MLA_PAD_DIM)[None, :]
    on_rope = (lane >= MLA_NOPE_DIM) & (lane < MLA_QK_DIM)
    spread = (on_rope & ((lane - MLA_NOPE_DIM) % half == jnp.arange(half)[:, None])).astype(BF16)
    spread = jnp.tile(spread, (3, 1))
    off_rope = (~on_rope).astype(F32)
    tri = jnp.where(jnp.arange(blk)[None, :] > jnp.arange(blk)[:, None], -1.0, 0.0).astype(BF16)
    pos = positions.reshape(b, 1, s)

    params = pltpu.CompilerParams(
        dimension_semantics=("arbitrary", "arbitrary"), vmem_limit_bytes=VMEM_LIMIT_BYTES)

    h = x
    for l in range(depth):
        o0 = Q_LORA_RANK
        o1 = o0 + KV_LORA_RANK
        o2 = o1 + MLA_ROPE_DIM
        w_in_t = jnp.swapaxes(w_in[l], 0, 1)
        w_cq, w_ckv, w_kr, w_sb = w_in_t[:o0], w_in_t[o0:o1], w_in_t[o1:o2], w_in_t[o2:]
        kr_slot = lambda w: jnp.pad(w, ((MLA_NOPE_DIM, MLA_PAD_DIM - MLA_QK_DIM), (0, 0)))
        w_kr_rot = jnp.concatenate([-w_kr[MLA_ROPE_DIM // 2:], w_kr[:MLA_ROPE_DIM // 2]], axis=0)
        w_lat = jnp.concatenate([w_cq, w_ckv, kr_slot(w_kr), kr_slot(w_kr_rot)],
                                axis=0).astype(BF16)
        uq = w_uq[l].reshape(Q_LORA_RANK, MLA_HEADS, MLA_QK_DIM)
        uq_nope, uq_rope = uq[..., :MLA_NOPE_DIM], uq[..., MLA_NOPE_DIM:]
        w_uq_lin = _head_slots(uq_nope, uq_rope).astype(BF16)
        w_uq_rot = _head_slots(None, _rotate_half_cols(uq_rope)).astype(BF16)
        ukv = w_ukv[l].reshape(KV_LORA_RANK, MLA_HEADS, MLA_NOPE_DIM + MLA_V_DIM)
        w_uk = _head_slots(ukv[..., :MLA_NOPE_DIM], None).astype(BF16)
        w_uv = ukv[..., MLA_NOPE_DIM:].reshape(KV_LORA_RANK, MLA_WIDTH).astype(BF16)

        head_major = lambda width: jax.ShapeDtypeStruct((b, MLA_HEADS, s, width), BF16)
        tile = PROJ_BLOCKS * blk
        head_spec = lambda width: pl.BlockSpec((1, MLA_HEADS, tile, width), lambda bi, ti: (bi, 0, ti, 0))
        vt_shape = lambda rows: jax.ShapeDtypeStruct((b, MLA_HEADS, nb, rows, blk), BF16)
        vt_spec = lambda rows: pl.BlockSpec((1, MLA_HEADS, PROJ_BLOCKS, rows, blk),
                                            lambda bi, ti: (bi, 0, ti, 0, 0))
        proj_steps = nb // PROJ_BLOCKS
        n_slabs = b * proj_steps
        assert d % (n_slabs * BF16_SUBLANE_TILE) == 0 and d_ff % (n_slabs * BF16_SUBLANE_TILE) == 0
        slab_in = lambda rows, cols: pl.BlockSpec(
            (1, rows // n_slabs, cols), lambda bi, ti: (l, bi * proj_steps + ti, 0))
        slab_out = lambda rows, cols: pl.BlockSpec(
            (rows // n_slabs, cols), lambda bi, ti: (bi * proj_steps + ti, 0))
        (q_mla, k_mla, vt_mla, q_sb, k_sb, vt_sb,
         w_gate_bf, w_up_bf, w_down_bf) = pl.pallas_call(
            _proj_kernel,
            grid=(b, proj_steps),
            in_specs=[
                pl.BlockSpec((1, tile, d), lambda bi, ti: (bi, ti, 0)),
                pl.BlockSpec((1, 1, tile), lambda bi, ti: (bi, 0, ti)),
                _const_spec((half, 1)),
                _const_spec((3 * half, MLA_PAD_DIM)),
                _const_spec((1, MLA_PAD_DIM)),
                _const_spec((1, d)),
                _const_spec((1, Q_LORA_RANK)),
                _const_spec((1, KV_LORA_RANK)),
                _const_spec(w_lat.shape),
                _const_spec((3 * SB_WIDTH, d)),
                _const_spec(w_uq_lin.shape),
                _const_spec(w_uq_rot.shape),
                _const_spec(w_uk.shape),
                _const_spec(w_uv.shape),
                slab_in(d, d_ff), slab_in(d, d_ff), slab_in(d_ff, d),
            ],
            out_specs=[head_spec(MLA_PAD_DIM), head_spec(MLA_PAD_DIM), vt_spec(MLA_VT_ROWS),
                       head_spec(SB_HEAD_DIM), head_spec(SB_HEAD_DIM), vt_spec(SB_HEAD_DIM),
                       slab_out(d, d_ff), slab_out(d, d_ff), slab_out(d_ff, d)],
            out_shape=[head_major(MLA_PAD_DIM), head_major(MLA_PAD_DIM), vt_shape(MLA_VT_ROWS),
                       head_major(SB_HEAD_DIM), head_major(SB_HEAD_DIM), vt_shape(SB_HEAD_DIM),
                       jax.ShapeDtypeStruct((d, d_ff), BF16), jax.ShapeDtypeStruct((d, d_ff), BF16),
                       jax.ShapeDtypeStruct((d_ff, d), BF16)],
            compiler_params=params,
            name="proj",
        )(h, pos, inv_freq[:, None], spread, off_rope,
          norm_mix[l][None, :], q_latent_norm[l][None, :], kv_latent_norm[l][None, :],
          w_lat, w_sb.astype(BF16), w_uq_lin, w_uq_rot, w_uk, w_uv, w_gate, w_up, w_down)

        seq_spec = lambda width: pl.BlockSpec((1, 1, s, width), lambda bi, hi: (bi, hi, 0, 0))
        blocked = lambda rows: pl.BlockSpec((1, 1, nb, rows, blk), lambda bi, hi: (bi, hi, 0, 0, 0))
        o_shape = jax.ShapeDtypeStruct((b, MLA_HEADS, nb, MLA_V_DIM, blk), F32)
        score_scratch = [pltpu.VMEM((2, N_CHAINS, blk, blk), F32),
                         pltpu.VMEM((N_CHAINS, blk, blk), BF16)]
        assert MLA_HEADS == SB_HEADS
        o_mla, o_sb = pl.pallas_call(
            functools.partial(_attn_kernel, n_super=nb // N_CHAINS),
            grid=(b, MLA_HEADS),
            in_specs=[seq_spec(MLA_PAD_DIM), seq_spec(MLA_PAD_DIM), blocked(MLA_VT_ROWS),
                      seq_spec(SB_HEAD_DIM), seq_spec(SB_HEAD_DIM), blocked(SB_HEAD_DIM),
                      _const_spec((blk, blk))],
            out_specs=[blocked(MLA_V_DIM), blocked(SB_HEAD_DIM)],
            out_shape=[o_shape, o_shape],
            scratch_shapes=score_scratch,
            compiler_params=params,
            name="attn",
        )(q_mla, k_mla, vt_mla, q_sb, k_sb, vt_sb, tri)

        assert depth == 1
        o_spec = pl.BlockSpec((1, MLA_HEADS, OUT_BLOCKS, MLA_V_DIM, blk), lambda bi, ti: (bi, 0, ti, 0, 0))
        tok_spec = pl.BlockSpec((1, OUT_BLOCKS * blk, d), lambda bi, ti: (bi, ti, 0))
        resident = lambda shape: pl.BlockSpec(shape, lambda *_: (0,) * len(shape),
                                              pipeline_mode=pl.Buffered(1))
        h = pl.pallas_call(
            functools.partial(_out_kernel, d_ff=d_ff),
            grid=(b, nb // OUT_BLOCKS),
            in_specs=[
                tok_spec, o_spec, o_spec,
                _const_spec((MLA_WIDTH, 1)), _const_spec((SB_WIDTH, 1)),
                resident((MLA_WIDTH, d)), resident((SB_WIDTH, d)),
                _const_spec((1, d)),
                resident((d, d_ff)), resident((d, d_ff)), resident((d_ff, d)),
                _const_spec((1, d)),
            ],
            out_specs=tok_spec,
            out_shape=jax.ShapeDtypeStruct((b, s, d), F32),
            compiler_params=params,
            name="out_ffn",
        )(h, o_mla, o_sb, out_norm_mla[l][:, None], out_norm_sb[l][:, None],
          w_o[l][:MLA_WIDTH].astype(BF16), w_o[l][MLA_WIDTH:].astype(BF16),
          norm_ffn[l][None, :], w_gate_bf, w_up_bf, w_down_bf, norm_final[None, :])
    return h
```

```python
import functools
import math

import jax
import jax.numpy as jnp
from jax import lax
from jax.experimental import pallas as pl
from jax.experimental.pallas import tpu as pltpu

EPS = 1e-6
ROPE_THETA = 10000.0
LOG2_E = 1.4426950408889634

LANES = 128
BF16_SUBLANE_TILE = 16
MXU_DIM_V7X = 256
VMEM_BYTES_V7X = 64 * 1024 * 1024

MLA_HEADS = 8
MLA_NOPE_DIM = 64
MLA_ROPE_DIM = 32
MLA_V_DIM = 64
MLA_QK_DIM = MLA_NOPE_DIM + MLA_ROPE_DIM
MLA_PAD_DIM = LANES
MLA_VT_ROWS = MLA_V_DIM + BF16_SUBLANE_TILE
Q_LORA_RANK = 256
KV_LORA_RANK = 128
SB_HEADS = 8
SB_HEAD_DIM = 64
MLA_WIDTH = MLA_HEADS * MLA_V_DIM
SB_WIDTH = SB_HEADS * SB_HEAD_DIM

SEQ_BLOCK = MXU_DIM_V7X
N_CHAINS = 4
MLA_K_PER_STEP = 4
MLA_STEP_SIZES = (4, 2, 1)
SB_DEAD_BITS = 160.0
FF_CHUNK = MXU_DIM_V7X
OUT_BLOCKS = 2
PROJ_BLOCKS = 4
VMEM_LIMIT_BYTES = VMEM_BYTES_V7X * 7 // 8

F32 = jnp.float32
BF16 = jnp.bfloat16


def _rms_scale(v, axis):
    return lax.rsqrt(jnp.mean(v * v, axis=axis, keepdims=True) + EPS)


def _dot(a, b):
    return jnp.dot(a, b, preferred_element_type=F32)


def _dot_nt(a, b):
    return lax.dot_general(a, b, (((1,), (1,)), ((), ())), preferred_element_type=F32)


def _dot_tn(a, b):
    return lax.dot_general(a, b, (((0,), (0,)), ((), ())), preferred_element_type=F32)


def _proj_kernel(x_ref, pos_ref, freq_ref, spread_ref, off_rope_ref, g_mix_ref, g_q_ref, g_kv_ref,
                 w_lat_ref, w_sb_ref, w_uq_ref, w_uq_rot_ref, w_uk_ref, w_uv_ref,
                 w_gate_ref, w_up_ref, w_down_ref,
                 q_mla_ref, k_mla_ref, vt_mla_ref, q_sb_ref, k_sb_ref, vt_sb_ref,
                 w_gate_bf_ref, w_up_bf_ref, w_down_bf_ref):
    w_gate_bf_ref[...] = w_gate_ref[0].astype(BF16)
    w_up_bf_ref[...] = w_up_ref[0].astype(BF16)
    w_down_bf_ref[...] = w_down_ref[0].astype(BF16)

    x = x_ref[0]
    u = (x * _rms_scale(x, -1) * g_mix_ref[...]).astype(BF16)

    ang_t = freq_ref[...] * pos_ref[0].astype(F32)
    cos_t = jnp.cos(ang_t)
    sin_t = jnp.sin(ang_t)

    lat = _dot_nt(u, w_lat_ref[...])
    sb = _dot_nt(u, w_sb_ref[...])
    def spread(t):
        t1 = t.astype(BF16)
        r1 = t - t1.astype(F32)
        t2 = r1.astype(BF16)
        t3 = (r1 - t2.astype(F32)).astype(BF16)
        return _dot_tn(jnp.concatenate([t1, t2, t3], axis=0), spread_ref[...])

    cos = spread(cos_t) + off_rope_ref[...]
    sin = spread(sin_t)
    c_q = lat[:, :Q_LORA_RANK]
    c_kv = lat[:, Q_LORA_RANK:Q_LORA_RANK + KV_LORA_RANK]
    k_r = lat[:, Q_LORA_RANK + KV_LORA_RANK:Q_LORA_RANK + KV_LORA_RANK + MLA_PAD_DIM]
    k_r_rot = lat[:, Q_LORA_RANK + KV_LORA_RANK + MLA_PAD_DIM:]
    k_rope = k_r * cos + k_r_rot * sin

    ql = (c_q * _rms_scale(c_q, -1) * g_q_ref[...]).astype(BF16)
    kvl = (c_kv * _rms_scale(c_kv, -1) * g_kv_ref[...]).astype(BF16)

    q_lin = _dot(ql, w_uq_ref[...])
    q_rot = _dot(ql, w_uq_rot_ref[...])
    k_nope = _dot(kvl, w_uk_ref[...])
    v_mla = _dot(kvl, w_uv_ref[...])
    q_scale = LOG2_E / math.sqrt(MLA_QK_DIM)
    for h in range(MLA_HEADS):
        sl = slice(h * MLA_PAD_DIM, (h + 1) * MLA_PAD_DIM)
        q_h = (q_lin[:, sl] * cos + q_rot[:, sl] * sin) * q_scale
        q_mla_ref[0, h] = q_h.astype(BF16)
        k_mla_ref[0, h] = (k_nope[:, sl] + k_rope).astype(BF16)
    v_mla_t = v_mla.T
    pad_row = lax.broadcasted_iota(jnp.int32, (MLA_VT_ROWS - MLA_V_DIM, SEQ_BLOCK), 0)
    ones_then_zeros = jnp.where(pad_row == 0, 1.0, 0.0).astype(BF16)
    for h in range(MLA_HEADS):
        for t in range(PROJ_BLOCKS):
            vt_mla_ref[0, h, t, :MLA_V_DIM, :] = v_mla_t[h * MLA_V_DIM:(h + 1) * MLA_V_DIM,
                                                         t * SEQ_BLOCK:(t + 1) * SEQ_BLOCK].astype(BF16)
            vt_mla_ref[0, h, t, MLA_V_DIM:, :] = ones_then_zeros

    sb_scale = LOG2_E / math.sqrt(SB_HEAD_DIM)
    q_sb = sb[:, :SB_WIDTH] * sb_scale
    k_sb = sb[:, SB_WIDTH:2 * SB_WIDTH]
    v_sb_t = sb[:, 2 * SB_WIDTH:].T
    for h in range(SB_HEADS):
        sl = slice(h * SB_HEAD_DIM, (h + 1) * SB_HEAD_DIM)
        q_sb_ref[0, h] = q_sb[:, sl].astype(BF16)
        k_sb_ref[0, h] = k_sb[:, sl].astype(BF16)
        for t in range(PROJ_BLOCKS):
            vt_sb_ref[0, h, t] = v_sb_t[sl, t * SEQ_BLOCK:(t + 1) * SEQ_BLOCK].astype(BF16)


def _block_iotas():
    key_idx = lax.broadcasted_iota(jnp.int32, (SEQ_BLOCK, SEQ_BLOCK), 0)
    qry_idx = lax.broadcasted_iota(jnp.int32, (SEQ_BLOCK, SEQ_BLOCK), 1)
    return key_idx, qry_idx


def _attn_kernel(q_ref, k_ref, vt_ref, q_sb_ref, k_sb_ref, vt_sb_ref, tri_ref, o_ref, o_sb_ref,
                 z_scr, w_scr, acc_scr, *, n_super):
    blk, n, k_per_step = SEQ_BLOCK, N_CHAINS, MLA_K_PER_STEP
    key_idx, qry_idx = _block_iotas()
    causal = key_idx <= qry_idx
    unit = jnp.ones((1, blk), F32)
    sb_stages, sb_tail = _sb_super_block(q_sb_ref, k_sb_ref, vt_sb_ref, tri_ref, o_sb_ref)

    def rows(ref, j):
        return ref[0, 0, pl.ds(pl.multiple_of(j * blk, blk), blk), :]

    def softmax_blocks(zs, ms, diag_chain):
        new_ms, ws, alphas = [], [], []
        for i, (s, m) in enumerate(zip(zs, ms)):
            if i == diag_chain:
                s = jnp.where(causal, s, -jnp.inf)
            s_max = jnp.max(s, axis=0, keepdims=True)
            if m is None:
                m_new, alpha = s_max, None
            else:
                m_new = jnp.maximum(m, s_max)
                alpha = jnp.exp2(m - m_new)
            new_ms.append(m_new)
            ws.append(jnp.exp2(s - m_new).astype(BF16))
            alphas.append(alpha)
        return new_ms, ws, alphas

    def rescale_add(acc, pv, alpha):
        return pv if acc is None else alpha * acc + pv

    def super_body(sb, carry):
        base = sb * n

        def scores(j, chains):
            k = rows(k_ref, j)
            return [_dot_nt(k, rows(q_ref, base + a)) for a in chains]

        ms, accs = [None] * n, [None] * n
        side, sb_box = sb_stages(base)
        assert len(side) == n + 1
        zs_of = {n - 1: scores(base + n - 1, [n - 1])}
        side.pop(0)()
        for kb in reversed(range(n - 1)):
            zs_of[kb] = scores(base + kb, range(kb, n))
        for a, z in enumerate(scores(jnp.maximum(base - 1, 0), range(n))):
            z_scr[0, a] = z
        pending = None
        for kb in reversed(range(n)):
            chains = list(range(kb, n))
            zs = zs_of[kb]
            if pending is not None:
                vt = vt_ref[0, 0, base + kb + 1]
                for a, w, alpha in zip(*pending):
                    accs[a] = rescale_add(accs[a], _dot(vt, w), alpha)
            new_ms, ws, alphas = softmax_blocks(zs, [ms[a] for a in chains], 0)
            for a, m in zip(chains, new_ms):
                ms[a] = m
            side.pop(0)()
            if kb > 0:
                pending = (chains, ws, alphas)
            else:
                for a in range(n):
                    w_scr[a] = ws[a]
                accs[0] = jnp.zeros((vt_ref.shape[3], blk), F32)
                alphas[0] = unit

        for a in range(n):
            acc_scr[a] = accs[a]

        def k_body(blocks, first):
            def body(it, state):
                ms, alphas = (list(t) for t in state)
                j = first - blocks * it
                for u in range(blocks):
                    slot, j_cur, j_next = u % 2, j - u, jnp.maximum(j - u - 1, 0)
                    k_next, vt_prev = rows(k_ref, j_next), vt_ref[0, 0, j_cur + 1]
                    for a in range(n):
                        z_scr[1 - slot, a] = _dot_nt(k_next, rows(q_ref, base + a))
                        acc_scr[a] = rescale_add(acc_scr[a], _dot(vt_prev, w_scr[a]), alphas[a])
                    ms, ws, alphas = softmax_blocks([z_scr[slot, a] for a in range(n)], ms, None)
                    for a in range(n):
                        w_scr[a] = ws[a]
                return tuple(ms), tuple(alphas)
            return body

        todo, first = base // k_per_step, base - 1
        state = (tuple(ms), tuple(alphas))
        for size in MLA_STEP_SIZES:
            steps = todo // size
            state = lax.fori_loop(0, steps, k_body(size * k_per_step, first), state)
            todo, first = todo - steps * size, first - steps * size * k_per_step
        ms, alphas = state
        accs = [acc_scr[a] for a in range(n)]
        vt = vt_ref[0, 0, 0]
        accs = [rescale_add(acc, _dot(vt, w_scr[a]), alpha)
                for a, (acc, alpha) in enumerate(zip(accs, alphas))]
        for a in range(n):
            o_ref[0, 0, base + a] = accs[a][:MLA_V_DIM] / accs[a][MLA_V_DIM:MLA_V_DIM + 1]
        sb_tail(base, sb_box)
        return carry

    lax.fori_loop(0, n_super, super_body, 0)


def _softplus2(z):
    return jnp.maximum(z, 0.0) + jnp.log2(1.0 + jnp.exp2(-jnp.abs(z)))


def _sb_super_block(q_ref, k_ref, vt_ref, tri_ref, o_ref):
    blk, n = SEQ_BLOCK, N_CHAINS
    key_idx, qry_idx = _block_iotas()
    strict = key_idx < qry_idx

    def rows(ref, j):
        return ref[0, 0, pl.ds(pl.multiple_of(j * blk, blk), blk), :]

    def logits(base, specs):
        return [_dot_nt(rows(k_ref, j), rows(q_ref, base + a)) for a, j, _ in specs]

    def suffix_sums(zs, specs):
        log_betas, sps = [], []
        for z, (_, _, diag) in zip(zs, specs):
            sp = _softplus2(z)
            log_betas.append(z - sp)
            if diag:
                sp = jnp.where(strict, sp, 0.0)
            sps.append(sp.astype(BF16))
        return log_betas, sps, [_dot(tri_ref[...], sp) for sp in sps]

    def products(log_betas, sps, sufs, specs):
        ws = []
        for log_beta, suf, (_, _, diag) in zip(log_betas, sufs, specs):
            w = jnp.exp2(log_beta + suf)
            if diag:
                w = jnp.where(strict, w, 0.0)
            ws.append(w.astype(BF16))
        return [(suf[0:1, :] - sp[0:1, :].astype(F32), _dot(vt_ref[0, 0, j], w))
                for w, sp, suf, (_, j, _) in zip(ws, sps, sufs, specs)]

    def tiles(base, groups):
        zs = [logits(base, g) for g in groups]
        out, prev = [], None
        for g, z in zip(groups, zs):
            cur = suffix_sums(z, g) + (g,)
            if prev is not None:
                out += products(*prev)
            prev = cur
        return out + products(*prev)

    def band(base, d):
        return [(a, jnp.maximum(base + a - d, 0), False) for a in range(n)]

    def factor(base, a, d, c):
        return jnp.where(base + a - d >= 0, jnp.exp2(c), 0.0)

    def live(cs):
        return (jnp.max(functools.reduce(jnp.maximum, cs)) > -SB_DEAD_BITS).astype(jnp.int32)

    def stages(base):
        groups = [[(a, base + a, True) for a in range(n)], band(base, 1)]
        box = {}

        def s0():
            box["z"] = [logits(base, g) for g in groups]

        def s1():
            box["c0"] = suffix_sums(box["z"][0], groups[0])

        def s2():
            box["c1"] = suffix_sums(box["z"][1], groups[1])
            box["r0"] = products(*box["c0"], groups[0])

        def s3():
            box["r1"] = products(*box["c1"], groups[1])

        def s4():
            cs = []
            for a in range(n):
                (sum0, pv0), (sum1, pv1) = box["r0"][a], box["r1"][a]
                o_ref[0, 0, base + a] = pv0 + pv1 * factor(base, a, 1, sum0)
                cs.append(sum0 + sum1)
            box["carry"] = cs
            box["live"] = live(cs)

        return [s0, s1, s2, s3, s4], box

    def tail(base, box):
        def w_body(state):
            d, _, cs = state
            res = tiles(base, [band(base, d)])
            for a, (c, (_, pv)) in enumerate(zip(cs, res)):
                o_ref[0, 0, base + a] = o_ref[0, 0, base + a] + pv * factor(base, a, d, c)
            cs = tuple(c + block_sum for c, (block_sum, _) in zip(cs, res))
            return d + 1, live(cs), cs

        cs = box["carry"]
        lax.while_loop(lambda state: jnp.logical_and(state[0] < base + n, state[1] > 0), w_body,
                       (jnp.int32(2), box["live"], tuple(cs)))

    return stages, tail


def _out_kernel(x_ref, o_mla_ref, o_sb_ref, g_mla_ref, g_sb_ref, w_o_mla_ref, w_o_sb_ref,
                g_ffn_ref, w_gate_ref, w_up_ref, w_down_ref, g_final_ref, out_ref, *, d_ff):
    blk = SEQ_BLOCK

    def group(o_ref, g_ref, w_ref):
        parts = []
        for t in range(OUT_BLOCKS):
            o = o_ref[0, :, t].reshape(-1, blk)
            y = (o * _rms_scale(o, 0) * g_ref[...]).astype(BF16)
            parts.append(_dot_tn(y, w_ref[...]))
        return jnp.concatenate(parts, axis=0)

    h = x_ref[0] + group(o_mla_ref, g_mla_ref, w_o_mla_ref) + group(o_sb_ref, g_sb_ref, w_o_sb_ref)
    f = (h * _rms_scale(h, -1) * g_ffn_ref[...]).astype(BF16)
    ffn = jnp.zeros_like(h)
    for c in range(0, d_ff, FF_CHUNK):
        gate = _dot(f, w_gate_ref[:, c:c + FF_CHUNK])
        up = _dot(f, w_up_ref[:, c:c + FF_CHUNK])
        act = (gate * jax.nn.sigmoid(gate) * up).astype(BF16)
        ffn = ffn + _dot(act, w_down_ref[c:c + FF_CHUNK, :])
    h = h + ffn
    out_ref[0] = h * _rms_scale(h, -1) * g_final_ref[...]


def _rotate_half_cols(w):
    half = w.shape[-1] // 2
    return jnp.concatenate([-w[..., half:], w[..., :half]], axis=-1)


def _head_slots(nope, rope):
    ref = nope if nope is not None else rope
    r, h = ref.shape[0], ref.shape[1]
    nope = jnp.zeros((r, h, MLA_NOPE_DIM), ref.dtype) if nope is None else nope
    rope = jnp.zeros((r, h, MLA_ROPE_DIM), ref.dtype) if rope is None else rope
    pad = jnp.zeros((r, h, MLA_PAD_DIM - MLA_QK_DIM), ref.dtype)
    return jnp.concatenate([nope, rope, pad], axis=-1).reshape(r, h * MLA_PAD_DIM)


def _const_spec(shape):
    return pl.BlockSpec(shape, lambda *_: (0,) * len(shape))


def kernel(x, positions, norm_mix, w_in, q_latent_norm, w_uq, kv_latent_norm, w_ukv,
           out_norm_mla, out_norm_sb, w_o, norm_ffn, w_gate, w_up, w_down, norm_final):
    b, s, d = x.shape
    depth = w_in.shape[0]
    d_ff = w_gate.shape[-1]
    blk = SEQ_BLOCK
    nb = s // blk
    assert s % (blk * N_CHAINS) == 0 and nb % PROJ_BLOCKS == 0 and nb % OUT_BLOCKS == 0
    assert N_CHAINS % MLA_K_PER_STEP == 0 and MLA_K_PER_STEP % 2 == 0 and d_ff % FF_CHUNK == 0

    inv_freq = ROPE_THETA ** (-jnp.arange(0, MLA_ROPE_DIM, 2, dtype=F32) / MLA_ROPE_DIM)
    half = MLA_ROPE_DIM // 2
    lane = jnp.arange(MLA_PAD_DIM)[None, :]
    on_rope = (lane >= MLA_NOPE_DIM) & (lane < MLA_QK_DIM)
    spread = (on_rope & ((lane - MLA_NOPE_DIM) % half == jnp.arange(half)[:, None])).astype(BF16)
    spread = jnp.tile(spread, (3, 1))
    off_rope = (~on_rope).astype(F32)
    tri = jnp.where(jnp.arange(blk)[None, :] > jnp.arange(blk)[:, None], -1.0, 0.0).astype(BF16)
    pos = positions.reshape(b, 1, s)

    params = pltpu.CompilerParams(
        dimension_semantics=("arbitrary", "arbitrary"), vmem_limit_bytes=VMEM_LIMIT_BYTES)

    h = x
    for l in range(depth):
        o0 = Q_LORA_RANK
        o1 = o0 + KV_LORA_RANK
        o2 = o1 + MLA_ROPE_DIM
        w_in_t = jnp.swapaxes(w_in[l], 0, 1)
        w_cq, w_ckv, w_kr, w_sb = w_in_t[:o0], w_in_t[o0:o1], w_in_t[o1:o2], w_in_t[o2:]
        kr_slot = lambda w: jnp.pad(w, ((MLA_NOPE_DIM, MLA_PAD_DIM - MLA_QK_DIM), (0, 0)))
        w_kr_rot = jnp.concatenate([-w_kr[MLA_ROPE_DIM // 2:], w_kr[:MLA_ROPE_DIM // 2]], axis=0)
        w_lat = jnp.concatenate([w_cq, w_ckv, kr_slot(w_kr), kr_slot(w_kr_rot)],
                                axis=0).astype(BF16)
        uq = w_uq[l].reshape(Q_LORA_RANK, MLA_HEADS, MLA_QK_DIM)
        uq_nope, uq_rope = uq[..., :MLA_NOPE_DIM], uq[..., MLA_NOPE_DIM:]
        w_uq_lin = _head_slots(uq_nope, uq_rope).astype(BF16)
        w_uq_rot = _head_slots(None, _rotate_half_cols(uq_rope)).astype(BF16)
        ukv = w_ukv[l].reshape(KV_LORA_RANK, MLA_HEADS, MLA_NOPE_DIM + MLA_V_DIM)
        w_uk = _head_slots(ukv[..., :MLA_NOPE_DIM], None).astype(BF16)
        w_uv = ukv[..., MLA_NOPE_DIM:].reshape(KV_LORA_RANK, MLA_WIDTH).astype(BF16)

        head_major = lambda width: jax.ShapeDtypeStruct((b, MLA_HEADS, s, width), BF16)
        tile = PROJ_BLOCKS * blk
        head_spec = lambda width: pl.BlockSpec((1, MLA_HEADS, tile, width), lambda bi, ti: (bi, 0, ti, 0))
        vt_shape = lambda rows: jax.ShapeDtypeStruct((b, MLA_HEADS, nb, rows, blk), BF16)
        vt_spec = lambda rows: pl.BlockSpec((1, MLA_HEADS, PROJ_BLOCKS, rows, blk),
                                            lambda bi, ti: (bi, 0, ti, 0, 0))
        proj_steps = nb // PROJ_BLOCKS
        n_slabs = b * proj_steps
        assert d % (n_slabs * BF16_SUBLANE_TILE) == 0 and d_ff % (n_slabs * BF16_SUBLANE_TILE) == 0
        slab_in = lambda rows, cols: pl.BlockSpec(
            (1, rows // n_slabs, cols), lambda bi, ti: (l, bi * proj_steps + ti, 0))
        slab_out = lambda rows, cols: pl.BlockSpec(
            (rows // n_slabs, cols), lambda bi, ti: (bi * proj_steps + ti, 0))
        (q_mla, k_mla, vt_mla, q_sb, k_sb, vt_sb,
         w_gate_bf, w_up_bf, w_down_bf) = pl.pallas_call(
            _proj_kernel,
            grid=(b, proj_steps),
            in_specs=[
                pl.BlockSpec((1, tile, d), lambda bi, ti: (bi, ti, 0)),
                pl.BlockSpec((1, 1, tile), lambda bi, ti: (bi, 0, ti)),
                _const_spec((half, 1)),
                _const_spec((3 * half, MLA_PAD_DIM)),
                _const_spec((1, MLA_PAD_DIM)),
                _const_spec((1, d)),
                _const_spec((1, Q_LORA_RANK)),
                _const_spec((1, KV_LORA_RANK)),
                _const_spec(w_lat.shape),
                _const_spec((3 * SB_WIDTH, d)),
                _const_spec(w_uq_lin.shape),
                _const_spec(w_uq_rot.shape),
                _const_spec(w_uk.shape),
                _const_spec(w_uv.shape),
                slab_in(d, d_ff), slab_in(d, d_ff), slab_in(d_ff, d),
            ],
            out_specs=[head_spec(MLA_PAD_DIM), head_spec(MLA_PAD_DIM), vt_spec(MLA_VT_ROWS),
                       head_spec(SB_HEAD_DIM), head_spec(SB_HEAD_DIM), vt_spec(SB_HEAD_DIM),
                       slab_out(d, d_ff), slab_out(d, d_ff), slab_out(d_ff, d)],
            out_shape=[head_major(MLA_PAD_DIM), head_major(MLA_PAD_DIM), vt_shape(MLA_VT_ROWS),
                       head_major(SB_HEAD_DIM), head_major(SB_HEAD_DIM), vt_shape(SB_HEAD_DIM),
                       jax.ShapeDtypeStruct((d, d_ff), BF16), jax.ShapeDtypeStruct((d, d_ff), BF16),
                       jax.ShapeDtypeStruct((d_ff, d), BF16)],
            compiler_params=params,
            name="proj",
        )(h, pos, inv_freq[:, None], spread, off_rope,
          norm_mix[l][None, :], q_latent_norm[l][None, :], kv_latent_norm[l][None, :],
          w_lat, w_sb.astype(BF16), w_uq_lin, w_uq_rot, w_uk, w_uv, w_gate, w_up, w_down)

        seq_spec = lambda width: pl.BlockSpec((1, 1, s, width), lambda bi, hi: (bi, hi, 0, 0))
        blocked = lambda rows: pl.BlockSpec((1, 1, nb, rows, blk), lambda bi, hi: (bi, hi, 0, 0, 0))
        o_shape = jax.ShapeDtypeStruct((b, MLA_HEADS, nb, MLA_V_DIM, blk), F32)
        score_scratch = [pltpu.VMEM((2, N_CHAINS, blk, blk), F32),
                         pltpu.VMEM((N_CHAINS, blk, blk), BF16),
                         pltpu.VMEM((N_CHAINS, MLA_VT_ROWS, blk), F32)]
        assert MLA_HEADS == SB_HEADS
        o_mla, o_sb = pl.pallas_call(
            functools.partial(_attn_kernel, n_super=nb // N_CHAINS),
            grid=(b, MLA_HEADS),
            in_specs=[seq_spec(MLA_PAD_DIM), seq_spec(MLA_PAD_DIM), blocked(MLA_VT_ROWS),
                      seq_spec(SB_HEAD_DIM), seq_spec(SB_HEAD_DIM), blocked(SB_HEAD_DIM),
                      _const_spec((blk, blk))],
            out_specs=[blocked(MLA_V_DIM), blocked(SB_HEAD_DIM)],
            out_shape=[o_shape, o_shape],
            scratch_shapes=score_scratch,
            compiler_params=params,
            name="attn",
        )(q_mla, k_mla, vt_mla, q_sb, k_sb, vt_sb, tri)

        assert depth == 1
        o_spec = pl.BlockSpec((1, MLA_HEADS, OUT_BLOCKS, MLA_V_DIM, blk), lambda bi, ti: (bi, 0, ti, 0, 0))
        tok_spec = pl.BlockSpec((1, OUT_BLOCKS * blk, d), lambda bi, ti: (bi, ti, 0))
        resident = lambda shape: pl.BlockSpec(shape, lambda *_: (0,) * len(shape),
                                              pipeline_mode=pl.Buffered(1))
        h = pl.pallas_call(
            functools.partial(_out_kernel, d_ff=d_ff),
            grid=(b, nb // OUT_BLOCKS),
            in_specs=[
                tok_spec, o_spec, o_spec,
                _const_spec((MLA_WIDTH, 1)), _const_spec((SB_WIDTH, 1)),
                resident((MLA_WIDTH, d)), resident((SB_WIDTH, d)),
                _const_spec((1, d)),
                resident((d, d_ff)), resident((d, d_ff)), resident((d_ff, d)),
                _const_spec((1, d)),
            ],
            out_specs=tok_spec,
            out_shape=jax.ShapeDtypeStruct((b, s, d), F32),
            compiler_params=params,
            name="out_ffn",
        )(h, o_mla, o_sb, out_norm_mla[l][:, None], out_norm_sb[l][:, None],
          w_o[l][:MLA_WIDTH].astype(BF16), w_o[l][MLA_WIDTH:].astype(BF16),
          norm_ffn[l][None, :], w_gate_bf, w_up_bf, w_down_bf, norm_final[None, :])
    return h
```
